```python
import jax, jax.numpy as jnp
from jax import lax
import numpy as np

D_MODEL = 1024
BATCH = 8
SEQ = 2048
DEPTH = 4

F32 = jnp.float32
CTX_LEN = 256
GRID_W = 64
N_EVEN = (DEPTH + 1) // 2
N_ODD = DEPTH // 2
RMS_EPS = 1e-6
NEG_BIG = -1e30
F_MIN = 1e-30

A_WIDTH = D_MODEL
A_HEADS = 8
A_BLOCK = A_WIDTH // A_HEADS
A_CONV = 4
A_C = 8.0
B_HEADS = 16
B_KV_HEADS = 4
B_HEAD_DIM = 64
B_GROUP = B_HEADS // B_KV_HEADS
B_WIDTH = B_HEADS * B_HEAD_DIM
B_KV_WIDTH = B_KV_HEADS * B_HEAD_DIM
WINDOW = 128
ROPE_BASE = 10000.0
ATTN_SCALE = B_HEAD_DIM ** -0.5
C_EXPAND = 128
C_HEADS = D_MODEL // C_EXPAND
C_FDIM = C_HEADS * C_EXPAND
C_VDIM = D_MODEL
C_HEAD_V = C_VDIM // C_HEADS
C_CHUNK = 64

EVEN_IN = 2 * A_WIDTH + 2 * B_WIDTH + 2 * B_KV_WIDTH
EVEN_MIX = A_WIDTH + B_WIDTH
ODD_IN = 3 * C_FDIM + 2 * C_VDIM

kernel_name = 'hybrid_rglru_swa_hgrn2_dit'


def rmsnorm(x, w):
    xf = x.astype(F32)
    y = xf * lax.rsqrt(jnp.mean(xf * xf, axis=-1, keepdims=True) + RMS_EPS)
    return (y * w.astype(F32)).astype(x.dtype)


def modulate(h, shift, scale):
    return h * (1 + scale) + shift


def axial_rope_tables(n_tokens):
    rows = n_tokens // GRID_W
    row = jnp.repeat(jnp.arange(rows), GRID_W).astype(F32)
    col = jnp.tile(jnp.arange(GRID_W), rows).astype(F32)
    axis_dim = B_HEAD_DIM // 2
    inv = ROPE_BASE ** (-jnp.arange(0, axis_dim, 2, dtype=F32) / axis_dim)
    ang = jnp.concatenate([row[:, None] * inv, col[:, None] * inv], axis=-1)
    return jnp.cos(ang), jnp.sin(ang)


def apply_axial_rope(t, cos, sin):
    c = cos[None, :, None, :].astype(t.dtype)
    s = sin[None, :, None, :].astype(t.dtype)
    t1, t2 = t[..., 0::2], t[..., 1::2]
    return jnp.stack([t1 * c - t2 * s, t1 * s + t2 * c], axis=-1).reshape(t.shape)


def centred_dwconv(x, w, b):
    L = x.shape[1]
    left = (A_CONV - 1) // 2
    xp = jnp.pad(x, ((0, 0), (left, A_CONV - 1 - left), (0, 0)))
    y = b
    for k in range(A_CONV):
        y = y + xp[:, k:k + L] * w[k]
    return y


def block_diag_linear(x, w, b):
    xb = x.reshape(x.shape[0], x.shape[1], A_HEADS, A_BLOCK)
    return jnp.einsum('blhi,hij->blhj', xb, w).reshape(x.shape) + b


def rglru_coeffs(u, wx, bx, wa, ba, lam):
    gate_x = jax.nn.sigmoid(block_diag_linear(u, wx, bx))
    gate_a = jax.nn.sigmoid(block_diag_linear(u, wa, ba))
    log_a = -A_C * gate_a * jax.nn.softplus(-lam.astype(F32))
    mult = jnp.sqrt(-jnp.expm1(2 * log_a))
    return jnp.exp(log_a), mult * gate_x * u


def linear_scan(a, b, h0, reverse):
    def combine(e1, e2):
        a1, b1 = e1
        a2, b2 = e2
        return a1 * a2, a2 * b1 + b2
    a_cum, b_cum = lax.associative_scan(combine, (a, b), reverse=reverse, axis=1)
    return b_cum + a_cum * h0[:, None, :]


def rglru_mixer(xa_ctx, xa_lat, conv_w, conv_b, wx, bx, wa, ba, lam, with_ctx):
    dt = xa_lat.dtype
    u_ctx = centred_dwconv(xa_ctx, conv_w, conv_b).astype(F32)
    u_lat = centred_dwconv(xa_lat, conv_w, conv_b).astype(F32)
    zeros = jnp.zeros((u_ctx.shape[0], A_WIDTH), F32)
    y_ctx, y_lat = [], []
    for d, rev in enumerate((False, True)):
        a_c, b_c = rglru_coeffs(u_ctx, wx[d], bx[d], wa[d], ba[d], lam[d])
        h_c = linear_scan(a_c, b_c, zeros, rev)
        h0 = h_c[:, 0] if rev else h_c[:, -1]
        a_l, b_l = rglru_coeffs(u_lat, wx[d], bx[d], wa[d], ba[d], lam[d])
        y_lat.append(linear_scan(a_l, b_l, h0, rev))
        if with_ctx:
            y_ctx.append(h_c)
    out_ctx = (y_ctx[0] + y_ctx[1]).astype(dt) if with_ctx else None
    return out_ctx, (y_lat[0] + y_lat[1]).astype(dt)


def latent_window_attention(q, k, v, k_ctx, v_ctx, sink):
    Bsz, S = q.shape[0], q.shape[1]
    nb = S // WINDOW
    qb = q.reshape(Bsz, nb, WINDOW, B_KV_HEADS, B_GROUP, B_HEAD_DIM)

    def band(t):
        tp = jnp.pad(t, ((0, 0), (WINDOW, WINDOW), (0, 0), (0, 0)))
        tp = tp.reshape(Bsz, nb + 2, WINDOW, B_KV_HEADS, B_HEAD_DIM)
        return jnp.concatenate([tp[:, :-2], tp[:, 1:-1], tp[:, 2:]], axis=2)

    kw, vw = band(k), band(v)
    s_win = jnp.einsum('bnqhgd,bnkhd->bnhgqk', qb, kw).astype(F32) * ATTN_SCALE
    blk = jnp.arange(nb)[:, None, None] * WINDOW
    q_pos = blk + jnp.arange(WINDOW)[None, :, None]
    k_pos = blk - WINDOW + jnp.arange(3 * WINDOW)[None, None, :]
    valid = (jnp.abs(q_pos - k_pos) <= WINDOW) & (k_pos >= 0) & (k_pos < S)
    s_win = jnp.where(valid[None, :, None, None], s_win, NEG_BIG)
    s_ctx = jnp.einsum('bnqhgd,bchd->bnhgqc', qb, k_ctx).astype(F32) * ATTN_SCALE
    snk = sink.astype(F32).reshape(1, 1, B_KV_HEADS, B_GROUP, 1, 1)
    m = jnp.maximum(jnp.maximum(s_win.max(-1, keepdims=True), s_ctx.max(-1, keepdims=True)), snk)
    p_win = jnp.exp(s_win - m)
    p_ctx = jnp.exp(s_ctx - m)
    denom = p_win.sum(-1, keepdims=True) + p_ctx.sum(-1, keepdims=True) + jnp.exp(snk - m)
    o = (jnp.einsum('bnhgqk,bnkhd->bnhgqd', p_win, vw.astype(F32))
         + jnp.einsum('bnhgqc,bchd->bnhgqd', p_ctx, v_ctx.astype(F32))) / denom
    return o.transpose(0, 1, 4, 2, 3, 5).reshape(Bsz, S, B_WIDTH).astype(q.dtype)


def context_attention(q, k, v, sink):
    Bsz, L = q.shape[0], q.shape[1]
    qc = q.reshape(Bsz, L, B_KV_HEADS, B_GROUP, B_HEAD_DIM)
    s = jnp.einsum('bqhgd,bkhd->bhgqk', qc, k).astype(F32) * ATTN_SCALE
    snk = jnp.broadcast_to(sink.astype(F32).reshape(1, B_KV_HEADS, B_GROUP, 1, 1), s.shape[:-1] + (1,))
    p = jax.nn.softmax(jnp.concatenate([s, snk], axis=-1), axis=-1)[..., :-1]
    o = jnp.einsum('bhgqk,bkhd->bqhgd', p, v.astype(F32))
    return o.reshape(Bsz, L, B_WIDTH).astype(q.dtype)


def even_mixer(u_ctx, u_lat, cos, sin, w_in, conv_w, conv_b, rg_wx, rg_bx, rg_wa, rg_ba, rg_lam,
               sink, w_out, with_ctx):
    n_state = A_WIDTH + 2 * B_KV_WIDTH
    p_lat = u_lat @ w_in
    p_ctx = u_ctx @ (w_in if with_ctx else w_in[:, :n_state])

    def kv(p):
        sh = p.shape[:2] + (B_KV_HEADS, B_HEAD_DIM)
        return (p[..., A_WIDTH:A_WIDTH + B_KV_WIDTH].reshape(sh),
                p[..., A_WIDTH + B_KV_WIDTH:n_state].reshape(sh))

    def gates_q(p):
        o = n_state
        ga = p[..., o:o + A_WIDTH]
        q = p[..., o + A_WIDTH:o + A_WIDTH + B_WIDTH].reshape(p.shape[:2] + (B_HEADS, B_HEAD_DIM))
        gb = p[..., o + A_WIDTH + B_WIDTH:]
        return ga, q, gb

    k_c, v_c = kv(p_ctx)
    k_l, v_l = kv(p_lat)
    ga_l, q_l, gb_l = gates_q(p_lat)
    q_l = apply_axial_rope(q_l, cos, sin)
    k_l = apply_axial_rope(k_l, cos, sin)
    ya_c, ya_l = rglru_mixer(p_ctx[..., :A_WIDTH], p_lat[..., :A_WIDTH], conv_w, conv_b,
                             rg_wx, rg_bx, rg_wa, rg_ba, rg_lam, with_ctx)
    yb_l = latent_window_attention(q_l, k_l, v_l, k_c, v_c, sink)
    out_l = jnp.concatenate([ya_l * jax.nn.silu(ga_l), yb_l * jax.nn.silu(gb_l)], axis=-1) @ w_out
    out_c = None
    if with_ctx:
        ga_c, q_c, gb_c = gates_q(p_ctx)
        yb_c = context_attention(q_c, k_c, v_c, sink)
        out_c = jnp.concatenate([ya_c * jax.nn.silu(ga_c), yb_c * jax.nn.silu(gb_c)], axis=-1) @ w_out
    return out_c, out_l


def gla_chunk_scan(q, k, g, v, s0):
    Bsz, L, H, _ = k.shape
    dv = v.shape[-1]
    n = L // C_CHUNK

    def chunks(t):
        return t.astype(F32).reshape(Bsz, n, C_CHUNK, H, t.shape[-1]).transpose(1, 0, 3, 2, 4)

    lower = jnp.tril(jnp.ones((C_CHUNK, C_CHUNK), dtype=bool))[:, :, None]

    def advance(S, kc, gc, vc):
        b = jnp.cumsum(gc, axis=2)
        b_last = b[:, :, -1:]
        S_new = (jnp.exp(b_last)[:, :, 0, :, None] * S
                 + jnp.einsum('bhsd,bhse->bhde', kc * jnp.exp(b_last - b), vc))
        return b, S_new

    if q is None:
        def step_state(S, inp):
            _, S_new = advance(S, *inp)
            return S_new, None
        s_fin, _ = lax.scan(step_state, s0, (chunks(k), chunks(g), chunks(v)))
        return None, s_fin

    def step(S, inp):
        qc, kc, gc, vc = inp
        b, S_new = advance(S, kc, gc, vc)
        decay = jnp.exp(jnp.where(lower, b[:, :, :, None] - b[:, :, None], NEG_BIG))
        attn = jnp.einsum('bhtd,bhsd,bhtsd->bhts', qc, kc, decay)
        o = (jnp.einsum('bhtd,bhde->bhte', qc * jnp.exp(b), S)
             + jnp.einsum('bhts,bhse->bhte', attn, vc))
        return S_new, o

    s_fin, o = lax.scan(step, s0, (chunks(q), chunks(k), chunks(g), chunks(v)))
    return o.transpose(1, 0, 3, 2, 4).reshape(Bsz, L, H, dv), s_fin


def maybe_flip(t, rev):
    return jnp.flip(t, axis=1) if rev else t


def odd_mixer(u_ctx, u_lat, w_in, lb, gnorm_w, w_out, with_ctx):
    dt = u_lat.dtype
    n_state = 2 * C_FDIM + C_VDIM
    p_lat = u_lat @ w_in
    p_ctx = u_ctx @ (w_in if with_ctx else w_in[:, :n_state])
    lb = lb.reshape(C_HEADS, C_EXPAND)

    def heads(t):
        return t.reshape(t.shape[0], t.shape[1], C_HEADS, -1)

    def forget(z):
        z = heads(z).astype(F32)
        f = lb + (1 - lb) * jax.nn.sigmoid(z)
        return (1 - lb) * jax.nn.sigmoid(-z), jnp.log(jnp.maximum(f, F_MIN))

    def state_parts(p):
        return p[..., :C_FDIM], p[..., C_FDIM:2 * C_FDIM], heads(p[..., 2 * C_FDIM:n_state])

    def out_parts(p):
        return jax.nn.silu(heads(p[..., n_state:n_state + C_FDIM])), p[..., n_state + C_FDIM:]

    ff_c, fb_c, v_c = state_parts(p_ctx)
    ff_l, fb_l, v_l = state_parts(p_lat)
    q_l, og_l = out_parts(p_lat)
    q_c, og_c = out_parts(p_ctx) if with_ctx else (None, None)
    zeros = jnp.zeros((u_lat.shape[0], C_HEADS, C_EXPAND, C_HEAD_V), F32)
    o_ctx, o_lat = [], []
    for f_c, f_l, rev in ((ff_c, ff_l, False), (fb_c, fb_l, True)):
        k_c, g_c = forget(f_c)
        k_l, g_l = forget(f_l)
        oc, s_c = gla_chunk_scan(maybe_flip(q_c, rev) if with_ctx else None, maybe_flip(k_c, rev),
                                 maybe_flip(g_c, rev), maybe_flip(v_c, rev), zeros)
        ol, _ = gla_chunk_scan(maybe_flip(q_l, rev), maybe_flip(k_l, rev), maybe_flip(g_l, rev),
                               maybe_flip(v_l, rev), s_c)
        o_lat.append(maybe_flip(ol, rev))
        if with_ctx:
            o_ctx.append(maybe_flip(oc, rev))

    def readout(o, og):
        y = rmsnorm(o, gnorm_w).reshape(o.shape[0], o.shape[1], C_VDIM) * jax.nn.silu(og.astype(F32))
        return y.astype(dt) @ w_out

    out_c = readout(o_ctx[0] + o_ctx[1], og_c) if with_ctx else None
    return out_c, readout(o_lat[0] + o_lat[1], og_l)


def setup_inputs(seed: int = 0) -> dict:
    key = jax.random.key(seed)
    ks = jax.random.split(key, 24)
    D = D_MODEL

    def nrm(k, shape, s):
        return jax.random.normal(k, shape, F32) * s

    u = jax.random.uniform(ks[14], (N_EVEN, 2, A_WIDTH), F32, 0.9, 0.999)
    sig = u ** (1.0 / A_C)
    return {
        'x': nrm(ks[0], (BATCH, SEQ, D), 1.0),
        'c': nrm(ks[1], (BATCH, D), 1.0),
        'ctx': nrm(ks[2], (BATCH, CTX_LEN, D), 1.0),
        'c_ctx': nrm(ks[3], (D,), 1.0),
        'ada_w': nrm(ks[4], (DEPTH, D, 3 * D), 0.5 * D ** -0.5),
        'ada_b': nrm(ks[5], (DEPTH, 3 * D), 0.02),
        'norm_w': 1.0 + nrm(ks[6], (DEPTH, D), 0.05),
        'ev_w_in': nrm(ks[7], (N_EVEN, D, EVEN_IN), D ** -0.5),
        'ev_conv_w': nrm(ks[8], (N_EVEN, A_CONV, A_WIDTH), A_CONV ** -0.5),
        'ev_conv_b': nrm(ks[9], (N_EVEN, A_WIDTH), 0.02),
        'ev_rg_wx': nrm(ks[10], (N_EVEN, 2, A_HEADS, A_BLOCK, A_BLOCK), A_BLOCK ** -0.5),
        'ev_rg_bx': nrm(ks[11], (N_EVEN, 2, A_WIDTH), 0.02),
        'ev_rg_wa': nrm(ks[12], (N_EVEN, 2, A_HEADS, A_BLOCK, A_BLOCK), A_BLOCK ** -0.5),
        'ev_rg_ba': nrm(ks[13], (N_EVEN, 2, A_WIDTH), 0.02),
        'ev_rg_lambda': jnp.log(sig) - jnp.log1p(-sig),
        'ev_sink': nrm(ks[15], (N_EVEN, B_HEADS), 0.5),
        'ev_w_out': nrm(ks[16], (N_EVEN, EVEN_MIX, D), EVEN_MIX ** -0.5),
        'od_w_in': nrm(ks[17], (N_ODD, D, ODD_IN), D ** -0.5),
        'od_lb_raw': nrm(ks[18], (N_ODD, C_FDIM), 1.0),
        'od_gnorm_w': 1.0 + nrm(ks[19], (N_ODD, C_HEAD_V), 0.05),
        'od_w_out': nrm(ks[20], (N_ODD, C_VDIM, D), C_VDIM ** -0.5),
        'final_norm_w': 1.0 + nrm(ks[21], (D,), 0.05),
    }


def reference(x, c, ctx, c_ctx, ada_w, ada_b, norm_w, ev_w_in, ev_conv_w, ev_conv_b, ev_rg_wx,
              ev_rg_bx, ev_rg_wa, ev_rg_ba, ev_rg_lambda, ev_sink, ev_w_out, od_w_in, od_lb_raw,
              od_gnorm_w, od_w_out, final_norm_w):
    D = D_MODEL
    cos, sin = axial_rope_tables(x.shape[1])
    silu_c = jax.nn.silu(c)
    silu_cc = jax.nn.silu(c_ctx)
    lb_p = jax.nn.softmax(od_lb_raw.astype(F32), axis=0)
    lower_bounds = jnp.cumsum(lb_p, axis=0) - lb_p[0]
    h_lat, h_ctx = x, ctx
    for layer in range(DEPTH):
        last = layer == DEPTH - 1
        j = layer // 2
        mod_l = silu_c @ ada_w[layer] + ada_b[layer]
        sh_l, sc_l, gt_l = jnp.split(mod_l[:, None, :], 3, axis=-1)
        n_mod = 2 * D if last else 3 * D
        mod_c = silu_cc @ ada_w[layer][:, :n_mod] + ada_b[layer][:n_mod]
        u_lat = modulate(rmsnorm(h_lat, norm_w[layer]), sh_l, sc_l)
        u_ctx = modulate(rmsnorm(h_ctx, norm_w[layer]), mod_c[:D], mod_c[D:2 * D])
        if layer % 2 == 0:
            m_ctx, m_lat = even_mixer(u_ctx, u_lat, cos, sin, ev_w_in[j], ev_conv_w[j], ev_conv_b[j],
                                      ev_rg_wx[j], ev_rg_bx[j], ev_rg_wa[j], ev_rg_ba[j],
                                      ev_rg_lambda[j], ev_sink[j], ev_w_out[j], not last)
        else:
            m_ctx, m_lat = odd_mixer(u_ctx, u_lat, od_w_in[j], lower_bounds[j], od_gnorm_w[j],
                                     od_w_out[j], not last)
        h_lat = h_lat + gt_l * m_lat
        if not last:
            h_ctx = h_ctx + mod_c[2 * D:] * m_ctx
    return rmsnorm(h_lat, final_norm_w)
```

```python
import functools

import jax
import jax.numpy as jnp
import numpy as np
from jax import lax
from jax.experimental import pallas as pl
from jax.experimental.pallas import tpu as pltpu

F32 = jnp.float32
BF16 = jnp.bfloat16

D_MODEL = 1024
DEPTH = 4
GRID_W = 64
RMS_EPS = 1e-6
NEG_BIG = -1e30
F_MIN = 1e-30

A_WIDTH = D_MODEL
A_HEADS = 8
A_BLOCK = A_WIDTH // A_HEADS
A_CONV = 4
A_C = 8.0
B_HEADS = 16
B_KV_HEADS = 4
B_HEAD_DIM = 64
B_GROUP = B_HEADS // B_KV_HEADS
B_WIDTH = B_HEADS * B_HEAD_DIM
B_KV_WIDTH = B_KV_HEADS * B_HEAD_DIM
WINDOW = 128
ROPE_BASE = 10000.0
ATTN_SCALE = B_HEAD_DIM ** -0.5
C_EXPAND = 128
C_HEADS = D_MODEL // C_EXPAND
C_CHUNK = 64

SUBLANES = 8
LANES = 128
ROW_TILE = 256
VMEM_LIMIT = 56 * 1024 * 1024

_NT = (((1,), (1,)), ((), ()))
_TN = (((0,), (0,)), ((), ()))


def _params(*sem):
    return pltpu.CompilerParams(dimension_semantics=sem, vmem_limit_bytes=VMEM_LIMIT)


def _sigmoid(x):
    return 1.0 / (1.0 + jnp.exp(-x))


def _silu(x):
    return x * _sigmoid(x)


def _ada_kernel(sc_ref, w_ref, b_ref, o_ref):
    o_ref[...] = jnp.dot(sc_ref[...], w_ref[...], preferred_element_type=F32,
                         precision=lax.Precision.HIGHEST) + b_ref[...]


def _ada_mod(sc, ada_w, ada_b):
    depth, d, n = ada_w.shape
    rows = sc.shape[0]
    tn = 1024
    return pl.pallas_call(
        _ada_kernel,
        grid=(depth, n // tn),
        in_specs=[pl.BlockSpec((rows, d), lambda l, j: (0, 0)),
                  pl.BlockSpec((None, d, tn), lambda l, j: (l, 0, j)),
                  pl.BlockSpec((None, 1, tn), lambda l, j: (l, 0, j))],
        out_specs=pl.BlockSpec((None, rows, tn), lambda l, j: (l, 0, j)),
        out_shape=jax.ShapeDtypeStruct((depth, rows, n), F32),
        compiler_params=_params("parallel", "parallel"),
        name="ada_mod",
    )(sc, ada_w, ada_b.reshape(depth, 1, n))


def _inproj_kernel(h_ref, sh_ref, sc_ref, nw_ref, w_ref, *rest, emit_u, col_chunk):
    if emit_u:
        p_ref, u_ref, u_scr = rest
    else:
        p_ref, u_scr = rest
    x = h_ref[...]
    tm, d = x.shape
    y = x * lax.rsqrt(jnp.mean(x * x, axis=-1, keepdims=True) + RMS_EPS) * nw_ref[...]
    y3 = y.reshape(tm // SUBLANES, SUBLANES, d)
    u = (y3 * (1.0 + sc_ref[...])[None] + sh_ref[...][None]).reshape(tm, d)
    ub = u.astype(BF16)
    u_scr[...] = ub
    if emit_u:
        u_ref[...] = ub
    n = p_ref.shape[-1]
    for c in range(0, n, col_chunk):
        p_ref[:, c:c + col_chunk] = jnp.dot(u_scr[...], w_ref[:, c:c + col_chunk],
                                            preferred_element_type=F32)


def _inproj(h, shift, scale, norm_w, w, n_ctx_tiles, emit_u):
    b, t, d = h.shape
    n = w.shape[1]
    tm = ROW_TILE

    def grp(i, j):
        return jnp.where(j < n_ctx_tiles, b, i)

    out_shape = [jax.ShapeDtypeStruct((b, t, n), F32)]
    out_specs = [pl.BlockSpec((None, tm, n), lambda i, j: (i, j, 0))]
    if emit_u:
        out_shape.append(jax.ShapeDtypeStruct((b, t, d), BF16))
        out_specs.append(pl.BlockSpec((None, tm, d), lambda i, j: (i, j, 0)))
    res = pl.pallas_call(
        functools.partial(_inproj_kernel, emit_u=emit_u, col_chunk=512),
        grid=(b, t // tm),
        in_specs=[pl.BlockSpec((None, tm, d), lambda i, j: (i, j, 0)),
                  pl.BlockSpec((None, SUBLANES, d), lambda i, j: (grp(i, j), 0, 0)),
                  pl.BlockSpec((None, SUBLANES, d), lambda i, j: (grp(i, j), 0, 0)),
                  pl.BlockSpec((1, d), lambda i, j: (0, 0)),
                  pl.BlockSpec((d, n), lambda i, j: (0, 0))],
        out_specs=out_specs,
        out_shape=out_shape,
        scratch_shapes=[pltpu.VMEM((tm, d), BF16)],
        compiler_params=_params("parallel", "parallel"),
        name="inproj",
    )(h, shift, scale, norm_w.reshape(1, d), w)
    return res if emit_u else res[0]


def _matmul_kernel(u_ref, w_ref, p_ref, *, col_chunk):
    n = p_ref.shape[-1]
    for c in range(0, n, col_chunk):
        p_ref[:, c:c + col_chunk] = jnp.dot(u_ref[...], w_ref[:, c:c + col_chunk],
                                            preferred_element_type=F32)


def _matmul(u, w):
    r, d = u.shape
    n = w.shape[1]
    tm = 512
    return pl.pallas_call(
        functools.partial(_matmul_kernel, col_chunk=512),
        grid=(r // tm,),
        in_specs=[pl.BlockSpec((tm, d), lambda i: (i, 0)),
                  pl.BlockSpec((d, n), lambda i: (0, 0))],
        out_specs=pl.BlockSpec((tm, n), lambda i: (i, 0)),
        out_shape=jax.ShapeDtypeStruct((r, n), F32),
        compiler_params=_params("parallel"),
        name="matmul_tm",
    )(u, w)


def _scan_tile(i, n_ctx, n_lat, rev):
    if not rev:
        return i
    return jnp.where(i < n_ctx, n_ctx - 1 - i, 2 * n_ctx + n_lat - 1 - i)


def _rglru_kernel(*refs, rev, tt, n_ctx, n_lat):
    if rev:
        (xa_ref, xp_ref, xn1_ref, xn2_ref, ga_ref, yf_ref, cw_ref, cb_ref, w_ref, bx_ref, ba_ref,
         la_ref, out_ref, h_scr, xpad_scr, u_scr, a_scr, b_scr) = refs
    else:
        (xa_ref, xp_ref, xn1_ref, xn2_ref, cw_ref, cb_ref, w_ref, bx_ref, ba_ref,
         la_ref, out_ref, h_scr, xpad_scr, u_scr, a_scr, b_scr) = refs
    i = pl.program_id(0)
    tile = _scan_tile(i, n_ctx, n_lat, rev)
    seg_first = jnp.logical_or(tile == 0, tile == n_ctx)
    seg_last = jnp.logical_or(tile == n_ctx - 1, tile == n_ctx + n_lat - 1)
    rows = tt * SUBLANES

    @pl.when(i == 0)
    def _():
        h_scr[...] = jnp.zeros_like(h_scr)

    xpad_scr[0:8, :] = jnp.where(seg_first, 0.0, xp_ref[...])
    xpad_scr[8:8 + rows, :] = xa_ref[...]
    xpad_scr[8 + rows:16 + rows, :] = jnp.where(seg_last, 0.0, xn1_ref[...])
    xpad_scr[16 + rows:24 + rows, :] = jnp.where(seg_last, 0.0, xn2_ref[...])
    u = cb_ref[...] + xpad_scr[0:rows, :] * cw_ref[0:1, :]
    for k in range(1, A_CONV):
        u = u + xpad_scr[8 * k:8 * k + rows, :] * cw_ref[k:k + 1, :]
    u_scr[...] = u

    for hd in range(A_HEADS):
        sl = slice(hd * A_BLOCK, (hd + 1) * A_BLOCK)
        uh = u_scr[:, sl]
        g = jnp.dot(uh.astype(BF16), w_ref[hd], preferred_element_type=F32)
        gate_x = _sigmoid(g[:, :A_BLOCK] + bx_ref[:, sl])
        gate_a = _sigmoid(g[:, A_BLOCK:] + ba_ref[:, sl])
        log_a = gate_a * la_ref[:, sl]
        a = jnp.exp(log_a)
        a_scr[:, sl] = a
        b_scr[:, sl] = jnp.sqrt(-jnp.tanh(log_a) * (1.0 + a * a)) * gate_x * uh

    def step(s, h):
        t = (tt - 1 - s) if rev else s
        r0 = pl.multiple_of(t * SUBLANES, SUBLANES)
        h = a_scr[pl.ds(r0, SUBLANES), :] * h + b_scr[pl.ds(r0, SUBLANES), :]
        b_scr[pl.ds(r0, SUBLANES), :] = h
        return h

    h_scr[...] = lax.fori_loop(0, tt, step, h_scr[...], unroll=8)
    if rev:
        out_ref[...] = ((yf_ref[...] + b_scr[...]) * _silu(ga_ref[...])).astype(BF16)
    else:
        out_ref[...] = b_scr[...]


def _rglru_pass(p_a, yf, conv_w, conv_b, w_cat, bx, ba, la, t_ctx, t_lat, rev):
    w = A_WIDTH
    tt = 64
    rows = tt * SUBLANES
    n_ctx, n_lat = t_ctx // tt, t_lat // tt
    t_all = t_ctx + t_lat
    tile = functools.partial(_scan_tile, n_ctx=n_ctx, n_lat=n_lat, rev=rev)
    const = lambda i: (0, 0)
    in_specs = [pl.BlockSpec((rows, w), lambda i: (tile(i), 0)),
                pl.BlockSpec((SUBLANES, w), lambda i: (jnp.maximum(tile(i) * tt - 1, 0), 0)),
                pl.BlockSpec((SUBLANES, w), lambda i: (jnp.minimum(tile(i) * tt + tt, t_all - 1), 0)),
                pl.BlockSpec((SUBLANES, w), lambda i: (jnp.minimum(tile(i) * tt + tt + 1, t_all - 1), 0))]
    args = [p_a, p_a, p_a, p_a]
    if rev:
        in_specs += [pl.BlockSpec((rows, w), lambda i: (tile(i), 1)),
                     pl.BlockSpec((rows, w), lambda i: (tile(i), 0))]
        args += [p_a, yf]
    in_specs += [pl.BlockSpec((A_CONV, w), const), pl.BlockSpec((1, w), const),
                 pl.BlockSpec((A_HEADS, A_BLOCK, 2 * A_BLOCK), lambda i: (0, 0, 0)),
                 pl.BlockSpec((1, w), const), pl.BlockSpec((1, w), const), pl.BlockSpec((1, w), const)]
    args += [conv_w, conv_b, w_cat, bx, ba, la]
    return pl.pallas_call(
        functools.partial(_rglru_kernel, rev=rev, tt=tt, n_ctx=n_ctx, n_lat=n_lat),
        grid=(n_ctx + n_lat,),
        in_specs=in_specs,
        out_specs=pl.BlockSpec((rows, w), lambda i: (tile(i), 0)),
        out_shape=jax.ShapeDtypeStruct((t_all * SUBLANES, w), BF16 if rev else F32),
        scratch_shapes=[pltpu.VMEM((SUBLANES, w), F32),
                        pltpu.VMEM((rows + 3 * SUBLANES, w), F32),
                        pltpu.VMEM((rows, w), F32),
                        pltpu.VMEM((rows, w), F32),
                        pltpu.VMEM((rows, w), F32)],
        compiler_params=_params("arbitrary"),
        name="rglru_bwd" if rev else "rglru_fwd",
    )(*args)


def _rope(x, cos, sin_a, sin_b):
    return x * cos + pltpu.roll(x, 96, 1) * sin_a + pltpu.roll(x, 32, 1) * sin_b


def _rope_kv_kernel(kv_ref, cos_ref, sa_ref, sb_ref, o_ref):
    cos, sa, sb = cos_ref[...], sa_ref[...], sb_ref[...]
    for g in range(B_KV_HEADS):
        sl = slice(g * LANES, (g + 1) * LANES)
        o_ref[:, sl] = _rope(kv_ref[:, sl], cos, sa, sb).astype(BF16)


def _rope_kv(p_b, tabs, col_block):
    b, t, _ = p_b.shape
    tm = ROW_TILE
    wkv = 2 * B_KV_WIDTH
    tab = pl.BlockSpec((tm, LANES), lambda i, j: (j, 0))
    return pl.pallas_call(
        _rope_kv_kernel,
        grid=(b, t // tm),
        in_specs=[pl.BlockSpec((None, tm, wkv), lambda i, j: (i, j, col_block)), tab, tab, tab],
        out_specs=pl.BlockSpec((None, tm, wkv), lambda i, j: (i, j, 0)),
        out_shape=jax.ShapeDtypeStruct((b, t, wkv), BF16),
        compiler_params=_params("parallel", "parallel"),
        name="rope_kv",
    )(p_b, *tabs)


def _attn_kernel(sink_ref, q_ref, kc_ref, kp_ref, ko_ref, kn_ref, gb_ref, cos_ref, sa_ref, sb_ref,
                 o_ref, *, n_ctx_blk, n_lat_blk):
    g = pl.program_id(1)
    n = pl.program_id(2)
    m = n - n_ctx_blk
    is_lat = n >= n_ctx_blk
    has_prev = jnp.logical_and(is_lat, m >= 1)
    has_next = jnp.logical_and(is_lat, m <= n_lat_blk - 2)
    w = WINDOW
    dh = B_HEAD_DIM
    lc = kc_ref.shape[0]

    cos, sa, sb = cos_ref[...], sa_ref[...], sb_ref[...]
    qr = [(_rope(q_ref[:, c * LANES:(c + 1) * LANES], cos, sa, sb) * ATTN_SCALE).astype(BF16)
          for c in range(2)]
    qs = jnp.concatenate([qr[h // 2][:, (h % 2) * dh:(h % 2 + 1) * dh] for h in range(B_GROUP)], axis=0)

    blocks = [kc_ref[...], kp_ref[...], ko_ref[...], kn_ref[...]]
    k_all = jnp.concatenate([x[:, :dh] for x in blocks], axis=0)
    v_all = jnp.concatenate([x[:, dh:] for x in blocks], axis=0)
    s = lax.dot_general(qs, k_all, _NT, preferred_element_type=F32)

    nk = lc + 3 * w
    qi = lax.broadcasted_iota(jnp.int32, (B_GROUP * w, nk), 0) % w
    col = lax.broadcasted_iota(jnp.int32, (B_GROUP * w, nk), 1)
    kj = col - lc
    valid = col < lc
    valid = jnp.logical_or(valid, jnp.logical_and(has_prev, jnp.logical_and(kj >= 0, jnp.logical_and(kj < w, kj >= qi))))
    valid = jnp.logical_or(valid, jnp.logical_and(is_lat, jnp.logical_and(kj >= w, kj < 2 * w)))
    valid = jnp.logical_or(valid, jnp.logical_and(has_next, jnp.logical_and(kj >= 2 * w, kj - 2 * w <= qi)))
    s = jnp.where(valid, s, NEG_BIG)

    snk = jnp.concatenate([jnp.full((w, 1), sink_ref[g * B_GROUP + h], F32) for h in range(B_GROUP)], axis=0)
    mx = jnp.maximum(jnp.max(s, axis=-1, keepdims=True), snk)
    p = jnp.exp(s - mx)
    den = jnp.sum(p, axis=-1, keepdims=True) + jnp.exp(snk - mx)
    o = jnp.dot(p.astype(BF16), v_all, preferred_element_type=F32) / den
    o4 = jnp.concatenate([o[h * w:(h + 1) * w, :] for h in range(B_GROUP)], axis=1)
    o_ref[...] = (o4 * _silu(gb_ref[...])).astype(BF16)


def _attention(p_b, kvr, sink, q_tabs, t_ctx, gate_col_block):
    b, t, _ = p_b.shape
    w = WINDOW
    n_ctx_blk, n_lat_blk = t_ctx // w, (t - t_ctx) // w
    nb = t // w
    gw = B_GROUP * B_HEAD_DIM
    kvb = lambda f: pl.BlockSpec((None, w, LANES), f)
    tab = pl.BlockSpec((w, LANES), lambda i, g, n, s: (n, 0))
    grid_spec = pltpu.PrefetchScalarGridSpec(
        num_scalar_prefetch=1,
        grid=(b, B_KV_HEADS, nb),
        in_specs=[pl.BlockSpec((None, w, gw), lambda i, g, n, s: (i, n, g)),
                  pl.BlockSpec((None, t_ctx, LANES), lambda i, g, n, s: (i, 0, g)),
                  kvb(lambda i, g, n, s: (i, jnp.maximum(n - 1, 0), g)),
                  kvb(lambda i, g, n, s: (i, n, g)),
                  kvb(lambda i, g, n, s: (i, jnp.minimum(n + 1, nb - 1), g)),
                  pl.BlockSpec((None, w, gw), lambda i, g, n, s: (i, n, gate_col_block + g)),
                  tab, tab, tab],
        out_specs=pl.BlockSpec((None, w, gw), lambda i, g, n, s: (i, n, g)),
    )
    return pl.pallas_call(
        functools.partial(_attn_kernel, n_ctx_blk=n_ctx_blk, n_lat_blk=n_lat_blk),
        grid_spec=grid_spec,
        out_shape=jax.ShapeDtypeStruct((b, t, B_WIDTH), BF16),
        compiler_params=_params("parallel", "parallel", "parallel"),
        name="window_attn",
    )(sink, p_b, kvr, kvr, kvr, kvr, p_b, *q_tabs)


def _gla_chunk(z, qr, v, lb, st, rev, consts):
    row8, ri, ci = consts
    c = C_CHUNK
    ng = c // SUBLANES
    one = jnp.ones((), F32)

    e = jnp.exp(-jnp.abs(z))
    r = 1.0 / (1.0 + e)
    pos = z >= 0
    sig_p = jnp.where(pos, r, e * r)
    sig_n = jnp.where(pos, e * r, r)
    f = lb + (1.0 - lb) * sig_p
    kk = (1.0 - lb) * sig_n
    ft = jnp.maximum(f, F_MIN)
    q = _silu(qr)

    pin = ft
    sin_ = ft
    for s in (1, 2, 4):
        pin = pin * jnp.where(row8 >= s, pltpu.roll(pin, s, 0), one)
        sin_ = sin_ * jnp.where(row8 < SUBLANES - s, pltpu.roll(sin_, c - s, 0), one)
    pex = jnp.where(row8 >= 1, pltpu.roll(pin, 1, 0), one)
    sex = jnp.where(row8 < SUBLANES - 1, pltpu.roll(sin_, c - 1, 0), one)
    cp, sp = (sin_, pex) if rev else (pin, sex)

    def grp(x, j):
        return x[j * SUBLANES:(j + 1) * SUBLANES, :]

    order = list(range(ng - 1, -1, -1)) if rev else list(range(ng))
    tg = [jnp.broadcast_to(pin[j * SUBLANES + SUBLANES - 1:(j + 1) * SUBLANES, :], (SUBLANES, LANES))
          for j in range(ng)]
    tgo = [tg[j] for j in order]
    cpo = [grp(cp, j) for j in order]
    spo = [grp(sp, j) for j in order]
    qo = [grp(q, j) for j in order]
    ko = [grp(kk, j) for j in order]

    def chain(xs):
        out, acc = [], None
        for x in xs:
            out.append(acc)
            acc = x if acc is None else acc * x
        return out, acc

    def mul(a, b):
        return a if b is None else a * b

    def assemble(parts):
        by_t = [None] * ng
        for p_, j in zip(parts, order):
            by_t[j] = p_
        return jnp.concatenate(by_t, axis=0)

    pre, total = chain(tgo)
    suf_r, _ = chain(tgo[::-1])
    suf = suf_r[::-1]
    qd = assemble([qo[p] * mul(cpo[p], pre[p]) for p in range(ng)])
    kd = assemble([ko[p] * mul(spo[p], suf[p]) for p in range(ng)])

    attn = jnp.zeros((c, c), F32)
    zero = jnp.zeros((SUBLANES, LANES), F32)
    for half in (1, 2, 4):
        qparts, kparts = [], []
        for p in range(ng):
            o = p % (2 * half)
            if o >= half:
                fac, _ = chain(tgo[p - (o - half):p])
                extra = None
                for x in tgo[p - (o - half):p]:
                    extra = x if extra is None else extra * x
                qparts.append(qo[p] * mul(cpo[p], extra))
                kparts.append(zero)
            else:
                extra = None
                for x in tgo[p + 1:p - o + half]:
                    extra = x if extra is None else extra * x
                kparts.append(ko[p] * mul(spo[p], extra))
                qparts.append(zero)
        a_l = lax.dot_general(assemble(qparts).astype(BF16), assemble(kparts).astype(BF16), _NT,
                              preferred_element_type=F32)
        blk = 2 * half * SUBLANES
        attn = attn + jnp.where(ri // blk == ci // blk, a_l, 0.0)

    dec = None
    for dlt in range(SUBLANES):
        if dlt == 0:
            x = q * kk
        else:
            sh = (c - 1) if rev else 1
            dec = ft if dec is None else ft * pltpu.roll(dec, sh, 0)
            ks = pltpu.roll(kk, (c - dlt) if rev else dlt, 0)
            x = q * ks * dec
        red = jnp.sum(x, axis=-1, keepdims=True)
        if rev:
            hit = jnp.logical_and(ci == ri + dlt, (ri % SUBLANES) < SUBLANES - dlt)
        else:
            hit = jnp.logical_and(ci == ri - dlt, (ri % SUBLANES) >= dlt)
        attn = attn + jnp.where(hit, red, 0.0)

    vb = v.astype(BF16)
    o = jnp.dot(attn.astype(BF16), vb, preferred_element_type=F32)
    o = o + lax.dot_general(qd.astype(BF16), st.astype(BF16), _NT, preferred_element_type=F32)
    st_new = total[0:1, :] * st + lax.dot_general(vb, kd.astype(BF16), _TN, preferred_element_type=F32)
    return o, st_new


def _gla_kernel(zf_ref, zb_ref, v_ref, q_ref, og_ref, lb_ref, gw_ref, y_ref, o_scr, *, n_ctx, n_lat):
    c = C_CHUNK
    lb = lb_ref[...]
    row8 = lax.broadcasted_iota(jnp.int32, (c, LANES), 0) % SUBLANES
    ri = lax.broadcasted_iota(jnp.int32, (c, c), 0)
    ci = lax.broadcasted_iota(jnp.int32, (c, c), 1)
    consts = (row8, ri, ci)
    n_all = n_ctx + n_lat

    def run(rev):
        z_ref = zb_ref if rev else zf_ref

        def body(i, st):
            cidx = jnp.where(i < n_ctx, n_ctx - 1 - i, 2 * n_ctx + n_lat - 1 - i) if rev else i
            r0 = pl.multiple_of(cidx * c, c)
            rows = pl.ds(r0, c)
            o, st = _gla_chunk(z_ref[rows, :], q_ref[rows, :], v_ref[rows, :], lb, st, rev, consts)
            if rev:
                o_scr[rows, :] = o_scr[rows, :] + o
            else:
                o_scr[rows, :] = o
            return st

        lax.fori_loop(0, n_all, body, jnp.zeros((C_EXPAND, LANES), F32))

    run(False)
    run(True)

    gw = gw_ref[...]
    piece = 256

    def fin(i, carry):
        rows = pl.ds(pl.multiple_of(i * piece, piece), piece)
        o = o_scr[rows, :]
        y = o * lax.rsqrt(jnp.mean(o * o, axis=-1, keepdims=True) + RMS_EPS) * gw
        y_ref[rows, :] = (y * _silu(og_ref[rows, :])).astype(BF16)
        return carry

    lax.fori_loop(0, (n_all * c) // piece, fin, 0)


def _gla(p, lb, gw, t_ctx):
    b, t, _ = p.shape
    hd = C_HEADS
    col = lambda k: pl.BlockSpec((None, t, LANES), lambda i, h: (i, 0, k * hd + h))
    vec = pl.BlockSpec((None, 1, LANES), lambda i, h: (h, 0, 0))
    return pl.pallas_call(
        functools.partial(_gla_kernel, n_ctx=t_ctx // C_CHUNK, n_lat=(t - t_ctx) // C_CHUNK),
        grid=(b, hd),
        in_specs=[col(0), col(1), col(2), col(3), col(4), vec,
                  pl.BlockSpec((1, LANES), lambda i, h: (0, 0))],
        out_specs=pl.BlockSpec((None, t, LANES), lambda i, h: (i, 0, h)),
        out_shape=jax.ShapeDtypeStruct((b, t, hd * LANES), BF16),
        scratch_shapes=[pltpu.VMEM((t, LANES), F32)],
        compiler_params=_params("parallel", "parallel"),
        name="hgrn2_gla",
    )(p, p, p, p, p, lb.reshape(hd, 1, LANES), gw.reshape(1, LANES))


def _outproj_kernel(*refs, n_in, final):
    y_refs = refs[:n_in]
    w_ref, h_ref, gt_ref = refs[n_in:n_in + 3]
    if final:
        fw_ref, o_ref = refs[n_in + 3:]
    else:
        (o_ref,) = refs[n_in + 3:]
    acc = None
    k0 = 0
    for y_ref in y_refs:
        kw = y_ref.shape[-1]
        part = jnp.dot(y_ref[...], w_ref[k0:k0 + kw, :], preferred_element_type=F32)
        acc = part if acc is None else acc + part
        k0 += kw
    tm, d = acc.shape
    hn = h_ref[...] + (acc.reshape(tm // SUBLANES, SUBLANES, d) * gt_ref[...][None]).reshape(tm, d)
    if final:
        hn = hn * lax.rsqrt(jnp.mean(hn * hn, axis=-1, keepdims=True) + RMS_EPS) * fw_ref[...]
    o_ref[...] = hn


def _outproj(ys, w, h, gate, n_ctx_tiles, final_w=None):
    b, t, d = h.shape
    tm = ROW_TILE
    final = final_w is not None
    off = n_ctx_tiles if final else 0
    nt = t // tm - off

    def grp(i, j):
        return jnp.where(j + off < n_ctx_tiles, b, i)

    in_specs = [pl.BlockSpec((None, tm, y.shape[-1]), lambda i, j: (i, j + off, 0)) for y in ys]
    in_specs += [pl.BlockSpec(w.shape, lambda i, j: (0, 0)),
                 pl.BlockSpec((None, tm, d), lambda i, j: (i, j + off, 0)),
                 pl.BlockSpec((None, SUBLANES, d), lambda i, j: (grp(i, j), 0, 0))]
    args = list(ys) + [w, h, gate]
    if final:
        in_specs.append(pl.BlockSpec((1, d), lambda i, j: (0, 0)))
        args.append(final_w.reshape(1, d))
    return pl.pallas_call(
        functools.partial(_outproj_kernel, n_in=len(ys), final=final),
        grid=(b, nt),
        in_specs=in_specs,
        out_specs=pl.BlockSpec((None, tm, d), lambda i, j: (i, j, 0)),
        out_shape=jax.ShapeDtypeStruct((b, nt * tm, d), F32),
        compiler_params=_params("parallel", "parallel"),
        name="outproj",
    )(*args)


def _rope_tables(s_lat, t_ctx):
    pos = np.arange(s_lat)
    row = (pos // GRID_W).astype(np.float32)
    colp = (pos % GRID_W).astype(np.float32)
    axis_dim = B_HEAD_DIM // 2
    inv = jnp.asarray(ROPE_BASE, F32) ** (-jnp.arange(0, axis_dim, 2, dtype=F32) / axis_dim)
    ang = jnp.concatenate([jnp.asarray(row)[:, None] * inv, jnp.asarray(colp)[:, None] * inv], axis=-1)
    cos, sin = jnp.cos(ang), jnp.sin(ang)
    zer = jnp.zeros_like(sin)
    cos64 = jnp.concatenate([cos, cos], -1)
    sa64 = jnp.concatenate([-sin, zer], -1)
    sb64 = jnp.concatenate([zer, sin], -1)
    one64 = jnp.ones_like(cos64)
    zer64 = jnp.zeros_like(cos64)

    def full(lat128, ident):
        return jnp.concatenate([jnp.broadcast_to(ident, (t_ctx, LANES)), lat128], axis=0)

    ident_c = jnp.ones((1, LANES), F32)
    ident_s = jnp.zeros((1, LANES), F32)
    q_tabs = (full(jnp.concatenate([cos64, cos64], -1), ident_c),
              full(jnp.concatenate([sa64, sa64], -1), ident_s),
              full(jnp.concatenate([sb64, sb64], -1), ident_s))
    kv_tabs = (full(jnp.concatenate([cos64, one64], -1), ident_c),
               full(jnp.concatenate([sa64, zer64], -1), ident_s),
               full(jnp.concatenate([sb64, zer64], -1), ident_s))
    return q_tabs, kv_tabs


def _even_weights(w_in):
    dh = B_HEAD_DIM
    perm = np.concatenate([np.arange(0, dh, 2), np.arange(1, dh, 2)])
    n_state = A_WIDTH + 2 * B_KV_WIDTH
    q0 = n_state + A_WIDTH
    q_cols = np.concatenate([q0 + h * dh + perm for h in range(B_HEADS)])
    kv_cols = np.concatenate([np.concatenate([A_WIDTH + j * dh + perm,
                                              A_WIDTH + B_KV_WIDTH + j * dh + np.arange(dh)])
                              for j in range(B_KV_HEADS)])
    gb_cols = np.arange(q0 + B_WIDTH, q0 + 2 * B_WIDTH)
    a_cols = np.concatenate([np.arange(0, A_WIDTH), np.arange(n_state, n_state + A_WIDTH)])
    w_b = w_in[:, np.concatenate([q_cols, kv_cols, gb_cols])].astype(BF16)
    w_a = w_in[:, a_cols].astype(BF16)
    return w_a, w_b


def _groups(vec_b, vec_c):
    allv = jnp.concatenate([vec_b, vec_c[None, :]], axis=0)
    return jnp.broadcast_to(allv[:, None, :], (allv.shape[0], SUBLANES, allv.shape[1]))


def kernel(x, c, ctx, c_ctx, ada_w, ada_b, norm_w, ev_w_in, ev_conv_w, ev_conv_b, ev_rg_wx, ev_rg_bx,
           ev_rg_wa, ev_rg_ba, ev_rg_lambda, ev_sink, ev_w_out, od_w_in, od_lb_raw, od_gnorm_w, od_w_out,
           final_norm_w):
    b, s_lat, d = x.shape
    t_ctx = ctx.shape[1]
    t_all = t_ctx + s_lat
    assert b == SUBLANES and d == D_MODEL
    assert t_ctx % ROW_TILE == 0 and s_lat % ROW_TILE == 0
    n_ctx_tiles = t_ctx // ROW_TILE

    sc_rows = jnp.zeros((2 * SUBLANES, d), F32)
    sc_rows = sc_rows.at[:b].set(jax.nn.silu(c)).at[b].set(jax.nn.silu(c_ctx))
    mod = _ada_mod(sc_rows, ada_w, ada_b)

    lb_p = jax.nn.softmax(od_lb_raw.astype(F32), axis=0)
    lower_bounds = jnp.cumsum(lb_p, axis=0) - lb_p[0]
    q_tabs, kv_tabs = _rope_tables(s_lat, t_ctx)

    h = jnp.concatenate([ctx, x], axis=1)
    out = None
    for layer in range(DEPTH):
        last = layer == DEPTH - 1
        j = layer // 2
        ml = mod[layer]
        shift = _groups(ml[:b, :d], ml[b, :d])
        scale = _groups(ml[:b, d:2 * d], ml[b, d:2 * d])
        gate = _groups(ml[:b, 2 * d:], ml[b, 2 * d:])
        if layer % 2 == 0:
            w_a, w_b = _even_weights(ev_w_in[j])
            p_b, u = _inproj(h, shift, scale, norm_w[layer], w_b, n_ctx_tiles, emit_u=True)
            u_tm = jnp.transpose(u, (1, 0, 2)).reshape(t_all * b, d)
            p_a = _matmul(u_tm, w_a)
            la = -A_C * jax.nn.softplus(-ev_rg_lambda[j].astype(F32))
            w_cat = jnp.concatenate([ev_rg_wx[j], ev_rg_wa[j]], axis=-1).astype(BF16)
            rg = lambda dr, yf: _rglru_pass(
                p_a, yf, ev_conv_w[j], ev_conv_b[j].reshape(1, -1), w_cat[dr],
                ev_rg_bx[j, dr].reshape(1, -1), ev_rg_ba[j, dr].reshape(1, -1), la[dr].reshape(1, -1),
                t_ctx, s_lat, rev=bool(dr))
            ya_tm = rg(1, rg(0, None))
            ya = jnp.transpose(ya_tm.reshape(t_all, b, A_WIDTH), (1, 0, 2))
            kvr = _rope_kv(p_b, kv_tabs, col_block=B_WIDTH // (2 * B_KV_WIDTH))
            yb = _attention(p_b, kvr, ev_sink[j].astype(F32), q_tabs, t_ctx,
                            gate_col_block=(B_WIDTH + 2 * B_KV_WIDTH) // (B_GROUP * B_HEAD_DIM))
            ys, w_out = [ya, yb], ev_w_out[j].astype(BF16)
        else:
            p = _inproj(h, shift, scale, norm_w[layer], od_w_in[j].astype(BF16), n_ctx_tiles, emit_u=False)
            ys, w_out = [_gla(p, lower_bounds[j], od_gnorm_w[j], t_ctx)], od_w_out[j].astype(BF16)
        if last:
            out = _outproj(ys, w_out, h, gate, n_ctx_tiles, final_w=final_norm_w)
        else:
            h = _outproj(ys, w_out, h, gate, n_ctx_tiles)
    return out
```

```python
import functools

import jax
import jax.numpy as jnp
import numpy as np
from jax import lax
from jax.experimental import pallas as pl
from jax.experimental.pallas import tpu as pltpu

F32 = jnp.float32
BF16 = jnp.bfloat16

D_MODEL = 1024
DEPTH = 4
GRID_W = 64
RMS_EPS = 1e-6
NEG_BIG = -1e30
F_MIN = 1e-30

A_WIDTH = D_MODEL
A_HEADS = 8
A_BLOCK = A_WIDTH // A_HEADS
A_CONV = 4
A_C = 8.0
B_HEADS = 16
B_KV_HEADS = 4
B_HEAD_DIM = 64
B_GROUP = B_HEADS // B_KV_HEADS
B_WIDTH = B_HEADS * B_HEAD_DIM
B_KV_WIDTH = B_KV_HEADS * B_HEAD_DIM
WINDOW = 128
ROPE_BASE = 10000.0
ATTN_SCALE = B_HEAD_DIM ** -0.5
LOG2E = 1.4426950408889634
C_EXPAND = 128
C_HEADS = D_MODEL // C_EXPAND
C_CHUNK = 64

SUBLANES = 8
LANES = 128
ROW_TILE = 256
GLA_HEADS_PER_STEP = 2
VMEM_LIMIT = 56 * 1024 * 1024

_NT = (((1,), (1,)), ((), ()))
_TN = (((0,), (0,)), ((), ()))


def _params(*sem):
    return pltpu.CompilerParams(dimension_semantics=sem, vmem_limit_bytes=VMEM_LIMIT)


def _sigmoid(x):
    return 1.0 / (1.0 + jnp.exp(-x))


def _silu(x):
    return x * _sigmoid(x)


def _ada_kernel(sc_ref, w_ref, b_ref, o_ref):
    o_ref[...] = jnp.dot(sc_ref[...], w_ref[...], preferred_element_type=F32,
                         precision=lax.Precision.HIGHEST) + b_ref[...]


def _ada_mod(sc, ada_w, ada_b):
    depth, d, n = ada_w.shape
    rows = sc.shape[0]
    tn = 1024
    return pl.pallas_call(
        _ada_kernel,
        grid=(depth, n // tn),
        in_specs=[pl.BlockSpec((rows, d), lambda l, j: (0, 0)),
                  pl.BlockSpec((None, d, tn), lambda l, j: (l, 0, j)),
                  pl.BlockSpec((None, 1, tn), lambda l, j: (l, 0, j))],
        out_specs=pl.BlockSpec((None, rows, tn), lambda l, j: (l, 0, j)),
        out_shape=jax.ShapeDtypeStruct((depth, rows, n), F32),
        compiler_params=_params("parallel", "parallel"),
        name="ada_mod",
    )(sc, ada_w, ada_b.reshape(depth, 1, n))


def _inproj_kernel(h_ref, sh_ref, sc_ref, nw_ref, w_ref, *rest, emit_u, col_chunk):
    if emit_u:
        p_ref, u_ref, u_scr = rest
    else:
        p_ref, u_scr = rest
    x = h_ref[...]
    tm, d = x.shape
    y = x * lax.rsqrt(jnp.mean(x * x, axis=-1, keepdims=True) + RMS_EPS) * nw_ref[...]
    y3 = y.reshape(tm // SUBLANES, SUBLANES, d)
    u = (y3 * (1.0 + sc_ref[...])[None] + sh_ref[...][None]).reshape(tm, d)
    ub = u.astype(BF16)
    u_scr[...] = ub
    if emit_u:
        u_ref[...] = ub
    n = p_ref.shape[-1]
    for c in range(0, n, col_chunk):
        p_ref[:, c:c + col_chunk] = jnp.dot(u_scr[...], w_ref[:, c:c + col_chunk],
                                            preferred_element_type=F32)


def _inproj(h, shift, scale, norm_w, w, n_ctx_tiles, emit_u):
    b, t, d = h.shape
    n = w.shape[1]
    tm = ROW_TILE

    def grp(i, j):
        return jnp.where(j < n_ctx_tiles, b, i)

    out_shape = [jax.ShapeDtypeStruct((b, t, n), F32)]
    out_specs = [pl.BlockSpec((None, tm, n), lambda i, j: (i, j, 0))]
    if emit_u:
        out_shape.append(jax.ShapeDtypeStruct((b, t, d), BF16))
        out_specs.append(pl.BlockSpec((None, tm, d), lambda i, j: (i, j, 0)))
    res = pl.pallas_call(
        functools.partial(_inproj_kernel, emit_u=emit_u, col_chunk=512),
        grid=(b, t // tm),
        in_specs=[pl.BlockSpec((None, tm, d), lambda i, j: (i, j, 0)),
                  pl.BlockSpec((None, SUBLANES, d), lambda i, j: (grp(i, j), 0, 0)),
                  pl.BlockSpec((None, SUBLANES, d), lambda i, j: (grp(i, j), 0, 0)),
                  pl.BlockSpec((1, d), lambda i, j: (0, 0)),
                  pl.BlockSpec((d, n), lambda i, j: (0, 0))],
        out_specs=out_specs,
        out_shape=out_shape,
        scratch_shapes=[pltpu.VMEM((tm, d), BF16)],
        compiler_params=_params("parallel", "parallel"),
        name="inproj",
    )(h, shift, scale, norm_w.reshape(1, d), w)
    return res if emit_u else res[0]


def _matmul_kernel(u_ref, w_ref, p_ref, *, col_chunk):
    n = p_ref.shape[-1]
    for c in range(0, n, col_chunk):
        p_ref[:, c:c + col_chunk] = jnp.dot(u_ref[...], w_ref[:, c:c + col_chunk],
                                            preferred_element_type=F32)


def _matmul(u, w):
    r, d = u.shape
    n = w.shape[1]
    tm = 512
    return pl.pallas_call(
        functools.partial(_matmul_kernel, col_chunk=512),
        grid=(r // tm,),
        in_specs=[pl.BlockSpec((tm, d), lambda i: (i, 0)),
                  pl.BlockSpec((d, n), lambda i: (0, 0))],
        out_specs=pl.BlockSpec((tm, n), lambda i: (i, 0)),
        out_shape=jax.ShapeDtypeStruct((r, n), F32),
        compiler_params=_params("parallel"),
        name="matmul_tm",
    )(u, w)


def _scan_tile(i, n_ctx, n_lat, rev):
    if not rev:
        return i
    return jnp.where(i < n_ctx, n_ctx - 1 - i, 2 * n_ctx + n_lat - 1 - i)


def _rglru_kernel(*refs, rev, tt, n_ctx, n_lat):
    if rev:
        (xa_ref, xp_ref, xn1_ref, xn2_ref, ga_ref, yf_ref, cw_ref, cb_ref, w_ref, bx_ref, ba_ref,
         la_ref, out_ref, h_scr, xpad_scr, u_scr, a_scr, b_scr) = refs
    else:
        (xa_ref, xp_ref, xn1_ref, xn2_ref, cw_ref, cb_ref, w_ref, bx_ref, ba_ref,
         la_ref, out_ref, h_scr, xpad_scr, u_scr, a_scr, b_scr) = refs
    i = pl.program_id(0)
    tile = _scan_tile(i, n_ctx, n_lat, rev)
    seg_first = jnp.logical_or(tile == 0, tile == n_ctx)
    seg_last = jnp.logical_or(tile == n_ctx - 1, tile == n_ctx + n_lat - 1)
    rows = tt * SUBLANES

    @pl.when(i == 0)
    def _():
        h_scr[...] = jnp.zeros_like(h_scr)

    xpad_scr[0:8, :] = jnp.where(seg_first, 0.0, xp_ref[...])
    xpad_scr[8:8 + rows, :] = xa_ref[...]
    xpad_scr[8 + rows:16 + rows, :] = jnp.where(seg_last, 0.0, xn1_ref[...])
    xpad_scr[16 + rows:24 + rows, :] = jnp.where(seg_last, 0.0, xn2_ref[...])
    u = cb_ref[...] + xpad_scr[0:rows, :] * cw_ref[0:1, :]
    for k in range(1, A_CONV):
        u = u + xpad_scr[8 * k:8 * k + rows, :] * cw_ref[k:k + 1, :]
    u_scr[...] = u

    for hd in range(A_HEADS):
        sl = slice(hd * A_BLOCK, (hd + 1) * A_BLOCK)
        uh = u_scr[:, sl]
        g = jnp.dot(uh.astype(BF16), w_ref[hd], preferred_element_type=F32)
        gate_x = _sigmoid(g[:, :A_BLOCK] + bx_ref[:, sl])
        gate_a = _sigmoid(g[:, A_BLOCK:] + ba_ref[:, sl])
        log_a = gate_a * la_ref[:, sl]
        a = jnp.exp(log_a)
        a_scr[:, sl] = a
        b_scr[:, sl] = jnp.sqrt(-jnp.tanh(log_a) * (1.0 + a * a)) * gate_x * uh

    def step(s, h):
        t = (tt - 1 - s) if rev else s
        r0 = pl.multiple_of(t * SUBLANES, SUBLANES)
        h = a_scr[pl.ds(r0, SUBLANES), :] * h + b_scr[pl.ds(r0, SUBLANES), :]
        b_scr[pl.ds(r0, SUBLANES), :] = h
        return h

    h_scr[...] = lax.fori_loop(0, tt, step, h_scr[...], unroll=8)
    if rev:
        out_ref[...] = ((yf_ref[...] + b_scr[...]) * _silu(ga_ref[...])).astype(BF16)
    else:
        out_ref[...] = b_scr[...]


def _rglru_pass(p_a, yf, conv_w, conv_b, w_cat, bx, ba, la, t_ctx, t_lat, rev):
    w = A_WIDTH
    tt = 64
    rows = tt * SUBLANES
    n_ctx, n_lat = t_ctx // tt, t_lat // tt
    t_all = t_ctx + t_lat
    tile = functools.partial(_scan_tile, n_ctx=n_ctx, n_lat=n_lat, rev=rev)
    const = lambda i: (0, 0)
    in_specs = [pl.BlockSpec((rows, w), lambda i: (tile(i), 0)),
                pl.BlockSpec((SUBLANES, w), lambda i: (jnp.maximum(tile(i) * tt - 1, 0), 0)),
                pl.BlockSpec((SUBLANES, w), lambda i: (jnp.minimum(tile(i) * tt + tt, t_all - 1), 0)),
                pl.BlockSpec((SUBLANES, w), lambda i: (jnp.minimum(tile(i) * tt + tt + 1, t_all - 1), 0))]
    args = [p_a, p_a, p_a, p_a]
    if rev:
        in_specs += [pl.BlockSpec((rows, w), lambda i: (tile(i), 1)),
                     pl.BlockSpec((rows, w), lambda i: (tile(i), 0))]
        args += [p_a, yf]
    in_specs += [pl.BlockSpec((A_CONV, w), const), pl.BlockSpec((1, w), const),
                 pl.BlockSpec((A_HEADS, A_BLOCK, 2 * A_BLOCK), lambda i: (0, 0, 0)),
                 pl.BlockSpec((1, w), const), pl.BlockSpec((1, w), const), pl.BlockSpec((1, w), const)]
    args += [conv_w, conv_b, w_cat, bx, ba, la]
    return pl.pallas_call(
        functools.partial(_rglru_kernel, rev=rev, tt=tt, n_ctx=n_ctx, n_lat=n_lat),
        grid=(n_ctx + n_lat,),
        in_specs=in_specs,
        out_specs=pl.BlockSpec((rows, w), lambda i: (tile(i), 0)),
        out_shape=jax.ShapeDtypeStruct((t_all * SUBLANES, w), BF16 if rev else F32),
        scratch_shapes=[pltpu.VMEM((SUBLANES, w), F32),
                        pltpu.VMEM((rows + 3 * SUBLANES, w), F32),
                        pltpu.VMEM((rows, w), F32),
                        pltpu.VMEM((rows, w), F32),
                        pltpu.VMEM((rows, w), F32)],
        compiler_params=_params("arbitrary"),
        name="rglru_bwd" if rev else "rglru_fwd",
    )(*args)


def _rope(x, cos, sin_a, sin_b):
    return x * cos + pltpu.roll(x, 96, 1) * sin_a + pltpu.roll(x, 32, 1) * sin_b


def _rope_kv_kernel(kv_ref, cos_ref, sa_ref, sb_ref, o_ref):
    cos, sa, sb = cos_ref[...], sa_ref[...], sb_ref[...]
    for g in range(B_KV_HEADS):
        sl = slice(g * LANES, (g + 1) * LANES)
        o_ref[:, sl] = _rope(kv_ref[:, sl], cos, sa, sb).astype(BF16)


def _rope_kv(p_b, tabs, col_block):
    b, t, _ = p_b.shape
    tm = ROW_TILE
    wkv = 2 * B_KV_WIDTH
    tab = pl.BlockSpec((tm, LANES), lambda i, j: (j, 0))
    return pl.pallas_call(
        _rope_kv_kernel,
        grid=(b, t // tm),
        in_specs=[pl.BlockSpec((None, tm, wkv), lambda i, j: (i, j, col_block)), tab, tab, tab],
        out_specs=pl.BlockSpec((None, tm, wkv), lambda i, j: (i, j, 0)),
        out_shape=jax.ShapeDtypeStruct((b, t, wkv), BF16),
        compiler_params=_params("parallel", "parallel"),
        name="rope_kv",
    )(p_b, *tabs)


def _attn_kernel(sink_ref, q_ref, kc_ref, kp_ref, ko_ref, kn_ref, gb_ref, cos_ref, sa_ref, sb_ref,
                 o_ref, s_scr, p_scr, *, n_ctx_blk, n_lat_blk):
    n = pl.program_id(1)
    m = n - n_ctx_blk
    is_lat = n >= n_ctx_blk
    has_prev = jnp.logical_and(is_lat, m >= 1)
    has_next = jnp.logical_and(is_lat, m <= n_lat_blk - 2)
    w = WINDOW
    dh = B_HEAD_DIM
    lc = kc_ref.shape[0]

    cos, sa, sb = cos_ref[...], sa_ref[...], sb_ref[...]
    low = lax.broadcasted_iota(jnp.int32, (w, LANES), 1) < dh
    nq = B_GROUP * w
    kj = lax.broadcasted_iota(jnp.int32, (w, nq), 0)
    qi = lax.broadcasted_iota(jnp.int32, (w, nq), 1) % w
    ok_prev = jnp.logical_and(has_prev, kj >= qi)
    ok_next = jnp.logical_and(has_next, kj <= qi)
    segs = ((0, lc), (lc, w), (lc + w, w), (lc + 2 * w, w))

    def fold8(x, op):
        return op(x.reshape(x.shape[0] // SUBLANES, SUBLANES, x.shape[1]), axis=0)

    for g in range(B_KV_HEADS):
        parts = []
        for c in range(2):
            col0 = g * B_GROUP * dh + c * LANES
            qc = _rope(q_ref[:, col0:col0 + LANES], cos, sa, sb) * (ATTN_SCALE * LOG2E)
            parts.append(jnp.where(low, qc, 0.0).astype(BF16))
            parts.append(jnp.where(low, pltpu.roll(qc, dh, 1), 0.0).astype(BF16))
        qs = jnp.concatenate(parts, axis=0)
        ks = slice(g * LANES, (g + 1) * LANES)
        kv_all = jnp.concatenate([kc_ref[:, ks], kp_ref[:, ks], ko_ref[:, ks], kn_ref[:, ks]], axis=0)
        st = lax.dot_general(kv_all, qs, _NT, preferred_element_type=F32)
        s_scr[g, 0:lc, :] = st[0:lc]
        s_scr[g, lc:lc + w, :] = jnp.where(ok_prev, st[lc:lc + w], NEG_BIG)
        s_scr[g, lc + w:lc + 2 * w, :] = jnp.where(is_lat, st[lc + w:lc + 2 * w], NEG_BIG)
        s_scr[g, lc + 2 * w:, :] = jnp.where(ok_next, st[lc + 2 * w:], NEG_BIG)

        rdens = []
        for h in range(B_GROUP):
            cb = slice(h * w, (h + 1) * w)
            snk = sink_ref[g * B_GROUP + h] * LOG2E
            mx8 = jnp.full((SUBLANES, w), snk, F32)
            for r0, nr in segs:
                mx8 = jnp.maximum(mx8, fold8(s_scr[g, r0:r0 + nr, cb], jnp.max))
            mx = jnp.max(mx8, axis=0, keepdims=True)
            den8 = jnp.zeros((SUBLANES, w), F32)
            for r0, nr in segs:
                p = jnp.exp2(s_scr[g, r0:r0 + nr, cb] - mx)
                den8 = den8 + fold8(p, jnp.sum)
                p_scr[g, r0:r0 + nr, cb] = p.astype(BF16)
            rdens.append(1.0 / (jnp.sum(den8, axis=0, keepdims=True) + jnp.exp2(snk - mx)))

        ot = lax.dot_general(kv_all, p_scr[g], _TN, preferred_element_type=F32)
        ov = ot[dh:, :] * jnp.concatenate(rdens, axis=1)
        for c in range(2):
            pair = jnp.concatenate([ov[:, (2 * c) * w:(2 * c + 1) * w], ov[:, (2 * c + 1) * w:(2 * c + 2) * w]],
                                   axis=0)
            col0 = g * B_GROUP * dh + c * LANES
            y = pair.T * _silu(gb_ref[:, col0:col0 + LANES])
            o_ref[:, col0:col0 + LANES] = y.astype(BF16)


def _attention(p_b, kvr, sink, q_tabs, t_ctx, gate_col_block):
    b, t, _ = p_b.shape
    w = WINDOW
    n_ctx_blk, n_lat_blk = t_ctx // w, (t - t_ctx) // w
    nb = t // w
    wkv = 2 * B_KV_WIDTH
    kvb = lambda f: pl.BlockSpec((None, w, wkv), f)
    tab = pl.BlockSpec((w, LANES), lambda i, n, s: (n, 0))
    grid_spec = pltpu.PrefetchScalarGridSpec(
        num_scalar_prefetch=1,
        grid=(b, nb),
        in_specs=[pl.BlockSpec((None, w, B_WIDTH), lambda i, n, s: (i, n, 0)),
                  pl.BlockSpec((None, t_ctx, wkv), lambda i, n, s: (i, 0, 0)),
                  kvb(lambda i, n, s: (i, jnp.maximum(n - 1, 0), 0)),
                  kvb(lambda i, n, s: (i, n, 0)),
                  kvb(lambda i, n, s: (i, jnp.minimum(n + 1, nb - 1), 0)),
                  pl.BlockSpec((None, w, B_WIDTH), lambda i, n, s: (i, n, gate_col_block)),
                  tab, tab, tab],
        out_specs=pl.BlockSpec((None, w, B_WIDTH), lambda i, n, s: (i, n, 0)),
        scratch_shapes=[pltpu.VMEM((B_KV_HEADS, t_ctx + 3 * w, B_GROUP * w), F32),
                        pltpu.VMEM((B_KV_HEADS, t_ctx + 3 * w, B_GROUP * w), BF16)],
    )
    return pl.pallas_call(
        functools.partial(_attn_kernel, n_ctx_blk=n_ctx_blk, n_lat_blk=n_lat_blk),
        grid_spec=grid_spec,
        out_shape=jax.ShapeDtypeStruct((b, t, B_WIDTH), BF16),
        compiler_params=_params("parallel", "parallel"),
        name="window_attn",
    )(sink, p_b, kvr, kvr, kvr, kvr, p_b, *q_tabs)


def _gla_region_ids(rev):
    c = C_CHUNK
    ri = lax.broadcasted_iota(jnp.int32, (c, c), 0)
    ci = lax.broadcasted_iota(jnp.int32, (c, c), 1)
    reg = jnp.where(ri // 8 == ci // 8, 0,
                    jnp.where(ri // 16 == ci // 16, 1, jnp.where(ri // 32 == ci // 32, 2, 3)))
    seen = (ci >= ri) if rev else (ci <= ri)
    return jnp.where(seen, reg, -1)


def _prod(xs):
    acc = None
    for x in xs:
        acc = x if acc is None else acc * x
    return acc


def _mul(a, b):
    return a if b is None else a * b


def _gla_chunk(z, qr, v, lb, st, rev, reg, row8, lane):
    c = C_CHUNK
    ng = c // SUBLANES
    one = jnp.ones((), F32)

    e = jnp.exp(-jnp.abs(z))
    r = 1.0 / (1.0 + e)
    pos = z >= 0
    f = lb + (1.0 - lb) * jnp.where(pos, r, e * r)
    kk = (1.0 - lb) * jnp.where(pos, e * r, r)
    ft = jnp.maximum(f, F_MIN)
    q = _silu(qr)

    def grp(x, j):
        return x[j * SUBLANES:(j + 1) * SUBLANES, :]

    def toward(x, s):
        return pltpu.roll(x, (SUBLANES - s) if rev else s, 0)

    def ahead(x, s):
        return pltpu.roll(x, s if rev else (SUBLANES - s), 0)

    pos8 = (SUBLANES - 1 - row8) if rev else row8
    order = list(range(ng - 1, -1, -1)) if rev else list(range(ng))
    cpo, spo, tgo, qo, ko = [], [], [], [], []
    dparts = [None] * ng
    for j in order:
        fg, qg, kg = grp(ft, j), grp(q, j), grp(kk, j)
        inc = fg
        exc = jnp.where(pos8 < SUBLANES - 1, ahead(fg, 1), one)
        for s in (1, 2, 4):
            inc = inc * jnp.where(pos8 >= s, toward(inc, s), one)
            exc = exc * jnp.where(pos8 < SUBLANES - s, ahead(exc, s), one)
        last = 0 if rev else SUBLANES - 1
        cpo.append(inc)
        spo.append(exc)
        tgo.append(jnp.broadcast_to(inc[last:last + 1, :], (SUBLANES, LANES)))
        qo.append(qg)
        ko.append(kg)
        w_ = kg
        acc = jnp.where(lane == 0, jnp.sum(qg * kg, axis=-1, keepdims=True), 0.0)
        for dlt in range(1, SUBLANES):
            w_ = fg * toward(w_, 1)
            red = jnp.sum(qg * w_, axis=-1, keepdims=True)
            acc = jnp.where(lane == (dlt if rev else LANES - dlt), red, acc)
        dparts[j] = acc
    diag = pltpu.roll(jnp.concatenate(dparts, axis=0), 0, 1, stride=1, stride_axis=0)[:, :c]

    def assemble(parts):
        by_t = [None] * ng
        for p_, j in zip(parts, order):
            by_t[j] = p_
        return jnp.concatenate(by_t, axis=0)

    total = _prod(tgo)
    qd = assemble([qo[p] * _mul(cpo[p], _prod(tgo[:p])) for p in range(ng)])
    kd = assemble([ko[p] * _mul(spo[p], _prod(tgo[p + 1:])) for p in range(ng)])

    zero = jnp.zeros((SUBLANES, LANES), F32)
    levels = []
    for half in (1, 2, 4):
        qparts, kparts = [], []
        for p in range(ng):
            o = p % (2 * half)
            if o >= half:
                qparts.append(qo[p] * _mul(cpo[p], _prod(tgo[p - (o - half):p])))
                kparts.append(zero)
            else:
                kparts.append(ko[p] * _mul(spo[p], _prod(tgo[p + 1:p - o + half])))
                qparts.append(zero)
        levels.append(lax.dot_general(assemble(qparts).astype(BF16), assemble(kparts).astype(BF16), _NT,
                                      preferred_element_type=F32))

    attn = jnp.where(reg == 0, diag,
                     jnp.where(reg == 1, levels[0],
                               jnp.where(reg == 2, levels[1], jnp.where(reg == 3, levels[2], 0.0))))

    vb = v.astype(BF16)
    o = jnp.dot(attn.astype(BF16), vb, preferred_element_type=F32)
    o = o + lax.dot_general(qd.astype(BF16), st.astype(BF16), _NT, preferred_element_type=F32)
    st_new = total[0:1, :] * st + lax.dot_general(vb, kd.astype(BF16), _TN, preferred_element_type=F32)
    return o, st_new


def _gla_kernel(zf_ref, zb_ref, v_ref, q_ref, og_ref, lb_ref, gw_ref, y_ref, of_scr, ob_scr, st_scr,
                *, n_ctx, n_lat, heads):
    c = C_CHUNK
    row8 = lax.broadcasted_iota(jnp.int32, (SUBLANES, LANES), 0)
    lane = lax.broadcasted_iota(jnp.int32, (SUBLANES, LANES), 1)
    regs = (_gla_region_ids(False), _gla_region_ids(True))
    n_all = n_ctx + n_lat
    st_scr[...] = jnp.zeros_like(st_scr)

    def body(i, carry):
        for hh in range(heads):
            ls = slice(hh * LANES, (hh + 1) * LANES)
            lb = lb_ref[:, ls]
            for rev in (False, True):
                cidx = jnp.where(i < n_ctx, n_ctx - 1 - i, 2 * n_ctx + n_lat - 1 - i) if rev else i
                rows = pl.ds(pl.multiple_of(cidx * c, c), c)
                z_ref = zb_ref if rev else zf_ref
                o, st = _gla_chunk(z_ref[rows, ls], q_ref[rows, ls], v_ref[rows, ls], lb,
                                   st_scr[2 * hh + int(rev)], rev, regs[int(rev)], row8, lane)
                (ob_scr if rev else of_scr)[rows, ls] = o
                st_scr[2 * hh + int(rev)] = st
        return carry

    lax.fori_loop(0, n_all, body, 0)

    gw = gw_ref[...]
    piece = 256

    def fin(i, carry):
        rows = pl.ds(pl.multiple_of(i * piece, piece), piece)
        for hh in range(heads):
            ls = slice(hh * LANES, (hh + 1) * LANES)
            o = of_scr[rows, ls] + ob_scr[rows, ls]
            y = o * lax.rsqrt(jnp.mean(o * o, axis=-1, keepdims=True) + RMS_EPS) * gw
            y_ref[rows, ls] = (y * _silu(og_ref[rows, ls])).astype(BF16)
        return carry

    lax.fori_loop(0, (n_all * c) // piece, fin, 0)


def _gla(p, lb, gw, t_ctx):
    b, t, _ = p.shape
    hps = GLA_HEADS_PER_STEP
    wb = hps * LANES
    nblk = C_HEADS // hps
    col = lambda k: pl.BlockSpec((None, t, wb), lambda i, h: (i, 0, k * nblk + h))
    return pl.pallas_call(
        functools.partial(_gla_kernel, n_ctx=t_ctx // C_CHUNK, n_lat=(t - t_ctx) // C_CHUNK, heads=hps),
        grid=(b, nblk),
        in_specs=[col(0), col(1), col(2), col(3), col(4),
                  pl.BlockSpec((None, 1, wb), lambda i, h: (h, 0, 0)),
                  pl.BlockSpec((1, LANES), lambda i, h: (0, 0))],
        out_specs=pl.BlockSpec((None, t, wb), lambda i, h: (i, 0, h)),
        out_shape=jax.ShapeDtypeStruct((b, t, C_HEADS * LANES), BF16),
        scratch_shapes=[pltpu.VMEM((t, wb), F32), pltpu.VMEM((t, wb), F32),
                        pltpu.VMEM((2 * hps, C_EXPAND, LANES), F32)],
        compiler_params=_params("parallel", "parallel"),
        name="hgrn2_gla",
    )(p, p, p, p, p, lb.reshape(nblk, 1, wb), gw.reshape(1, LANES))


def _outproj_kernel(*refs, n_in, final):
    y_refs = refs[:n_in]
    w_ref, h_ref, gt_ref = refs[n_in:n_in + 3]
    if final:
        fw_ref, o_ref = refs[n_in + 3:]
    else:
        (o_ref,) = refs[n_in + 3:]
    acc = None
    k0 = 0
    for y_ref in y_refs:
        kw = y_ref.shape[-1]
        part = jnp.dot(y_ref[...], w_ref[k0:k0 + kw, :], preferred_element_type=F32)
        acc = part if acc is None else acc + part
        k0 += kw
    tm, d = acc.shape
    hn = h_ref[...] + (acc.reshape(tm // SUBLANES, SUBLANES, d) * gt_ref[...][None]).reshape(tm, d)
    if final:
        hn = hn * lax.rsqrt(jnp.mean(hn * hn, axis=-1, keepdims=True) + RMS_EPS) * fw_ref[...]
    o_ref[...] = hn


def _outproj(ys, w, h, gate, n_ctx_tiles, final_w=None):
    b, t, d = h.shape
    tm = ROW_TILE
    final = final_w is not None
    off = n_ctx_tiles if final else 0
    nt = t // tm - off

    def grp(i, j):
        return jnp.where(j + off < n_ctx_tiles, b, i)

    in_specs = [pl.BlockSpec((None, tm, y.shape[-1]), lambda i, j: (i, j + off, 0)) for y in ys]
    in_specs += [pl.BlockSpec(w.shape, lambda i, j: (0, 0)),
                 pl.BlockSpec((None, tm, d), lambda i, j: (i, j + off, 0)),
                 pl.BlockSpec((None, SUBLANES, d), lambda i, j: (grp(i, j), 0, 0))]
    args = list(ys) + [w, h, gate]
    if final:
        in_specs.append(pl.BlockSpec((1, d), lambda i, j: (0, 0)))
        args.append(final_w.reshape(1, d))
    return pl.pallas_call(
        functools.partial(_outproj_kernel, n_in=len(ys), final=final),
        grid=(b, nt),
        in_specs=in_specs,
        out_specs=pl.BlockSpec((None, tm, d), lambda i, j: (i, j, 0)),
        out_shape=jax.ShapeDtypeStruct((b, nt * tm, d), F32),
        compiler_params=_params("parallel", "parallel"),
        name="outproj",
    )(*args)


def _rope_tables(s_lat, t_ctx):
    pos = np.arange(s_lat)
    row = (pos // GRID_W).astype(np.float32)
    colp = (pos % GRID_W).astype(np.float32)
    axis_dim = B_HEAD_DIM // 2
    inv = jnp.asarray(ROPE_BASE, F32) ** (-jnp.arange(0, axis_dim, 2, dtype=F32) / axis_dim)
    ang = jnp.concatenate([jnp.asarray(row)[:, None] * inv, jnp.asarray(colp)[:, None] * inv], axis=-1)
    cos, sin = jnp.cos(ang), jnp.sin(ang)
    zer = jnp.zeros_like(sin)
    cos64 = jnp.concatenate([cos, cos], -1)
    sa64 = jnp.concatenate([-sin, zer], -1)
    sb64 = jnp.concatenate([zer, sin], -1)
    one64 = jnp.ones_like(cos64)
    zer64 = jnp.zeros_like(cos64)

    def full(lat128, ident):
        return jnp.concatenate([jnp.broadcast_to(ident, (t_ctx, LANES)), lat128], axis=0)

    ident_c = jnp.ones((1, LANES), F32)
    ident_s = jnp.zeros((1, LANES), F32)
    q_tabs = (full(jnp.concatenate([cos64, cos64], -1), ident_c),
              full(jnp.concatenate([sa64, sa64], -1), ident_s),
              full(jnp.concatenate([sb64, sb64], -1), ident_s))
    kv_tabs = (full(jnp.concatenate([cos64, one64], -1), ident_c),
               full(jnp.concatenate([sa64, zer64], -1), ident_s),
               full(jnp.concatenate([sb64, zer64], -1), ident_s))
    return q_tabs, kv_tabs


def _even_weights(w_in):
    dh = B_HEAD_DIM
    perm = np.concatenate([np.arange(0, dh, 2), np.arange(1, dh, 2)])
    n_state = A_WIDTH + 2 * B_KV_WIDTH
    q0 = n_state + A_WIDTH
    q_cols = np.concatenate([q0 + h * dh + perm for h in range(B_HEADS)])
    kv_cols = np.concatenate([np.concatenate([A_WIDTH + j * dh + perm,
                                              A_WIDTH + B_KV_WIDTH + j * dh + np.arange(dh)])
                              for j in range(B_KV_HEADS)])
    gb_cols = np.arange(q0 + B_WIDTH, q0 + 2 * B_WIDTH)
    a_cols = np.concatenate([np.arange(0, A_WIDTH), np.arange(n_state, n_state + A_WIDTH)])
    w_b = w_in[:, np.concatenate([q_cols, gb_cols, kv_cols])].astype(BF16)
    w_a = w_in[:, a_cols].astype(BF16)
    return w_a, w_b


def _groups(vec_b, vec_c):
    allv = jnp.concatenate([vec_b, vec_c[None, :]], axis=0)
    return jnp.broadcast_to(allv[:, None, :], (allv.shape[0], SUBLANES, allv.shape[1]))


def kernel(x, c, ctx, c_ctx, ada_w, ada_b, norm_w, ev_w_in, ev_conv_w, ev_conv_b, ev_rg_wx, ev_rg_bx,
           ev_rg_wa, ev_rg_ba, ev_rg_lambda, ev_sink, ev_w_out, od_w_in, od_lb_raw, od_gnorm_w, od_w_out,
           final_norm_w):
    b, s_lat, d = x.shape
    t_ctx = ctx.shape[1]
    t_all = t_ctx + s_lat
    assert b == SUBLANES and d == D_MODEL
    assert t_ctx % ROW_TILE == 0 and s_lat % ROW_TILE == 0
    n_ctx_tiles = t_ctx // ROW_TILE

    sc_rows = jnp.zeros((2 * SUBLANES, d), F32)
    sc_rows = sc_rows.at[:b].set(jax.nn.silu(c)).at[b].set(jax.nn.silu(c_ctx))
    mod = _ada_mod(sc_rows, ada_w, ada_b)

    lb_p = jax.nn.softmax(od_lb_raw.astype(F32), axis=0)
    lower_bounds = jnp.cumsum(lb_p, axis=0) - lb_p[0]
    q_tabs, kv_tabs = _rope_tables(s_lat, t_ctx)

    h = jnp.concatenate([ctx, x], axis=1)
    out = None
    for layer in range(DEPTH):
        last = layer == DEPTH - 1
        j = layer // 2
        ml = mod[layer]
        shift = _groups(ml[:b, :d], ml[b, :d])
        scale = _groups(ml[:b, d:2 * d], ml[b, d:2 * d])
        gate = _groups(ml[:b, 2 * d:], ml[b, 2 * d:])
        if layer % 2 == 0:
            w_a, w_b = _even_weights(ev_w_in[j])
            p_b, u = _inproj(h, shift, scale, norm_w[layer], w_b, n_ctx_tiles, emit_u=True)
            u_tm = jnp.transpose(u, (1, 0, 2)).reshape(t_all * b, d)
            p_a = _matmul(u_tm, w_a)
            la = -A_C * jax.nn.softplus(-ev_rg_lambda[j].astype(F32))
            w_cat = jnp.concatenate([ev_rg_wx[j], ev_rg_wa[j]], axis=-1).astype(BF16)
            rg = lambda dr, yf: _rglru_pass(
                p_a, yf, ev_conv_w[j], ev_conv_b[j].reshape(1, -1), w_cat[dr],
                ev_rg_bx[j, dr].reshape(1, -1), ev_rg_ba[j, dr].reshape(1, -1), la[dr].reshape(1, -1),
                t_ctx, s_lat, rev=bool(dr))
            ya_tm = rg(1, rg(0, None))
            ya = jnp.transpose(ya_tm.reshape(t_all, b, A_WIDTH), (1, 0, 2))
            kvr = _rope_kv(p_b, kv_tabs, col_block=2 * B_WIDTH // (2 * B_KV_WIDTH))
            yb = _attention(p_b, kvr, ev_sink[j].astype(F32), q_tabs, t_ctx, gate_col_block=1)
            ys, w_out = [ya, yb], ev_w_out[j].astype(BF16)
        else:
            p = _inproj(h, shift, scale, norm_w[layer], od_w_in[j].astype(BF16), n_ctx_tiles, emit_u=False)
            ys, w_out = [_gla(p, lower_bounds[j], od_gnorm_w[j], t_ctx)], od_w_out[j].astype(BF16)
        if last:
            out = _outproj(ys, w_out, h, gate, n_ctx_tiles, final_w=final_norm_w)
        else:
            h = _outproj(ys, w_out, h, gate, n_ctx_tiles)
    return out
```

```python
import functools

import jax
import jax.numpy as jnp
import numpy as np
from jax import lax
from jax.experimental import pallas as pl
from jax.experimental.pallas import tpu as pltpu

F32 = jnp.float32
BF16 = jnp.bfloat16

D_MODEL = 1024
DEPTH = 4
GRID_W = 64
RMS_EPS = 1e-6
NEG_BIG = -1e30
F_MIN = 1e-30

A_WIDTH = D_MODEL
A_HEADS = 8
A_BLOCK = A_WIDTH // A_HEADS
A_CONV = 4
A_C = 8.0
B_HEADS = 16
B_KV_HEADS = 4
B_HEAD_DIM = 64
B_GROUP = B_HEADS // B_KV_HEADS
B_WIDTH = B_HEADS * B_HEAD_DIM
B_KV_WIDTH = B_KV_HEADS * B_HEAD_DIM
WINDOW = 128
ROPE_BASE = 10000.0
ATTN_SCALE = B_HEAD_DIM ** -0.5
LOG2E = 1.4426950408889634
C_EXPAND = 128
C_HEADS = D_MODEL // C_EXPAND
C_CHUNK = 64

SUBLANES = 8
LANES = 128
ROW_TILE = 256
GLA_HEADS_PER_STEP = 2
VMEM_LIMIT = 56 * 1024 * 1024

_NT = (((1,), (1,)), ((), ()))
_TN = (((0,), (0,)), ((), ()))


def _params(*sem):
    return pltpu.CompilerParams(dimension_semantics=sem, vmem_limit_bytes=VMEM_LIMIT)


def _sigmoid(x):
    return 0.5 * jnp.tanh(0.5 * x) + 0.5


def _silu(x):
    return x * _sigmoid(x)


def _ada_kernel(sc_ref, w_ref, b_ref, o_ref):
    o_ref[...] = jnp.dot(sc_ref[...], w_ref[...], preferred_element_type=F32,
                         precision=lax.Precision.HIGHEST) + b_ref[...]


def _ada_mod(sc, ada_w, ada_b):
    depth, d, n = ada_w.shape
    rows = sc.shape[0]
    tn = 1024
    return pl.pallas_call(
        _ada_kernel,
        grid=(depth, n // tn),
        in_specs=[pl.BlockSpec((rows, d), lambda l, j: (0, 0)),
                  pl.BlockSpec((None, d, tn), lambda l, j: (l, 0, j)),
                  pl.BlockSpec((None, 1, tn), lambda l, j: (l, 0, j))],
        out_specs=pl.BlockSpec((None, rows, tn), lambda l, j: (l, 0, j)),
        out_shape=jax.ShapeDtypeStruct((depth, rows, n), F32),
        compiler_params=_params("parallel", "parallel"),
        name="ada_mod",
    )(sc, ada_w, ada_b.reshape(depth, 1, n))


def _inproj_kernel(h_ref, sh_ref, sc_ref, nw_ref, w_ref, *rest, emit_u, col_chunk):
    if emit_u:
        p_ref, u_ref, u_scr = rest
    else:
        p_ref, u_scr = rest
    x = h_ref[...]
    tm, d = x.shape
    y = x * lax.rsqrt(jnp.mean(x * x, axis=-1, keepdims=True) + RMS_EPS) * nw_ref[...]
    y3 = y.reshape(tm // SUBLANES, SUBLANES, d)
    u = (y3 * (1.0 + sc_ref[...])[None] + sh_ref[...][None]).reshape(tm, d)
    ub = u.astype(BF16)
    u_scr[...] = ub
    if emit_u:
        u_ref[...] = ub
    n = p_ref.shape[-1]
    for c in range(0, n, col_chunk):
        p_ref[:, c:c + col_chunk] = jnp.dot(u_scr[...], w_ref[:, c:c + col_chunk],
                                            preferred_element_type=F32)


def _inproj(h, shift, scale, norm_w, w, n_ctx_tiles, emit_u):
    b, t, d = h.shape
    n = w.shape[1]
    tm = ROW_TILE

    def grp(i, j):
        return jnp.where(j < n_ctx_tiles, b, i)

    out_shape = [jax.ShapeDtypeStruct((b, t, n), F32)]
    out_specs = [pl.BlockSpec((None, tm, n), lambda i, j: (i, j, 0))]
    if emit_u:
        out_shape.append(jax.ShapeDtypeStruct((b, t, d), BF16))
        out_specs.append(pl.BlockSpec((None, tm, d), lambda i, j: (i, j, 0)))
    res = pl.pallas_call(
        functools.partial(_inproj_kernel, emit_u=emit_u, col_chunk=512),
        grid=(b, t // tm),
        in_specs=[pl.BlockSpec((None, tm, d), lambda i, j: (i, j, 0)),
                  pl.BlockSpec((None, SUBLANES, d), lambda i, j: (grp(i, j), 0, 0)),
                  pl.BlockSpec((None, SUBLANES, d), lambda i, j: (grp(i, j), 0, 0)),
                  pl.BlockSpec((1, d), lambda i, j: (0, 0)),
                  pl.BlockSpec((d, n), lambda i, j: (0, 0))],
        out_specs=out_specs,
        out_shape=out_shape,
        scratch_shapes=[pltpu.VMEM((tm, d), BF16)],
        compiler_params=_params("parallel", "parallel"),
        name="inproj",
    )(h, shift, scale, norm_w.reshape(1, d), w)
    return res if emit_u else res[0]


def _matmul_kernel(u_ref, w_ref, p_ref, *, col_chunk):
    n = p_ref.shape[-1]
    for c in range(0, n, col_chunk):
        p_ref[:, c:c + col_chunk] = jnp.dot(u_ref[...], w_ref[:, c:c + col_chunk],
                                            preferred_element_type=F32)


def _matmul(u, w):
    r, d = u.shape
    n = w.shape[1]
    tm = 512
    return pl.pallas_call(
        functools.partial(_matmul_kernel, col_chunk=512),
        grid=(r // tm,),
        in_specs=[pl.BlockSpec((tm, d), lambda i: (i, 0)),
                  pl.BlockSpec((d, n), lambda i: (0, 0))],
        out_specs=pl.BlockSpec((tm, n), lambda i: (i, 0)),
        out_shape=jax.ShapeDtypeStruct((r, n), F32),
        compiler_params=_params("parallel"),
        name="matmul_tm",
    )(u, w)


def _scan_tile(i, n_ctx, n_lat, rev):
    if not rev:
        return i
    return jnp.where(i < n_ctx, n_ctx - 1 - i, 2 * n_ctx + n_lat - 1 - i)


def _rglru_kernel(*refs, rev, tt, n_ctx, n_lat):
    if rev:
        (xa_ref, xp_ref, xn1_ref, xn2_ref, ga_ref, yf_ref, cw_ref, cb_ref, w_ref, bx_ref, ba_ref,
         la_ref, out_ref, h_scr, xpad_scr, u_scr, a_scr, b_scr) = refs
    else:
        (xa_ref, xp_ref, xn1_ref, xn2_ref, cw_ref, cb_ref, w_ref, bx_ref, ba_ref,
         la_ref, out_ref, h_scr, xpad_scr, u_scr, a_scr, b_scr) = refs
    i = pl.program_id(0)
    tile = _scan_tile(i, n_ctx, n_lat, rev)
    seg_first = jnp.logical_or(tile == 0, tile == n_ctx)
    seg_last = jnp.logical_or(tile == n_ctx - 1, tile == n_ctx + n_lat - 1)
    rows = tt * SUBLANES

    @pl.when(i == 0)
    def _():
        h_scr[...] = jnp.zeros_like(h_scr)

    xpad_scr[0:8, :] = jnp.where(seg_first, 0.0, xp_ref[...])
    xpad_scr[8:8 + rows, :] = xa_ref[...]
    xpad_scr[8 + rows:16 + rows, :] = jnp.where(seg_last, 0.0, xn1_ref[...])
    xpad_scr[16 + rows:24 + rows, :] = jnp.where(seg_last, 0.0, xn2_ref[...])
    u = cb_ref[...] + xpad_scr[0:rows, :] * cw_ref[0:1, :]
    for k in range(1, A_CONV):
        u = u + xpad_scr[8 * k:8 * k + rows, :] * cw_ref[k:k + 1, :]
    u_scr[...] = u

    for hd in range(A_HEADS):
        sl = slice(hd * A_BLOCK, (hd + 1) * A_BLOCK)
        uh = u_scr[:, sl]
        g = jnp.dot(uh.astype(BF16), w_ref[hd], preferred_element_type=F32)
        gate_x = _sigmoid(g[:, :A_BLOCK] + bx_ref[:, sl])
        gate_a = _sigmoid(g[:, A_BLOCK:] + ba_ref[:, sl])
        log_a = gate_a * la_ref[:, sl]
        a = jnp.exp(log_a)
        a_scr[:, sl] = a
        b_scr[:, sl] = jnp.sqrt(-jnp.tanh(log_a) * (1.0 + a * a)) * gate_x * uh

    def step(s, h):
        t = (tt - 1 - s) if rev else s
        r0 = pl.multiple_of(t * SUBLANES, SUBLANES)
        h = a_scr[pl.ds(r0, SUBLANES), :] * h + b_scr[pl.ds(r0, SUBLANES), :]
        b_scr[pl.ds(r0, SUBLANES), :] = h
        return h

    h_scr[...] = lax.fori_loop(0, tt, step, h_scr[...], unroll=8)
    if rev:
        out_ref[...] = ((yf_ref[...] + b_scr[...]) * _silu(ga_ref[...])).astype(BF16)
    else:
        out_ref[...] = b_scr[...]


def _rglru_pass(p_a, yf, conv_w, conv_b, w_cat, bx, ba, la, t_ctx, t_lat, rev):
    w = A_WIDTH
    tt = 64
    rows = tt * SUBLANES
    n_ctx, n_lat = t_ctx // tt, t_lat // tt
    t_all = t_ctx + t_lat
    tile = functools.partial(_scan_tile, n_ctx=n_ctx, n_lat=n_lat, rev=rev)
    const = lambda i: (0, 0)
    in_specs = [pl.BlockSpec((rows, w), lambda i: (tile(i), 0)),
                pl.BlockSpec((SUBLANES, w), lambda i: (jnp.maximum(tile(i) * tt - 1, 0), 0)),
                pl.BlockSpec((SUBLANES, w), lambda i: (jnp.minimum(tile(i) * tt + tt, t_all - 1), 0)),
                pl.BlockSpec((SUBLANES, w), lambda i: (jnp.minimum(tile(i) * tt + tt + 1, t_all - 1), 0))]
    args = [p_a, p_a, p_a, p_a]
    if rev:
        in_specs += [pl.BlockSpec((rows, w), lambda i: (tile(i), 1)),
                     pl.BlockSpec((rows, w), lambda i: (tile(i), 0))]
        args += [p_a, yf]
    in_specs += [pl.BlockSpec((A_CONV, w), const), pl.BlockSpec((1, w), const),
                 pl.BlockSpec((A_HEADS, A_BLOCK, 2 * A_BLOCK), lambda i: (0, 0, 0)),
                 pl.BlockSpec((1, w), const), pl.BlockSpec((1, w), const), pl.BlockSpec((1, w), const)]
    args += [conv_w, conv_b, w_cat, bx, ba, la]
    return pl.pallas_call(
        functools.partial(_rglru_kernel, rev=rev, tt=tt, n_ctx=n_ctx, n_lat=n_lat),
        grid=(n_ctx + n_lat,),
        in_specs=in_specs,
        out_specs=pl.BlockSpec((rows, w), lambda i: (tile(i), 0)),
        out_shape=jax.ShapeDtypeStruct((t_all * SUBLANES, w), BF16 if rev else F32),
        scratch_shapes=[pltpu.VMEM((SUBLANES, w), F32),
                        pltpu.VMEM((rows + 3 * SUBLANES, w), F32),
                        pltpu.VMEM((rows, w), F32),
                        pltpu.VMEM((rows, w), F32),
                        pltpu.VMEM((rows, w), F32)],
        compiler_params=_params("arbitrary"),
        name="rglru_bwd" if rev else "rglru_fwd",
    )(*args)


def _rope(x, cos, sin_a, sin_b):
    return x * cos + pltpu.roll(x, 96, 1) * sin_a + pltpu.roll(x, 32, 1) * sin_b


def _rope_kv_kernel(kv_ref, cos_ref, sa_ref, sb_ref, o_ref):
    cos, sa, sb = cos_ref[...], sa_ref[...], sb_ref[...]
    for g in range(B_KV_HEADS):
        sl = slice(g * LANES, (g + 1) * LANES)
        o_ref[:, sl] = _rope(kv_ref[:, sl], cos, sa, sb).astype(BF16)


def _rope_kv(p_b, tabs, col_block):
    b, t, _ = p_b.shape
    tm = 3 * ROW_TILE if t % (3 * ROW_TILE) == 0 else ROW_TILE
    wkv = 2 * B_KV_WIDTH
    tab = pl.BlockSpec((tm, LANES), lambda i, j: (j, 0))
    return pl.pallas_call(
        _rope_kv_kernel,
        grid=(b, t // tm),
        in_specs=[pl.BlockSpec((None, tm, wkv), lambda i, j: (i, j, col_block)), tab, tab, tab],
        out_specs=pl.BlockSpec((None, tm, wkv), lambda i, j: (i, j, 0)),
        out_shape=jax.ShapeDtypeStruct((b, t, wkv), BF16),
        compiler_params=_params("parallel", "parallel"),
        name="rope_kv",
    )(p_b, *tabs)


def _attn_kernel(sink_ref, q_ref, kc_ref, kp_ref, ko_ref, kn_ref, gb_ref, cos_ref, sa_ref, sb_ref,
                 o_ref, s_scr, p_scr, bias_scr, *, n_ctx_blk, n_lat_blk):
    n = pl.program_id(1)
    m = n - n_ctx_blk
    is_lat = n >= n_ctx_blk
    has_prev = jnp.logical_and(is_lat, m >= 1)
    has_next = jnp.logical_and(is_lat, m <= n_lat_blk - 2)
    w = WINDOW
    dh = B_HEAD_DIM
    lc = kc_ref.shape[0]

    cos, sa, sb = cos_ref[...], sa_ref[...], sb_ref[...]
    low = lax.broadcasted_iota(jnp.int32, (w, LANES), 1) < dh
    kj = lax.broadcasted_iota(jnp.int32, (w, w), 0)
    qi = lax.broadcasted_iota(jnp.int32, (w, w), 1)
    neg = jnp.full((w, w), NEG_BIG, F32)
    zer = jnp.zeros((w, w), F32)
    bias_scr[0] = jnp.where(jnp.logical_and(has_prev, kj >= qi), zer, neg)
    bias_scr[1] = jnp.where(jnp.logical_and(is_lat, kj >= 0), zer, neg)
    bias_scr[2] = jnp.where(jnp.logical_and(has_next, kj <= qi), zer, neg)
    piece = 32
    pieces = [(r0, bi, p0) for bi, (r0, nr) in enumerate(((0, lc), (lc, w), (lc + w, w), (lc + 2 * w, w)))
              for p0 in range(0, nr, piece)]

    def scores(g, cb, r0, bi, p0):
        x = s_scr[g, r0 + p0:r0 + p0 + piece, cb]
        return x if bi == 0 else x + bias_scr[bi - 1, p0:p0 + piece, :]

    def tree(xs, op):
        while len(xs) > 1:
            xs = [op(xs[i], xs[i + 1]) for i in range(0, len(xs) - 1, 2)] + ([xs[-1]] if len(xs) % 2 else [])
        return xs[0]

    def fold8(x, op):
        return op(x.reshape(x.shape[0] // SUBLANES, SUBLANES, x.shape[1]), axis=0)

    for g in range(B_KV_HEADS):
        parts = []
        for c in range(2):
            col0 = g * B_GROUP * dh + c * LANES
            qc = _rope(q_ref[:, col0:col0 + LANES], cos, sa, sb) * (ATTN_SCALE * LOG2E)
            parts.append(jnp.where(low, qc, 0.0).astype(BF16))
            parts.append(jnp.where(low, pltpu.roll(qc, dh, 1), 0.0).astype(BF16))
        qs = jnp.concatenate(parts, axis=0)
        ks = slice(g * LANES, (g + 1) * LANES)
        kv_all = jnp.concatenate([kc_ref[:, ks], kp_ref[:, ks], ko_ref[:, ks], kn_ref[:, ks]], axis=0)
        s_scr[g] = lax.dot_general(kv_all, qs, _NT, preferred_element_type=F32)

        rdens = []
        for h in range(B_GROUP):
            cb = slice(h * w, (h + 1) * w)
            snk = sink_ref[g * B_GROUP + h] * LOG2E
            mx8 = tree([fold8(scores(g, cb, r0, bi, p0), jnp.max) for r0, bi, p0 in pieces], jnp.maximum)
            mx = jnp.maximum(jnp.max(mx8, axis=0, keepdims=True), snk)
            sums = []
            for r0, bi, p0 in pieces:
                p = jnp.exp2(scores(g, cb, r0, bi, p0) - mx)
                sums.append(fold8(p, jnp.sum))
                p_scr[g, r0 + p0:r0 + p0 + piece, cb] = p.astype(BF16)
            den8 = tree(sums, jnp.add)
            rdens.append(1.0 / (jnp.sum(den8, axis=0, keepdims=True) + jnp.exp2(snk - mx)))

        ot = lax.dot_general(kv_all, p_scr[g], _TN, preferred_element_type=F32)
        ov = ot[dh:, :] * jnp.concatenate(rdens, axis=1)
        for c in range(2):
            pair = jnp.concatenate([ov[:, (2 * c) * w:(2 * c + 1) * w], ov[:, (2 * c + 1) * w:(2 * c + 2) * w]],
                                   axis=0)
            col0 = g * B_GROUP * dh + c * LANES
            y = pair.T * _silu(gb_ref[:, col0:col0 + LANES])
            o_ref[:, col0:col0 + LANES] = y.astype(BF16)


def _attention(p_b, kvr, sink, q_tabs, t_ctx, gate_col_block):
    b, t, _ = p_b.shape
    w = WINDOW
    n_ctx_blk, n_lat_blk = t_ctx // w, (t - t_ctx) // w
    nb = t // w
    wkv = 2 * B_KV_WIDTH
    kvb = lambda f: pl.BlockSpec((None, w, wkv), f)
    tab = pl.BlockSpec((w, LANES), lambda i, n, s: (n, 0))
    grid_spec = pltpu.PrefetchScalarGridSpec(
        num_scalar_prefetch=1,
        grid=(b, nb),
        in_specs=[pl.BlockSpec((None, w, B_WIDTH), lambda i, n, s: (i, n, 0)),
                  pl.BlockSpec((None, t_ctx, wkv), lambda i, n, s: (i, 0, 0)),
                  kvb(lambda i, n, s: (i, jnp.maximum(n - 1, 0), 0)),
                  kvb(lambda i, n, s: (i, n, 0)),
                  kvb(lambda i, n, s: (i, jnp.minimum(n + 1, nb - 1), 0)),
                  pl.BlockSpec((None, w, B_WIDTH), lambda i, n, s: (i, n, gate_col_block)),
                  tab, tab, tab],
        out_specs=pl.BlockSpec((None, w, B_WIDTH), lambda i, n, s: (i, n, 0)),
        scratch_shapes=[pltpu.VMEM((B_KV_HEADS, t_ctx + 3 * w, B_GROUP * w), F32),
                        pltpu.VMEM((B_KV_HEADS, t_ctx + 3 * w, B_GROUP * w), BF16),
                        pltpu.VMEM((3, w, w), F32)],
    )
    return pl.pallas_call(
        functools.partial(_attn_kernel, n_ctx_blk=n_ctx_blk, n_lat_blk=n_lat_blk),
        grid_spec=grid_spec,
        out_shape=jax.ShapeDtypeStruct((b, t, B_WIDTH), BF16),
        compiler_params=_params("parallel", "parallel"),
        name="window_attn",
    )(sink, p_b, kvr, kvr, kvr, kvr, p_b, *q_tabs)


def _gla_region_ids(rev):
    c = C_CHUNK
    ri = lax.broadcasted_iota(jnp.int32, (c, c), 0)
    ci = lax.broadcasted_iota(jnp.int32, (c, c), 1)
    reg = jnp.where(ri // 8 == ci // 8, 0,
                    jnp.where(ri // 16 == ci // 16, 1, jnp.where(ri // 32 == ci // 32, 2, 3)))
    seen = (ci >= ri) if rev else (ci <= ri)
    return jnp.where(seen, reg, -1)


GLA_LEVELS = (8, 16, 32)
GLA_EXP_KINDS = 2 + len(GLA_LEVELS)


def _gla_exponent_matrix(rev):
    c = C_CHUNK
    t = lax.broadcasted_iota(jnp.int32, (c, c), 0)
    s = lax.broadcasted_iota(jnp.int32, (c, c), 1)
    upto = (s >= t) if rev else (s <= t)
    after = jnp.logical_not(upto)
    as_f32 = lambda cond: jnp.where(cond, 1.0, 0.0).astype(F32)
    mats = [as_f32(upto), as_f32(after)]
    for m in GLA_LEVELS:
        later = ((t // m) % 2 == 0) if rev else ((t // m) % 2 == 1)
        pick = jnp.logical_or(jnp.logical_and(later, upto), jnp.logical_and(jnp.logical_not(later), after))
        mats.append(as_f32(jnp.logical_and(t // m == s // m, pick)))
    mat = jnp.concatenate(mats, axis=0).astype(BF16)
    return jnp.concatenate([mat, mat], axis=1)


def _grp(x, j):
    return x[j * SUBLANES:(j + 1) * SUBLANES, :]


def _gla_stage_a1(z, qr, lb, emat, rev, lane):
    c = C_CHUNK
    ng = c // SUBLANES

    half = 0.5 * (1.0 - lb)
    ht = half * jnp.tanh(0.5 * z)
    f = (lb + half) + ht
    kk = half - ht
    ft = jnp.maximum(f, F_MIN)
    q = _silu(qr)

    g2 = jnp.log2(ft)
    g_hi = g2.astype(BF16)
    g_lo = (g2 - g_hi.astype(F32)).astype(BF16)
    expo = jnp.dot(emat, jnp.concatenate([g_hi, g_lo], axis=0), preferred_element_type=F32)

    dparts = []
    for j in range(ng):
        fg, qg, kg = _grp(ft, j), _grp(q, j), _grp(kk, j)
        w_ = kg
        acc = jnp.where(lane == 0, jnp.sum(qg * kg, axis=-1, keepdims=True), 0.0)
        for dlt in range(1, SUBLANES):
            w_ = fg * pltpu.roll(w_, (SUBLANES - 1) if rev else 1, 0)
            red = jnp.sum(qg * w_, axis=-1, keepdims=True)
            acc = jnp.where(lane == (dlt if rev else LANES - dlt), red, acc)
        dparts.append(acc)
    diag = pltpu.roll(jnp.concatenate(dparts, axis=0), 0, 1, stride=1, stride_axis=0)[:, :c]
    return q, kk, expo, diag


def _gla_stage_a2(q, kk, expo, diag, v, rev, reg):
    c = C_CHUNK
    ng = c // SUBLANES
    dec = jnp.exp2(expo)
    last = 0 if rev else c - 1
    total = dec[last:last + 1, :]
    qd = (q * dec[0:c]).astype(BF16)
    kd = (kk * dec[c:2 * c]).astype(BF16)
    grp = _grp

    zero = jnp.zeros((SUBLANES, LANES), F32)
    levels = []
    for li, m in enumerate(GLA_LEVELS):
        fac = dec[(2 + li) * c:(3 + li) * c]
        qparts, kparts = [], []
        for j in range(ng):
            later = ((j * SUBLANES // m) % 2 == 1) != rev
            prod = (grp(q, j) if later else grp(kk, j)) * grp(fac, j)
            qparts.append(prod if later else zero)
            kparts.append(zero if later else prod)
        levels.append(lax.dot_general(jnp.concatenate(qparts, axis=0).astype(BF16),
                                      jnp.concatenate(kparts, axis=0).astype(BF16), _NT,
                                      preferred_element_type=F32))

    attn = jnp.where(reg == 0, diag,
                     jnp.where(reg == 1, levels[0],
                               jnp.where(reg == 2, levels[1], jnp.where(reg == 3, levels[2], 0.0))))
    return attn.astype(BF16), qd, kd, v.astype(BF16), total


def _gla_stage_b(attn, qd, kd, vb, total, st):
    o = jnp.dot(attn, vb, preferred_element_type=F32)
    o = o + lax.dot_general(qd, st.astype(BF16), _NT, preferred_element_type=F32)
    st_new = total * st + lax.dot_general(vb, kd, _TN, preferred_element_type=F32)
    return o, st_new


def _gla_kernel(zf_ref, zb_ref, v_ref, q_ref, og_ref, lb_ref, gw_ref, y_ref, of_scr, ob_scr, st_scr,
                emat_scr, attn_c, qd_c, kd_c, vb_c, tot_c, q_c, k_c, expo_c, diag_c,
                *, n_ctx, n_lat, heads):
    c = C_CHUNK
    row8 = lax.broadcasted_iota(jnp.int32, (SUBLANES, LANES), 0)
    lane = lax.broadcasted_iota(jnp.int32, (SUBLANES, LANES), 1)
    regs = (_gla_region_ids(False), _gla_region_ids(True))
    n_all = n_ctx + n_lat
    st_scr[...] = jnp.zeros_like(st_scr)
    emat_scr[0] = _gla_exponent_matrix(False)
    emat_scr[1] = _gla_exponent_matrix(True)
    streams = [(hh, rev) for hh in range(heads) for rev in (False, True)]

    def rows_of(i, rev):
        if isinstance(i, int):
            cidx = ((n_ctx - 1 - i) if i < n_ctx else (2 * n_ctx + n_lat - 1 - i)) if rev else i
            return pl.ds(cidx * c, c)
        cidx = jnp.where(i < n_ctx, n_ctx - 1 - i, 2 * n_ctx + n_lat - 1 - i) if rev else i
        return pl.ds(pl.multiple_of(cidx * c, c), c)

    def stage_a1(i):
        for k, (hh, rev) in enumerate(streams):
            ls = slice(hh * LANES, (hh + 1) * LANES)
            rows = rows_of(i, rev)
            z_ref = zb_ref if rev else zf_ref
            q, kk, expo, diag = _gla_stage_a1(z_ref[rows, ls], q_ref[rows, ls], lb_ref[:, ls],
                                              emat_scr[int(rev)], rev, lane)
            q_c[k] = q
            k_c[k] = kk
            expo_c[k] = expo
            diag_c[k] = diag

    def stage_a2(i):
        for k, (hh, rev) in enumerate(streams):
            ls = slice(hh * LANES, (hh + 1) * LANES)
            attn, qd, kd, vb, total = _gla_stage_a2(q_c[k], k_c[k], expo_c[k], diag_c[k],
                                                    v_ref[rows_of(i, rev), ls], rev, regs[int(rev)])
            attn_c[k] = attn
            qd_c[k] = qd
            kd_c[k] = kd
            vb_c[k] = vb
            tot_c[k] = jnp.broadcast_to(total, (SUBLANES, LANES))

    def stage_b(i):
        for k, (hh, rev) in enumerate(streams):
            ls = slice(hh * LANES, (hh + 1) * LANES)
            o, st = _gla_stage_b(attn_c[k], qd_c[k], kd_c[k], vb_c[k], tot_c[k, 0:1, :], st_scr[k])
            (ob_scr if rev else of_scr)[rows_of(i, rev), ls] = o
            st_scr[k] = st

    stage_a1(0)
    stage_a2(0)
    stage_a1(1)

    def step(i):
        stage_b(i - 2)
        stage_a2(i - 1)
        stage_a1(i)

    def body(j, carry):
        step(2 + 2 * j)
        step(3 + 2 * j)
        return carry

    lax.fori_loop(0, (n_all - 2) // 2, body, 0)
    if (n_all - 2) % 2:
        step(n_all - 1)
    stage_b(n_all - 2)
    stage_a2(n_all - 1)
    stage_b(n_all - 1)

    gw = gw_ref[...]
    piece = 256

    def fin(i, carry):
        rows = pl.ds(pl.multiple_of(i * piece, piece), piece)
        for hh in range(heads):
            ls = slice(hh * LANES, (hh + 1) * LANES)
            o = of_scr[rows, ls] + ob_scr[rows, ls]
            y = o * lax.rsqrt(jnp.mean(o * o, axis=-1, keepdims=True) + RMS_EPS) * gw
            y_ref[rows, ls] = (y * _silu(og_ref[rows, ls])).astype(BF16)
        return carry

    lax.fori_loop(0, (n_all * c) // piece, fin, 0)


def _gla(p, lb, gw, t_ctx):
    b, t, _ = p.shape
    hps = GLA_HEADS_PER_STEP
    wb = hps * LANES
    nblk = C_HEADS // hps
    col = lambda k: pl.BlockSpec((None, t, wb), lambda i, h: (i, 0, k * nblk + h))
    return pl.pallas_call(
        functools.partial(_gla_kernel, n_ctx=t_ctx // C_CHUNK, n_lat=(t - t_ctx) // C_CHUNK, heads=hps),
        grid=(b, nblk),
        in_specs=[col(0), col(1), col(2), col(3), col(4),
                  pl.BlockSpec((None, 1, wb), lambda i, h: (h, 0, 0)),
                  pl.BlockSpec((1, LANES), lambda i, h: (0, 0))],
        out_specs=pl.BlockSpec((None, t, wb), lambda i, h: (i, 0, h)),
        out_shape=jax.ShapeDtypeStruct((b, t, C_HEADS * LANES), BF16),
        scratch_shapes=[pltpu.VMEM((t, wb), F32), pltpu.VMEM((t, wb), F32),
                        pltpu.VMEM((2 * hps, C_EXPAND, LANES), F32),
                        pltpu.VMEM((2, GLA_EXP_KINDS * C_CHUNK, 2 * C_CHUNK), BF16),
                        pltpu.VMEM((2 * hps, C_CHUNK, C_CHUNK), BF16),
                        pltpu.VMEM((2 * hps, C_CHUNK, LANES), BF16),
                        pltpu.VMEM((2 * hps, C_CHUNK, LANES), BF16),
                        pltpu.VMEM((2 * hps, C_CHUNK, LANES), BF16),
                        pltpu.VMEM((2 * hps, SUBLANES, LANES), F32),
                        pltpu.VMEM((2 * hps, C_CHUNK, LANES), F32),
                        pltpu.VMEM((2 * hps, C_CHUNK, LANES), F32),
                        pltpu.VMEM((2 * hps, GLA_EXP_KINDS * C_CHUNK, LANES), F32),
                        pltpu.VMEM((2 * hps, C_CHUNK, C_CHUNK), F32)],
        compiler_params=_params("parallel", "parallel"),
        name="hgrn2_gla",
    )(p, p, p, p, p, lb.reshape(nblk, 1, wb), gw.reshape(1, LANES))


def _outproj_kernel(*refs, n_in, final):
    y_refs = refs[:n_in]
    w_ref, h_ref, gt_ref = refs[n_in:n_in + 3]
    if final:
        fw_ref, o_ref = refs[n_in + 3:]
    else:
        (o_ref,) = refs[n_in + 3:]
    acc = None
    k0 = 0
    for y_ref in y_refs:
        kw = y_ref.shape[-1]
        part = jnp.dot(y_ref[...], w_ref[k0:k0 + kw, :], preferred_element_type=F32)
        acc = part if acc is None else acc + part
        k0 += kw
    tm, d = acc.shape
    hn = h_ref[...] + (acc.reshape(tm // SUBLANES, SUBLANES, d) * gt_ref[...][None]).reshape(tm, d)
    if final:
        hn = hn * lax.rsqrt(jnp.mean(hn * hn, axis=-1, keepdims=True) + RMS_EPS) * fw_ref[...]
    o_ref[...] = hn


def _outproj(ys, w, h, gate, n_ctx_tiles, final_w=None):
    b, t, d = h.shape
    tm = ROW_TILE
    final = final_w is not None
    off = n_ctx_tiles if final else 0
    nt = t // tm - off

    def grp(i, j):
        return jnp.where(j + off < n_ctx_tiles, b, i)

    in_specs = [pl.BlockSpec((None, tm, y.shape[-1]), lambda i, j: (i, j + off, 0)) for y in ys]
    in_specs += [pl.BlockSpec(w.shape, lambda i, j: (0, 0)),
                 pl.BlockSpec((None, tm, d), lambda i, j: (i, j + off, 0)),
                 pl.BlockSpec((None, SUBLANES, d), lambda i, j: (grp(i, j), 0, 0))]
    args = list(ys) + [w, h, gate]
    if final:
        in_specs.append(pl.BlockSpec((1, d), lambda i, j: (0, 0)))
        args.append(final_w.reshape(1, d))
    return pl.pallas_call(
        functools.partial(_outproj_kernel, n_in=len(ys), final=final),
        grid=(b, nt),
        in_specs=in_specs,
        out_specs=pl.BlockSpec((None, tm, d), lambda i, j: (i, j, 0)),
        out_shape=jax.ShapeDtypeStruct((b, nt * tm, d), F32),
        compiler_params=_params("parallel", "parallel"),
        name="outproj",
    )(*args)


def _rope_tables(s_lat, t_ctx):
    pos = np.arange(s_lat)
    row = (pos // GRID_W).astype(np.float32)
    colp = (pos % GRID_W).astype(np.float32)
    axis_dim = B_HEAD_DIM // 2
    inv = jnp.asarray(ROPE_BASE, F32) ** (-jnp.arange(0, axis_dim, 2, dtype=F32) / axis_dim)
    ang = jnp.concatenate([jnp.asarray(row)[:, None] * inv, jnp.asarray(colp)[:, None] * inv], axis=-1)
    cos, sin = jnp.cos(ang), jnp.sin(ang)
    zer = jnp.zeros_like(sin)
    cos64 = jnp.concatenate([cos, cos], -1)
    sa64 = jnp.concatenate([-sin, zer], -1)
    sb64 = jnp.concatenate([zer, sin], -1)
    one64 = jnp.ones_like(cos64)
    zer64 = jnp.zeros_like(cos64)

    def full(lat128, ident):
        return jnp.concatenate([jnp.broadcast_to(ident, (t_ctx, LANES)), lat128], axis=0)

    ident_c = jnp.ones((1, LANES), F32)
    ident_s = jnp.zeros((1, LANES), F32)
    q_tabs = (full(jnp.concatenate([cos64, cos64], -1), ident_c),
              full(jnp.concatenate([sa64, sa64], -1), ident_s),
              full(jnp.concatenate([sb64, sb64], -1), ident_s))
    kv_tabs = (full(jnp.concatenate([cos64, one64], -1), ident_c),
               full(jnp.concatenate([sa64, zer64], -1), ident_s),
               full(jnp.concatenate([sb64, zer64], -1), ident_s))
    return q_tabs, kv_tabs


def _even_weights(w_in):
    dh = B_HEAD_DIM
    perm = np.concatenate([np.arange(0, dh, 2), np.arange(1, dh, 2)])
    n_state = A_WIDTH + 2 * B_KV_WIDTH
    q0 = n_state + A_WIDTH
    q_cols = np.concatenate([q0 + h * dh + perm for h in range(B_HEADS)])
    kv_cols = np.concatenate([np.concatenate([A_WIDTH + j * dh + perm,
                                              A_WIDTH + B_KV_WIDTH + j * dh + np.arange(dh)])
                              for j in range(B_KV_HEADS)])
    gb_cols = np.arange(q0 + B_WIDTH, q0 + 2 * B_WIDTH)
    a_cols = np.concatenate([np.arange(0, A_WIDTH), np.arange(n_state, n_state + A_WIDTH)])
    w_b = w_in[:, np.concatenate([q_cols, gb_cols, kv_cols])].astype(BF16)
    w_a = w_in[:, a_cols].astype(BF16)
    return w_a, w_b


def _groups(vec_b, vec_c):
    allv = jnp.concatenate([vec_b, vec_c[None, :]], axis=0)
    return jnp.broadcast_to(allv[:, None, :], (allv.shape[0], SUBLANES, allv.shape[1]))


def kernel(x, c, ctx, c_ctx, ada_w, ada_b, norm_w, ev_w_in, ev_conv_w, ev_conv_b, ev_rg_wx, ev_rg_bx,
           ev_rg_wa, ev_rg_ba, ev_rg_lambda, ev_sink, ev_w_out, od_w_in, od_lb_raw, od_gnorm_w, od_w_out,
           final_norm_w):
    b, s_lat, d = x.shape
    t_ctx = ctx.shape[1]
    t_all = t_ctx + s_lat
    assert b == SUBLANES and d == D_MODEL
    assert t_ctx % ROW_TILE == 0 and s_lat % ROW_TILE == 0
    n_ctx_tiles = t_ctx // ROW_TILE

    sc_rows = jnp.zeros((2 * SUBLANES, d), F32)
    sc_rows = sc_rows.at[:b].set(jax.nn.silu(c)).at[b].set(jax.nn.silu(c_ctx))
    mod = _ada_mod(sc_rows, ada_w, ada_b)

    lb_p = jax.nn.softmax(od_lb_raw.astype(F32), axis=0)
    lower_bounds = jnp.cumsum(lb_p, axis=0) - lb_p[0]
    q_tabs, kv_tabs = _rope_tables(s_lat, t_ctx)

    h = jnp.concatenate([ctx, x], axis=1)
    out = None
    for layer in range(DEPTH):
        last = layer == DEPTH - 1
        j = layer // 2
        ml = mod[layer]
        shift = _groups(ml[:b, :d], ml[b, :d])
        scale = _groups(ml[:b, d:2 * d], ml[b, d:2 * d])
        gate = _groups(ml[:b, 2 * d:], ml[b, 2 * d:])
        if layer % 2 == 0:
            w_a, w_b = _even_weights(ev_w_in[j])
            p_b, u = _inproj(h, shift, scale, norm_w[layer], w_b, n_ctx_tiles, emit_u=True)
            u_tm = jnp.transpose(u, (1, 0, 2)).reshape(t_all * b, d)
            p_a = _matmul(u_tm, w_a)
            la = -A_C * jax.nn.softplus(-ev_rg_lambda[j].astype(F32))
            w_cat = jnp.concatenate([ev_rg_wx[j], ev_rg_wa[j]], axis=-1).astype(BF16)
            rg = lambda dr, yf: _rglru_pass(
                p_a, yf, ev_conv_w[j], ev_conv_b[j].reshape(1, -1), w_cat[dr],
                ev_rg_bx[j, dr].reshape(1, -1), ev_rg_ba[j, dr].reshape(1, -1), la[dr].reshape(1, -1),
                t_ctx, s_lat, rev=bool(dr))
            ya_tm = rg(1, rg(0, None))
            ya = jnp.transpose(ya_tm.reshape(t_all, b, A_WIDTH), (1, 0, 2))
            kvr = _rope_kv(p_b, kv_tabs, col_block=2 * B_WIDTH // (2 * B_KV_WIDTH))
            yb = _attention(p_b, kvr, ev_sink[j].astype(F32), q_tabs, t_ctx, gate_col_block=1)
            ys, w_out = [ya, yb], ev_w_out[j].astype(BF16)
        else:
            p = _inproj(h, shift, scale, norm_w[layer], od_w_in[j].astype(BF16), n_ctx_tiles, emit_u=False)
            ys, w_out = [_gla(p, lower_bounds[j], od_gnorm_w[j], t_ctx)], od_w_out[j].astype(BF16)
        if last:
            out = _outproj(ys, w_out, h, gate, n_ctx_tiles, final_w=final_norm_w)
        else:
            h = _outproj(ys, w_out, h, gate, n_ctx_tiles)
    return out
```

```python
import functools

import jax
import jax.numpy as jnp
import numpy as np
from jax import lax
from jax.experimental import pallas as pl
from jax.experimental.pallas import tpu as pltpu

F32 = jnp.float32
BF16 = jnp.bfloat16

D_MODEL = 1024
DEPTH = 4
GRID_W = 64
RMS_EPS = 1e-6
NEG_BIG = -1e30
F_MIN = 1e-30

A_WIDTH = D_MODEL
A_HEADS = 8
A_BLOCK = A_WIDTH // A_HEADS
A_CONV = 4
A_C = 8.0
B_HEADS = 16
B_KV_HEADS = 4
B_HEAD_DIM = 64
B_GROUP = B_HEADS // B_KV_HEADS
B_WIDTH = B_HEADS * B_HEAD_DIM
B_KV_WIDTH = B_KV_HEADS * B_HEAD_DIM
WINDOW = 128
ROPE_BASE = 10000.0
ATTN_SCALE = B_HEAD_DIM ** -0.5
LOG2E = 1.4426950408889634
C_EXPAND = 128
C_HEADS = D_MODEL // C_EXPAND
C_CHUNK = 64

SUBLANES = 8
LANES = 128
ROW_TILE = 256
GLA_HEADS_PER_STEP = 2
VMEM_LIMIT = 56 * 1024 * 1024

_NT = (((1,), (1,)), ((), ()))
_TN = (((0,), (0,)), ((), ()))


def _params(*sem):
    return pltpu.CompilerParams(dimension_semantics=sem, vmem_limit_bytes=VMEM_LIMIT)


def _sigmoid(x):
    return 0.5 * jnp.tanh(0.5 * x) + 0.5


def _silu(x):
    return x * _sigmoid(x)


def _ada_kernel(sc_ref, w_ref, b_ref, o_ref):
    o_ref[...] = jnp.dot(sc_ref[...], w_ref[...], preferred_element_type=F32,
                         precision=lax.Precision.HIGHEST) + b_ref[...]


def _ada_mod(sc, ada_w, ada_b):
    depth, d, n = ada_w.shape
    rows = sc.shape[0]
    tn = 1024
    return pl.pallas_call(
        _ada_kernel,
        grid=(depth, n // tn),
        in_specs=[pl.BlockSpec((rows, d), lambda l, j: (0, 0)),
                  pl.BlockSpec((None, d, tn), lambda l, j: (l, 0, j)),
                  pl.BlockSpec((None, 1, tn), lambda l, j: (l, 0, j))],
        out_specs=pl.BlockSpec((None, rows, tn), lambda l, j: (l, 0, j)),
        out_shape=jax.ShapeDtypeStruct((depth, rows, n), F32),
        compiler_params=_params("parallel", "parallel"),
        name="ada_mod",
    )(sc, ada_w, ada_b.reshape(depth, 1, n))


def _inproj_kernel(h_ref, sh_ref, sc_ref, nw_ref, w_ref, *rest, col_chunk):
    out_refs, u_scr = rest[:-1], rest[-1]
    x = h_ref[...]
    tm, d = x.shape
    y = x * lax.rsqrt(jnp.mean(x * x, axis=-1, keepdims=True) + RMS_EPS) * nw_ref[...]
    y3 = y.reshape(tm // SUBLANES, SUBLANES, d)
    u = (y3 * (1.0 + sc_ref[...])[None] + sh_ref[...][None]).reshape(tm, d)
    u_scr[...] = u.astype(BF16)
    c0 = 0
    for p_ref in out_refs:
        for c in range(0, p_ref.shape[-1], col_chunk):
            p_ref[:, c:c + col_chunk] = jnp.dot(u_scr[...], w_ref[:, c0 + c:c0 + c + col_chunk],
                                                preferred_element_type=F32)
        c0 += p_ref.shape[-1]


def _inproj(h, shift, scale, norm_w, w, n_ctx_tiles, n_time_major=0):
    b, t, d = h.shape
    n = w.shape[1] - n_time_major
    tm = ROW_TILE

    def grp(i, j):
        return jnp.where(j < n_ctx_tiles, b, i)

    out_shape = [jax.ShapeDtypeStruct((b, t, n), F32)]
    out_specs = [pl.BlockSpec((None, tm, n), lambda i, j: (i, j, 0))]
    if n_time_major:
        out_shape.append(jax.ShapeDtypeStruct((t, b * n_time_major), F32))
        out_specs.append(pl.BlockSpec((tm, n_time_major), lambda i, j: (j, i)))
    res = pl.pallas_call(
        functools.partial(_inproj_kernel, col_chunk=512),
        grid=(b, t // tm),
        in_specs=[pl.BlockSpec((None, tm, d), lambda i, j: (i, j, 0)),
                  pl.BlockSpec((None, SUBLANES, d), lambda i, j: (grp(i, j), 0, 0)),
                  pl.BlockSpec((None, SUBLANES, d), lambda i, j: (grp(i, j), 0, 0)),
                  pl.BlockSpec((1, d), lambda i, j: (0, 0)),
                  pl.BlockSpec(w.shape, lambda i, j: (0, 0))],
        out_specs=out_specs,
        out_shape=out_shape,
        scratch_shapes=[pltpu.VMEM((tm, d), BF16)],
        compiler_params=_params("parallel", "parallel"),
        name="inproj",
    )(h, shift, scale, norm_w.reshape(1, d), w)
    if n_time_major:
        return res[0], res[1].reshape(t * b, n_time_major)
    return res[0]


def _scan_tile(i, n_ctx, n_lat, rev):
    if not rev:
        return i
    return jnp.where(i < n_ctx, n_ctx - 1 - i, 2 * n_ctx + n_lat - 1 - i)


def _rglru_kernel(*refs, rev, tt, n_ctx, n_lat):
    if rev:
        (xa_ref, xp_ref, xn1_ref, xn2_ref, ga_ref, yf_ref, cw_ref, cb_ref, w_ref, bx_ref, ba_ref,
         la_ref, out_ref, h_scr, xpad_scr, u_scr, a_scr, b_scr) = refs
    else:
        (xa_ref, xp_ref, xn1_ref, xn2_ref, cw_ref, cb_ref, w_ref, bx_ref, ba_ref,
         la_ref, out_ref, h_scr, xpad_scr, u_scr, a_scr, b_scr) = refs
    i = pl.program_id(0)
    tile = _scan_tile(i, n_ctx, n_lat, rev)
    seg_first = jnp.logical_or(tile == 0, tile == n_ctx)
    seg_last = jnp.logical_or(tile == n_ctx - 1, tile == n_ctx + n_lat - 1)
    rows = tt * SUBLANES

    @pl.when(i == 0)
    def _():
        h_scr[...] = jnp.zeros_like(h_scr)

    xpad_scr[0:8, :] = jnp.where(seg_first, 0.0, xp_ref[...])
    xpad_scr[8:8 + rows, :] = xa_ref[...]
    xpad_scr[8 + rows:16 + rows, :] = jnp.where(seg_last, 0.0, xn1_ref[...])
    xpad_scr[16 + rows:24 + rows, :] = jnp.where(seg_last, 0.0, xn2_ref[...])
    u = cb_ref[...] + xpad_scr[0:rows, :] * cw_ref[0:1, :]
    for k in range(1, A_CONV):
        u = u + xpad_scr[8 * k:8 * k + rows, :] * cw_ref[k:k + 1, :]
    u_scr[...] = u

    for hd in range(A_HEADS):
        sl = slice(hd * A_BLOCK, (hd + 1) * A_BLOCK)
        uh = u_scr[:, sl]
        g = jnp.dot(uh.astype(BF16), w_ref[hd], preferred_element_type=F32)
        gate_x = _sigmoid(g[:, :A_BLOCK] + bx_ref[:, sl])
        gate_a = _sigmoid(g[:, A_BLOCK:] + ba_ref[:, sl])
        log_a = gate_a * la_ref[:, sl]
        a = jnp.exp(log_a)
        a_scr[:, sl] = a
        b_scr[:, sl] = jnp.sqrt(-jnp.tanh(log_a) * (1.0 + a * a)) * gate_x * uh

    def step(s, h):
        t = (tt - 1 - s) if rev else s
        r0 = pl.multiple_of(t * SUBLANES, SUBLANES)
        h = a_scr[pl.ds(r0, SUBLANES), :] * h + b_scr[pl.ds(r0, SUBLANES), :]
        b_scr[pl.ds(r0, SUBLANES), :] = h
        return h

    h_scr[...] = lax.fori_loop(0, tt, step, h_scr[...], unroll=8)
    if rev:
        out_ref[...] = ((yf_ref[...] + b_scr[...]) * _silu(ga_ref[...])).astype(BF16)
    else:
        out_ref[...] = b_scr[...]


def _rglru_pass(p_a, yf, conv_w, conv_b, w_cat, bx, ba, la, t_ctx, t_lat, rev):
    w = A_WIDTH
    tt = 64
    rows = tt * SUBLANES
    n_ctx, n_lat = t_ctx // tt, t_lat // tt
    t_all = t_ctx + t_lat
    tile = functools.partial(_scan_tile, n_ctx=n_ctx, n_lat=n_lat, rev=rev)
    const = lambda i: (0, 0)
    in_specs = [pl.BlockSpec((rows, w), lambda i: (tile(i), 0)),
                pl.BlockSpec((SUBLANES, w), lambda i: (jnp.maximum(tile(i) * tt - 1, 0), 0)),
                pl.BlockSpec((SUBLANES, w), lambda i: (jnp.minimum(tile(i) * tt + tt, t_all - 1), 0)),
                pl.BlockSpec((SUBLANES, w), lambda i: (jnp.minimum(tile(i) * tt + tt + 1, t_all - 1), 0))]
    args = [p_a, p_a, p_a, p_a]
    if rev:
        in_specs += [pl.BlockSpec((rows, w), lambda i: (tile(i), 1)),
                     pl.BlockSpec((rows, w), lambda i: (tile(i), 0))]
        args += [p_a, yf]
    in_specs += [pl.BlockSpec((A_CONV, w), const), pl.BlockSpec((1, w), const),
                 pl.BlockSpec((A_HEADS, A_BLOCK, 2 * A_BLOCK), lambda i: (0, 0, 0)),
                 pl.BlockSpec((1, w), const), pl.BlockSpec((1, w), const), pl.BlockSpec((1, w), const)]
    args += [conv_w, conv_b, w_cat, bx, ba, la]
    return pl.pallas_call(
        functools.partial(_rglru_kernel, rev=rev, tt=tt, n_ctx=n_ctx, n_lat=n_lat),
        grid=(n_ctx + n_lat,),
        in_specs=in_specs,
        out_specs=pl.BlockSpec((rows, w), lambda i: (tile(i), 0)),
        out_shape=jax.ShapeDtypeStruct((t_all * SUBLANES, w), BF16 if rev else F32),
        scratch_shapes=[pltpu.VMEM((SUBLANES, w), F32),
                        pltpu.VMEM((rows + 3 * SUBLANES, w), F32),
                        pltpu.VMEM((rows, w), F32),
                        pltpu.VMEM((rows, w), F32),
                        pltpu.VMEM((rows, w), F32)],
        compiler_params=_params("arbitrary"),
        name="rglru_bwd" if rev else "rglru_fwd",
    )(*args)


def _rope(x, cos, sin_a, sin_b):
    return x * cos + pltpu.roll(x, LANES - 1, 1) * sin_a + pltpu.roll(x, 1, 1) * sin_b


def _rope_kv_kernel(kv_ref, cos_ref, sa_ref, sb_ref, o_ref):
    cos, sa, sb = cos_ref[...], sa_ref[...], sb_ref[...]
    for g in range(B_KV_HEADS):
        sl = slice(g * LANES, (g + 1) * LANES)
        o_ref[:, sl] = _rope(kv_ref[:, sl], cos, sa, sb).astype(BF16)


def _rope_kv(p_b, tabs, col_block):
    b, t, _ = p_b.shape
    tm = 3 * ROW_TILE if t % (3 * ROW_TILE) == 0 else ROW_TILE
    wkv = 2 * B_KV_WIDTH
    tab = pl.BlockSpec((tm, LANES), lambda i, j: (j, 0))
    return pl.pallas_call(
        _rope_kv_kernel,
        grid=(b, t // tm),
        in_specs=[pl.BlockSpec((None, tm, wkv), lambda i, j: (i, j, col_block)), tab, tab, tab],
        out_specs=pl.BlockSpec((None, tm, wkv), lambda i, j: (i, j, 0)),
        out_shape=jax.ShapeDtypeStruct((b, t, wkv), BF16),
        compiler_params=_params("parallel", "parallel"),
        name="rope_kv",
    )(p_b, *tabs)


def _attn_kernel(sink_ref, q_ref, kc_ref, kp_ref, ko_ref, kn_ref, gb_ref, cos_ref, sa_ref, sb_ref,
                 o_ref, s_scr, p_scr, bias_scr, *, n_ctx_blk, n_lat_blk):
    n = pl.program_id(1)
    m = n - n_ctx_blk
    is_lat = n >= n_ctx_blk
    has_prev = jnp.logical_and(is_lat, m >= 1)
    has_next = jnp.logical_and(is_lat, m <= n_lat_blk - 2)
    w = WINDOW
    dh = B_HEAD_DIM
    lc = kc_ref.shape[0]

    cos, sa, sb = cos_ref[...], sa_ref[...], sb_ref[...]
    low = lax.broadcasted_iota(jnp.int32, (w, LANES), 1) < dh
    kj = lax.broadcasted_iota(jnp.int32, (w, w), 0)
    qi = lax.broadcasted_iota(jnp.int32, (w, w), 1)
    neg = jnp.full((w, w), NEG_BIG, F32)
    zer = jnp.zeros((w, w), F32)
    bias_scr[0] = jnp.where(jnp.logical_and(has_prev, kj >= qi), zer, neg)
    bias_scr[1] = jnp.where(jnp.logical_and(is_lat, kj >= 0), zer, neg)
    bias_scr[2] = jnp.where(jnp.logical_and(has_next, kj <= qi), zer, neg)
    piece = 32
    pieces = [(r0, bi, p0) for bi, (r0, nr) in enumerate(((0, lc), (lc, w), (lc + w, w), (lc + 2 * w, w)))
              for p0 in range(0, nr, piece)]

    def scores(g, cb, r0, bi, p0):
        x = s_scr[g, r0 + p0:r0 + p0 + piece, cb]
        return x if bi == 0 else x + bias_scr[bi - 1, p0:p0 + piece, :]

    def tree(xs, op):
        while len(xs) > 1:
            xs = [op(xs[i], xs[i + 1]) for i in range(0, len(xs) - 1, 2)] + ([xs[-1]] if len(xs) % 2 else [])
        return xs[0]

    def fold8(x, op):
        return op(x.reshape(x.shape[0] // SUBLANES, SUBLANES, x.shape[1]), axis=0)

    for g in range(B_KV_HEADS):
        parts = []
        for c in range(2):
            col0 = g * B_GROUP * dh + c * LANES
            qc = _rope(q_ref[:, col0:col0 + LANES], cos, sa, sb) * (ATTN_SCALE * LOG2E)
            parts.append(jnp.where(low, qc, 0.0).astype(BF16))
            parts.append(jnp.where(low, pltpu.roll(qc, dh, 1), 0.0).astype(BF16))
        qs = jnp.concatenate(parts, axis=0)
        ks = slice(g * LANES, (g + 1) * LANES)
        kv_all = jnp.concatenate([kc_ref[:, ks], kp_ref[:, ks], ko_ref[:, ks], kn_ref[:, ks]], axis=0)
        s_scr[g] = lax.dot_general(kv_all, qs, _NT, preferred_element_type=F32)

        rdens = []
        for h in range(B_GROUP):
            cb = slice(h * w, (h + 1) * w)
            snk = sink_ref[g * B_GROUP + h] * LOG2E
            mx8 = tree([fold8(scores(g, cb, r0, bi, p0), jnp.max) for r0, bi, p0 in pieces], jnp.maximum)
            mx = jnp.maximum(jnp.max(mx8, axis=0, keepdims=True), snk)
            sums = []
            for r0, bi, p0 in pieces:
                p = jnp.exp2(scores(g, cb, r0, bi, p0) - mx)
                sums.append(fold8(p, jnp.sum))
                p_scr[g, r0 + p0:r0 + p0 + piece, cb] = p.astype(BF16)
            den8 = tree(sums, jnp.add)
            rdens.append(1.0 / (jnp.sum(den8, axis=0, keepdims=True) + jnp.exp2(snk - mx)))

        ot = lax.dot_general(kv_all, p_scr[g], _TN, preferred_element_type=F32)
        ov = ot[dh:, :] * jnp.concatenate(rdens, axis=1)
        for c in range(2):
            pair = jnp.concatenate([ov[:, (2 * c) * w:(2 * c + 1) * w], ov[:, (2 * c + 1) * w:(2 * c + 2) * w]],
                                   axis=0)
            col0 = g * B_GROUP * dh + c * LANES
            y = pair.T * _silu(gb_ref[:, col0:col0 + LANES])
            o_ref[:, col0:col0 + LANES] = y.astype(BF16)


def _attention(p_b, kvr, sink, q_tabs, t_ctx, gate_col_block):
    b, t, _ = p_b.shape
    w = WINDOW
    n_ctx_blk, n_lat_blk = t_ctx // w, (t - t_ctx) // w
    nb = t // w
    wkv = 2 * B_KV_WIDTH
    kvb = lambda f: pl.BlockSpec((None, w, wkv), f)
    tab = pl.BlockSpec((w, LANES), lambda i, n, s: (n, 0))
    grid_spec = pltpu.PrefetchScalarGridSpec(
        num_scalar_prefetch=1,
        grid=(b, nb),
        in_specs=[pl.BlockSpec((None, w, B_WIDTH), lambda i, n, s: (i, n, 0)),
                  pl.BlockSpec((None, t_ctx, wkv), lambda i, n, s: (i, 0, 0)),
                  kvb(lambda i, n, s: (i, jnp.maximum(n - 1, 0), 0)),
                  kvb(lambda i, n, s: (i, n, 0)),
                  kvb(lambda i, n, s: (i, jnp.minimum(n + 1, nb - 1), 0)),
                  pl.BlockSpec((None, w, B_WIDTH), lambda i, n, s: (i, n, gate_col_block)),
                  tab, tab, tab],
        out_specs=pl.BlockSpec((None, w, B_WIDTH), lambda i, n, s: (i, n, 0)),
        scratch_shapes=[pltpu.VMEM((B_KV_HEADS, t_ctx + 3 * w, B_GROUP * w), F32),
                        pltpu.VMEM((B_KV_HEADS, t_ctx + 3 * w, B_GROUP * w), BF16),
                        pltpu.VMEM((3, w, w), F32)],
    )
    return pl.pallas_call(
        functools.partial(_attn_kernel, n_ctx_blk=n_ctx_blk, n_lat_blk=n_lat_blk),
        grid_spec=grid_spec,
        out_shape=jax.ShapeDtypeStruct((b, t, B_WIDTH), BF16),
        compiler_params=_params("parallel", "parallel"),
        name="window_attn",
    )(sink, p_b, kvr, kvr, kvr, kvr, p_b, *q_tabs)


GLA_DIAG = 4
GLA_LEVELS = (4, 8, 16, 32)
GLA_EXP_KINDS = 2 + len(GLA_LEVELS)


def _gla_region_ids(rev):
    c = C_CHUNK
    ri = lax.broadcasted_iota(jnp.int32, (c, c), 0)
    ci = lax.broadcasted_iota(jnp.int32, (c, c), 1)
    reg = jnp.full((c, c), len(GLA_LEVELS), jnp.int32)
    for li in range(len(GLA_LEVELS) - 1, -1, -1):
        m = GLA_LEVELS[li]
        reg = jnp.where(ri // m == ci // m, li, reg)
    seen = (ci >= ri) if rev else (ci <= ri)
    return jnp.where(seen, reg, -1)


def _gla_exponent_matrix(rev):
    c = C_CHUNK
    t = lax.broadcasted_iota(jnp.int32, (c, c), 0)
    s = lax.broadcasted_iota(jnp.int32, (c, c), 1)
    upto = (s >= t) if rev else (s <= t)
    after = jnp.logical_not(upto)
    as_f32 = lambda cond: jnp.where(cond, 1.0, 0.0).astype(F32)
    mats = [as_f32(upto), as_f32(after)]
    for m in GLA_LEVELS:
        later = ((t // m) % 2 == 0) if rev else ((t // m) % 2 == 1)
        pick = jnp.logical_or(jnp.logical_and(later, upto), jnp.logical_and(jnp.logical_not(later), after))
        mats.append(as_f32(jnp.logical_and(t // m == s // m, pick)))
    mat = jnp.concatenate(mats, axis=0).astype(BF16)
    return jnp.concatenate([mat, mat], axis=1)


def _grp(x, j):
    return x[j * SUBLANES:(j + 1) * SUBLANES, :]


def _gla_stage_a1(z, qr, lb, emat, rev, lane):
    c = C_CHUNK
    ng = c // SUBLANES

    half = 0.5 * (1.0 - lb)
    ht = half * jnp.tanh(0.5 * z)
    f = (lb + half) + ht
    kk = half - ht
    ft = jnp.maximum(f, F_MIN)
    q = _silu(qr)

    g2 = jnp.log2(ft)
    g_hi = g2.astype(BF16)
    g_lo = (g2 - g_hi.astype(F32)).astype(BF16)
    expo = jnp.dot(emat, jnp.concatenate([g_hi, g_lo], axis=0), preferred_element_type=F32)

    dparts = []
    for j in range(ng):
        fg, qg, kg = _grp(ft, j), _grp(q, j), _grp(kk, j)
        w_ = kg
        acc = jnp.where(lane == 0, jnp.sum(qg * kg, axis=-1, keepdims=True), 0.0)
        for dlt in range(1, GLA_DIAG):
            w_ = fg * pltpu.roll(w_, (SUBLANES - 1) if rev else 1, 0)
            red = jnp.sum(qg * w_, axis=-1, keepdims=True)
            acc = jnp.where(lane == (dlt if rev else LANES - dlt), red, acc)
        dparts.append(acc)
    diag = pltpu.roll(jnp.concatenate(dparts, axis=0), 0, 1, stride=1, stride_axis=0)[:, :c]
    return q, kk, expo, diag


def _gla_stage_a2(q, kk, expo, diag, v, rev, reg, row8):
    c = C_CHUNK
    ng = c // SUBLANES
    dec = jnp.exp2(expo)
    last = 0 if rev else c - 1
    total = dec[last:last + 1, :]
    qd = (q * dec[0:c]).astype(BF16)
    kd = (kk * dec[c:2 * c]).astype(BF16)
    grp = _grp

    zero = jnp.zeros((SUBLANES, LANES), F32)
    levels = []
    for li, m in enumerate(GLA_LEVELS):
        fac = dec[(2 + li) * c:(3 + li) * c]
        qparts, kparts = [], []
        for j in range(ng):
            if m >= SUBLANES:
                later = ((j * SUBLANES // m) % 2 == 1) != rev
                prod = (grp(q, j) if later else grp(kk, j)) * grp(fac, j)
                qparts.append(prod if later else zero)
                kparts.append(zero if later else prod)
            else:
                later = ((row8 // m) % 2 == 0) if rev else ((row8 // m) % 2 == 1)
                prod = jnp.where(later, grp(q, j), grp(kk, j)) * grp(fac, j)
                qparts.append(jnp.where(later, prod, 0.0))
                kparts.append(jnp.where(later, 0.0, prod))
        levels.append(lax.dot_general(jnp.concatenate(qparts, axis=0).astype(BF16),
                                      jnp.concatenate(kparts, axis=0).astype(BF16), _NT,
                                      preferred_element_type=F32))

    attn = jnp.zeros((c, c), F32)
    for li in range(len(GLA_LEVELS) - 1, -1, -1):
        attn = jnp.where(reg == li + 1, levels[li], attn)
    attn = jnp.where(reg == 0, diag, attn)
    return attn.astype(BF16), qd, kd, v.astype(BF16), total


def _gla_stage_b(attn, qd, kd, vb, total, st):
    o = jnp.dot(attn, vb, preferred_element_type=F32)
    o = o + lax.dot_general(qd, st.astype(BF16), _NT, preferred_element_type=F32)
    st_new = total * st + lax.dot_general(vb, kd, _TN, preferred_element_type=F32)
    return o, st_new


def _gla_kernel(zf_ref, zb_ref, v_ref, q_ref, og_ref, lb_ref, gw_ref, y_ref, of_scr, ob_scr, st_scr,
                emat_scr, attn_c, qd_c, kd_c, vb_c, tot_c, q_c, k_c, expo_c, diag_c,
                *, n_ctx, n_lat, heads):
    c = C_CHUNK
    row8 = lax.broadcasted_iota(jnp.int32, (SUBLANES, LANES), 0)
    lane = lax.broadcasted_iota(jnp.int32, (SUBLANES, LANES), 1)
    regs = (_gla_region_ids(False), _gla_region_ids(True))
    n_all = n_ctx + n_lat
    st_scr[...] = jnp.zeros_like(st_scr)
    emat_scr[0] = _gla_exponent_matrix(False)
    emat_scr[1] = _gla_exponent_matrix(True)
    streams = [(hh, rev) for hh in range(heads) for rev in (False, True)]

    def rows_of(i, rev):
        if isinstance(i, int):
            cidx = ((n_ctx - 1 - i) if i < n_ctx else (2 * n_ctx + n_lat - 1 - i)) if rev else i
            return pl.ds(cidx * c, c)
        cidx = jnp.where(i < n_ctx, n_ctx - 1 - i, 2 * n_ctx + n_lat - 1 - i) if rev else i
        return pl.ds(pl.multiple_of(cidx * c, c), c)

    def stage_a1(i):
        for k, (hh, rev) in enumerate(streams):
            ls = slice(hh * LANES, (hh + 1) * LANES)
            rows = rows_of(i, rev)
            z_ref = zb_ref if rev else zf_ref
            q, kk, expo, diag = _gla_stage_a1(z_ref[rows, ls], q_ref[rows, ls], lb_ref[:, ls],
                                              emat_scr[int(rev)], rev, lane)
            q_c[k] = q
            k_c[k] = kk
            expo_c[k] = expo
            diag_c[k] = diag

    def stage_a2(i):
        for k, (hh, rev) in enumerate(streams):
            ls = slice(hh * LANES, (hh + 1) * LANES)
            attn, qd, kd, vb, total = _gla_stage_a2(q_c[k], k_c[k], expo_c[k], diag_c[k],
                                                    v_ref[rows_of(i, rev), ls], rev, regs[int(rev)], row8)
            attn_c[k] = attn
            qd_c[k] = qd
            kd_c[k] = kd
            vb_c[k] = vb
            tot_c[k] = jnp.broadcast_to(total, (SUBLANES, LANES))

    def stage_b(i):
        for k, (hh, rev) in enumerate(streams):
            ls = slice(hh * LANES, (hh + 1) * LANES)
            o, st = _gla_stage_b(attn_c[k], qd_c[k], kd_c[k], vb_c[k], tot_c[k, 0:1, :], st_scr[k])
            (ob_scr if rev else of_scr)[rows_of(i, rev), ls] = o
            st_scr[k] = st

    stage_a1(0)
    stage_a2(0)
    stage_a1(1)

    def step(i):
        stage_b(i - 2)
        stage_a2(i - 1)
        stage_a1(i)

    def body(j, carry):
        step(2 + 2 * j)
        step(3 + 2 * j)
        return carry

    lax.fori_loop(0, (n_all - 2) // 2, body, 0)
    if (n_all - 2) % 2:
        step(n_all - 1)
    stage_b(n_all - 2)
    stage_a2(n_all - 1)
    stage_b(n_all - 1)

    gw = gw_ref[...]
    piece = 256

    def fin(i, carry):
        rows = pl.ds(pl.multiple_of(i * piece, piece), piece)
        for hh in range(heads):
            ls = slice(hh * LANES, (hh + 1) * LANES)
            o = of_scr[rows, ls] + ob_scr[rows, ls]
            y = o * lax.rsqrt(jnp.mean(o * o, axis=-1, keepdims=True) + RMS_EPS) * gw
            y_ref[rows, ls] = (y * _silu(og_ref[rows, ls])).astype(BF16)
        return carry

    lax.fori_loop(0, (n_all * c) // piece, fin, 0)


def _gla(p, lb, gw, t_ctx):
    b, t, _ = p.shape
    hps = GLA_HEADS_PER_STEP
    wb = hps * LANES
    nblk = C_HEADS // hps
    col = lambda k: pl.BlockSpec((None, t, wb), lambda i, h: (i, 0, k * nblk + h))
    return pl.pallas_call(
        functools.partial(_gla_kernel, n_ctx=t_ctx // C_CHUNK, n_lat=(t - t_ctx) // C_CHUNK, heads=hps),
        grid=(b, nblk),
        in_specs=[col(0), col(1), col(2), col(3), col(4),
                  pl.BlockSpec((None, 1, wb), lambda i, h: (h, 0, 0)),
                  pl.BlockSpec((1, LANES), lambda i, h: (0, 0))],
        out_specs=pl.BlockSpec((None, t, wb), lambda i, h: (i, 0, h)),
        out_shape=jax.ShapeDtypeStruct((b, t, C_HEADS * LANES), BF16),
        scratch_shapes=[pltpu.VMEM((t, wb), F32), pltpu.VMEM((t, wb), F32),
                        pltpu.VMEM((2 * hps, C_EXPAND, LANES), F32),
                        pltpu.VMEM((2, GLA_EXP_KINDS * C_CHUNK, 2 * C_CHUNK), BF16),
                        pltpu.VMEM((2 * hps, C_CHUNK, C_CHUNK), BF16),
                        pltpu.VMEM((2 * hps, C_CHUNK, LANES), BF16),
                        pltpu.VMEM((2 * hps, C_CHUNK, LANES), BF16),
                        pltpu.VMEM((2 * hps, C_CHUNK, LANES), BF16),
                        pltpu.VMEM((2 * hps, SUBLANES, LANES), F32),
                        pltpu.VMEM((2 * hps, C_CHUNK, LANES), F32),
                        pltpu.VMEM((2 * hps, C_CHUNK, LANES), F32),
                        pltpu.VMEM((2 * hps, GLA_EXP_KINDS * C_CHUNK, LANES), F32),
                        pltpu.VMEM((2 * hps, C_CHUNK, C_CHUNK), F32)],
        compiler_params=_params("parallel", "parallel"),
        name="hgrn2_gla",
    )(p, p, p, p, p, lb.reshape(nblk, 1, wb), gw.reshape(1, LANES))


def _outproj_kernel(*refs, n_in, final):
    y_refs = refs[:n_in]
    w_ref, h_ref, gt_ref = refs[n_in:n_in + 3]
    if final:
        fw_ref, o_ref = refs[n_in + 3:]
    else:
        (o_ref,) = refs[n_in + 3:]
    acc = None
    k0 = 0
    for y_ref in y_refs:
        kw = y_ref.shape[-1]
        part = jnp.dot(y_ref[...], w_ref[k0:k0 + kw, :], preferred_element_type=F32)
        acc = part if acc is None else acc + part
        k0 += kw
    tm, d = acc.shape
    hn = h_ref[...] + (acc.reshape(tm // SUBLANES, SUBLANES, d) * gt_ref[...][None]).reshape(tm, d)
    if final:
        hn = hn * lax.rsqrt(jnp.mean(hn * hn, axis=-1, keepdims=True) + RMS_EPS) * fw_ref[...]
    o_ref[...] = hn


def _outproj(ys, w, h, gate, n_ctx_tiles, final_w=None):
    b, t, d = h.shape
    tm = ROW_TILE
    final = final_w is not None
    off = n_ctx_tiles if final else 0
    nt = t // tm - off

    def grp(i, j):
        return jnp.where(j + off < n_ctx_tiles, b, i)

    in_specs = []
    for idx, y in enumerate(ys):
        kw = y.shape[-1]
        if y.ndim == 2:
            ys[idx] = y.reshape(t, b * kw)
            in_specs.append(pl.BlockSpec((tm, kw), lambda i, j: (j + off, i)))
        else:
            in_specs.append(pl.BlockSpec((None, tm, kw), lambda i, j: (i, j + off, 0)))
    in_specs += [pl.BlockSpec(w.shape, lambda i, j: (0, 0)),
                 pl.BlockSpec((None, tm, d), lambda i, j: (i, j + off, 0)),
                 pl.BlockSpec((None, SUBLANES, d), lambda i, j: (grp(i, j), 0, 0))]
    args = list(ys) + [w, h, gate]
    if final:
        in_specs.append(pl.BlockSpec((1, d), lambda i, j: (0, 0)))
        args.append(final_w.reshape(1, d))
    return pl.pallas_call(
        functools.partial(_outproj_kernel, n_in=len(ys), final=final),
        grid=(b, nt),
        in_specs=in_specs,
        out_specs=pl.BlockSpec((None, tm, d), lambda i, j: (i, j, 0)),
        out_shape=jax.ShapeDtypeStruct((b, nt * tm, d), F32),
        compiler_params=_params("parallel", "parallel"),
        name="outproj",
    )(*args)


def _rope_tables(s_lat, t_ctx):
    pos = np.arange(s_lat)
    row = (pos // GRID_W).astype(np.float32)
    colp = (pos % GRID_W).astype(np.float32)
    axis_dim = B_HEAD_DIM // 2
    inv = jnp.asarray(ROPE_BASE, F32) ** (-jnp.arange(0, axis_dim, 2, dtype=F32) / axis_dim)
    ang = jnp.concatenate([jnp.asarray(row)[:, None] * inv, jnp.asarray(colp)[:, None] * inv], axis=-1)
    cos, sin = jnp.cos(ang), jnp.sin(ang)
    zer = jnp.zeros_like(sin)
    pairs = lambda even, odd: jnp.stack([even, odd], axis=-1).reshape(s_lat, B_HEAD_DIM)
    cos64 = pairs(cos, cos)
    sa64 = pairs(-sin, zer)
    sb64 = pairs(zer, sin)
    one64 = jnp.ones_like(cos64)
    zer64 = jnp.zeros_like(cos64)

    def full(lat128, ident):
        return jnp.concatenate([jnp.broadcast_to(ident, (t_ctx, LANES)), lat128], axis=0)

    ident_c = jnp.ones((1, LANES), F32)
    ident_s = jnp.zeros((1, LANES), F32)
    q_tabs = (full(jnp.concatenate([cos64, cos64], -1), ident_c),
              full(jnp.concatenate([sa64, sa64], -1), ident_s),
              full(jnp.concatenate([sb64, sb64], -1), ident_s))
    kv_tabs = (full(jnp.concatenate([cos64, one64], -1), ident_c),
               full(jnp.concatenate([sa64, zer64], -1), ident_s),
               full(jnp.concatenate([sb64, zer64], -1), ident_s))
    return q_tabs, kv_tabs


def _even_weights(w_in):
    dh = B_HEAD_DIM
    n_state = A_WIDTH + 2 * B_KV_WIDTH
    q0 = n_state + A_WIDTH
    k0 = A_WIDTH
    v0 = A_WIDTH + B_KV_WIDTH
    parts = [w_in[:, q0:q0 + 2 * B_WIDTH]]
    for j in range(B_KV_HEADS):
        parts += [w_in[:, k0 + j * dh:k0 + (j + 1) * dh], w_in[:, v0 + j * dh:v0 + (j + 1) * dh]]
    parts += [w_in[:, :A_WIDTH], w_in[:, n_state:n_state + A_WIDTH]]
    return jnp.concatenate(parts, axis=1).astype(BF16)


def _groups(vec_b, vec_c):
    allv = jnp.concatenate([vec_b, vec_c[None, :]], axis=0)
    return jnp.broadcast_to(allv[:, None, :], (allv.shape[0], SUBLANES, allv.shape[1]))


def kernel(x, c, ctx, c_ctx, ada_w, ada_b, norm_w, ev_w_in, ev_conv_w, ev_conv_b, ev_rg_wx, ev_rg_bx,
           ev_rg_wa, ev_rg_ba, ev_rg_lambda, ev_sink, ev_w_out, od_w_in, od_lb_raw, od_gnorm_w, od_w_out,
           final_norm_w):
    b, s_lat, d = x.shape
    t_ctx = ctx.shape[1]
    t_all = t_ctx + s_lat
    assert b == SUBLANES and d == D_MODEL
    assert t_ctx % ROW_TILE == 0 and s_lat % ROW_TILE == 0
    n_ctx_tiles = t_ctx // ROW_TILE

    sc_rows = jnp.zeros((2 * SUBLANES, d), F32)
    sc_rows = sc_rows.at[:b].set(jax.nn.silu(c)).at[b].set(jax.nn.silu(c_ctx))
    mod = _ada_mod(sc_rows, ada_w, ada_b)

    lb_p = jax.nn.softmax(od_lb_raw.astype(F32), axis=0)
    lower_bounds = jnp.cumsum(lb_p, axis=0) - lb_p[0]
    q_tabs, kv_tabs = _rope_tables(s_lat, t_ctx)

    h = jnp.concatenate([ctx, x], axis=1)
    out = None
    for layer in range(DEPTH):
        last = layer == DEPTH - 1
        j = layer // 2
        ml = mod[layer]
        shift = _groups(ml[:b, :d], ml[b, :d])
        scale = _groups(ml[:b, d:2 * d], ml[b, d:2 * d])
        gate = _groups(ml[:b, 2 * d:], ml[b, 2 * d:])
        if layer % 2 == 0:
            p_b, p_a = _inproj(h, shift, scale, norm_w[layer], _even_weights(ev_w_in[j]), n_ctx_tiles,
                               n_time_major=2 * A_WIDTH)
            la = -A_C * jax.nn.softplus(-ev_rg_lambda[j].astype(F32))
            w_cat = jnp.concatenate([ev_rg_wx[j], ev_rg_wa[j]], axis=-1).astype(BF16)
            rg = lambda dr, yf: _rglru_pass(
                p_a, yf, ev_conv_w[j], ev_conv_b[j].reshape(1, -1), w_cat[dr],
                ev_rg_bx[j, dr].reshape(1, -1), ev_rg_ba[j, dr].reshape(1, -1), la[dr].reshape(1, -1),
                t_ctx, s_lat, rev=bool(dr))
            ya = rg(1, rg(0, None))
            kvr = _rope_kv(p_b, kv_tabs, col_block=2 * B_WIDTH // (2 * B_KV_WIDTH))
            yb = _attention(p_b, kvr, ev_sink[j].astype(F32), q_tabs, t_ctx, gate_col_block=1)
            ys, w_out = [ya, yb], ev_w_out[j].astype(BF16)
        else:
            p = _inproj(h, shift, scale, norm_w[layer], od_w_in[j].astype(BF16), n_ctx_tiles)
            ys, w_out = [_gla(p, lower_bounds[j], od_gnorm_w[j], t_ctx)], od_w_out[j].astype(BF16)
        if last:
            out = _outproj(ys, w_out, h, gate, n_ctx_tiles, final_w=final_norm_w)
        else:
            h = _outproj(ys, w_out, h, gate, n_ctx_tiles)
    return out
```

```python
import functools

import jax
import jax.numpy as jnp
import numpy as np
from jax import lax
from jax.experimental import pallas as pl
from jax.experimental.pallas import tpu as pltpu

F32 = jnp.float32
BF16 = jnp.bfloat16

D_MODEL = 1024
DEPTH = 4
GRID_W = 64
RMS_EPS = 1e-6
NEG_BIG = -1e30
F_MIN = 1e-30

A_WIDTH = D_MODEL
A_HEADS = 8
A_BLOCK = A_WIDTH // A_HEADS
A_CONV = 4
A_C = 8.0
B_HEADS = 16
B_KV_HEADS = 4
B_HEAD_DIM = 64
B_GROUP = B_HEADS // B_KV_HEADS
B_WIDTH = B_HEADS * B_HEAD_DIM
B_KV_WIDTH = B_KV_HEADS * B_HEAD_DIM
WINDOW = 128
ROPE_BASE = 10000.0
ATTN_SCALE = B_HEAD_DIM ** -0.5
LOG2E = 1.4426950408889634
C_EXPAND = 128
C_HEADS = D_MODEL // C_EXPAND
C_CHUNK = 64

SUBLANES = 8
LANES = 128
ROW_TILE = 256
GLA_HEADS_PER_STEP = 2
VMEM_LIMIT = 56 * 1024 * 1024

_NT = (((1,), (1,)), ((), ()))
_TN = (((0,), (0,)), ((), ()))


def _params(*sem):
    return pltpu.CompilerParams(dimension_semantics=sem, vmem_limit_bytes=VMEM_LIMIT)


def _sigmoid(x):
    return 0.5 * jnp.tanh(0.5 * x) + 0.5


def _silu(x):
    return x * _sigmoid(x)


def _ada_kernel(sc_ref, w_ref, b_ref, o_ref):
    o_ref[...] = jnp.dot(sc_ref[...], w_ref[...], preferred_element_type=F32,
                         precision=lax.Precision.HIGHEST) + b_ref[...]


def _ada_mod(sc, ada_w, ada_b):
    depth, d, n = ada_w.shape
    rows = sc.shape[0]
    tn = 1024
    return pl.pallas_call(
        _ada_kernel,
        grid=(depth, n // tn),
        in_specs=[pl.BlockSpec((rows, d), lambda l, j: (0, 0)),
                  pl.BlockSpec((None, d, tn), lambda l, j: (l, 0, j)),
                  pl.BlockSpec((None, 1, tn), lambda l, j: (l, 0, j))],
        out_specs=pl.BlockSpec((None, rows, tn), lambda l, j: (l, 0, j)),
        out_shape=jax.ShapeDtypeStruct((depth, rows, n), F32),
        compiler_params=_params("parallel", "parallel"),
        name="ada_mod",
    )(sc, ada_w, ada_b.reshape(depth, 1, n))


def _inproj_kernel(h_ref, sh_ref, sc_ref, nw_ref, w_ref, *rest, emit_u, col_chunk):
    if emit_u:
        p_ref, u_ref, u_scr = rest
    else:
        p_ref, u_scr = rest
    x = h_ref[...]
    tm, d = x.shape
    y = x * lax.rsqrt(jnp.mean(x * x, axis=-1, keepdims=True) + RMS_EPS) * nw_ref[...]
    y3 = y.reshape(tm // SUBLANES, SUBLANES, d)
    u = (y3 * (1.0 + sc_ref[...])[None] + sh_ref[...][None]).reshape(tm, d)
    ub = u.astype(BF16)
    u_scr[...] = ub
    if emit_u:
        u_ref[...] = ub
    for c in range(0, p_ref.shape[-1], col_chunk):
        p_ref[:, c:c + col_chunk] = jnp.dot(u_scr[...], w_ref[:, c:c + col_chunk],
                                            preferred_element_type=F32)


def _inproj(h, shift, scale, norm_w, w, n_ctx_tiles, emit_u):
    b, t, d = h.shape
    n = w.shape[1]
    tm = ROW_TILE

    def grp(i, j):
        return jnp.where(j < n_ctx_tiles, b, i)

    out_shape = [jax.ShapeDtypeStruct((b, t, n), F32)]
    out_specs = [pl.BlockSpec((None, tm, n), lambda i, j: (i, j, 0))]
    if emit_u:
        out_shape.append(jax.ShapeDtypeStruct((b, t, d), BF16))
        out_specs.append(pl.BlockSpec((None, tm, d), lambda i, j: (i, j, 0)))
    res = pl.pallas_call(
        functools.partial(_inproj_kernel, emit_u=emit_u, col_chunk=512),
        grid=(b, t // tm),
        in_specs=[pl.BlockSpec((None, tm, d), lambda i, j: (i, j, 0)),
                  pl.BlockSpec((None, SUBLANES, d), lambda i, j: (grp(i, j), 0, 0)),
                  pl.BlockSpec((None, SUBLANES, d), lambda i, j: (grp(i, j), 0, 0)),
                  pl.BlockSpec((1, d), lambda i, j: (0, 0)),
                  pl.BlockSpec(w.shape, lambda i, j: (0, 0))],
        out_specs=out_specs,
        out_shape=out_shape,
        scratch_shapes=[pltpu.VMEM((tm, d), BF16)],
        compiler_params=_params("parallel", "parallel"),
        name="inproj",
    )(h, shift, scale, norm_w.reshape(1, d), w)
    return res if emit_u else res[0]


def _matmul_kernel(u_ref, w_ref, p_ref, *, col_chunk):
    for c in range(0, p_ref.shape[-1], col_chunk):
        p_ref[:, c:c + col_chunk] = jnp.dot(u_ref[...], w_ref[:, c:c + col_chunk],
                                            preferred_element_type=F32)


def _matmul(u, w):
    r, d = u.shape
    n = w.shape[1]
    tm = 512
    return pl.pallas_call(
        functools.partial(_matmul_kernel, col_chunk=512),
        grid=(r // tm,),
        in_specs=[pl.BlockSpec((tm, d), lambda i: (i, 0)),
                  pl.BlockSpec((d, n), lambda i: (0, 0))],
        out_specs=pl.BlockSpec((tm, n), lambda i: (i, 0)),
        out_shape=jax.ShapeDtypeStruct((r, n), F32),
        compiler_params=_params("parallel"),
        name="matmul_tm",
    )(u, w)


def _scan_tile(i, n_ctx, n_lat, rev):
    if not rev:
        return i
    return jnp.where(i < n_ctx, n_ctx - 1 - i, 2 * n_ctx + n_lat - 1 - i)


def _rglru_kernel(*refs, rev, tt, n_ctx, n_lat):
    if rev:
        (xa_ref, xp_ref, xn1_ref, xn2_ref, ga_ref, yf_ref, cw_ref, cb_ref, w_ref, bx_ref, ba_ref,
         la_ref, out_ref, h_scr, xpad_scr, u_scr, a_scr, b_scr) = refs
    else:
        (xa_ref, xp_ref, xn1_ref, xn2_ref, cw_ref, cb_ref, w_ref, bx_ref, ba_ref,
         la_ref, out_ref, h_scr, xpad_scr, u_scr, a_scr, b_scr) = refs
    i = pl.program_id(0)
    tile = _scan_tile(i, n_ctx, n_lat, rev)
    seg_first = jnp.logical_or(tile == 0, tile == n_ctx)
    seg_last = jnp.logical_or(tile == n_ctx - 1, tile == n_ctx + n_lat - 1)
    rows = tt * SUBLANES

    @pl.when(i == 0)
    def _():
        h_scr[...] = jnp.zeros_like(h_scr)

    xpad_scr[0:8, :] = jnp.where(seg_first, 0.0, xp_ref[...])
    xpad_scr[8:8 + rows, :] = xa_ref[...]
    xpad_scr[8 + rows:16 + rows, :] = jnp.where(seg_last, 0.0, xn1_ref[...])
    xpad_scr[16 + rows:24 + rows, :] = jnp.where(seg_last, 0.0, xn2_ref[...])
    u = cb_ref[...] + xpad_scr[0:rows, :] * cw_ref[0:1, :]
    for k in range(1, A_CONV):
        u = u + xpad_scr[8 * k:8 * k + rows, :] * cw_ref[k:k + 1, :]
    u_scr[...] = u

    for hd in range(A_HEADS):
        sl = slice(hd * A_BLOCK, (hd + 1) * A_BLOCK)
        uh = u_scr[:, sl]
        g = jnp.dot(uh.astype(BF16), w_ref[hd], preferred_element_type=F32)
        gate_x = _sigmoid(g[:, :A_BLOCK] + bx_ref[:, sl])
        gate_a = _sigmoid(g[:, A_BLOCK:] + ba_ref[:, sl])
        log_a = gate_a * la_ref[:, sl]
        a = jnp.exp(log_a)
        a_scr[:, sl] = a
        b_scr[:, sl] = jnp.sqrt(-jnp.tanh(log_a) * (1.0 + a * a)) * gate_x * uh

    def step(s, h):
        t = (tt - 1 - s) if rev else s
        r0 = pl.multiple_of(t * SUBLANES, SUBLANES)
        h = a_scr[pl.ds(r0, SUBLANES), :] * h + b_scr[pl.ds(r0, SUBLANES), :]
        b_scr[pl.ds(r0, SUBLANES), :] = h
        return h

    h_scr[...] = lax.fori_loop(0, tt, step, h_scr[...], unroll=8)
    if rev:
        out_ref[...] = ((yf_ref[...] + b_scr[...]) * _silu(ga_ref[...])).astype(BF16)
    else:
        out_ref[...] = b_scr[...]


def _rglru_pass(p_a, yf, conv_w, conv_b, w_cat, bx, ba, la, t_ctx, t_lat, rev):
    w = A_WIDTH
    tt = 64
    rows = tt * SUBLANES
    n_ctx, n_lat = t_ctx // tt, t_lat // tt
    t_all = t_ctx + t_lat
    tile = functools.partial(_scan_tile, n_ctx=n_ctx, n_lat=n_lat, rev=rev)
    const = lambda i: (0, 0)
    in_specs = [pl.BlockSpec((rows, w), lambda i: (tile(i), 0)),
                pl.BlockSpec((SUBLANES, w), lambda i: (jnp.maximum(tile(i) * tt - 1, 0), 0)),
                pl.BlockSpec((SUBLANES, w), lambda i: (jnp.minimum(tile(i) * tt + tt, t_all - 1), 0)),
                pl.BlockSpec((SUBLANES, w), lambda i: (jnp.minimum(tile(i) * tt + tt + 1, t_all - 1), 0))]
    args = [p_a, p_a, p_a, p_a]
    if rev:
        in_specs += [pl.BlockSpec((rows, w), lambda i: (tile(i), 1)),
                     pl.BlockSpec((rows, w), lambda i: (tile(i), 0))]
        args += [p_a, yf]
    in_specs += [pl.BlockSpec((A_CONV, w), const), pl.BlockSpec((1, w), const),
                 pl.BlockSpec((A_HEADS, A_BLOCK, 2 * A_BLOCK), lambda i: (0, 0, 0)),
                 pl.BlockSpec((1, w), const), pl.BlockSpec((1, w), const), pl.BlockSpec((1, w), const)]
    args += [conv_w, conv_b, w_cat, bx, ba, la]
    return pl.pallas_call(
        functools.partial(_rglru_kernel, rev=rev, tt=tt, n_ctx=n_ctx, n_lat=n_lat),
        grid=(n_ctx + n_lat,),
        in_specs=in_specs,
        out_specs=pl.BlockSpec((rows, w), lambda i: (tile(i), 0)),
        out_shape=jax.ShapeDtypeStruct((t_all * SUBLANES, w), BF16 if rev else F32),
        scratch_shapes=[pltpu.VMEM((SUBLANES, w), F32),
                        pltpu.VMEM((rows + 3 * SUBLANES, w), F32),
                        pltpu.VMEM((rows, w), F32),
                        pltpu.VMEM((rows, w), F32),
                        pltpu.VMEM((rows, w), F32)],
        compiler_params=_params("arbitrary"),
        name="rglru_bwd" if rev else "rglru_fwd",
    )(*args)


def _rope(x, cos, sin_a, sin_b):
    return x * cos + pltpu.roll(x, LANES - 1, 1) * sin_a + pltpu.roll(x, 1, 1) * sin_b


def _rope_kv_kernel(kv_ref, cos_ref, sa_ref, sb_ref, o_ref, vt_ref):
    cos, sa, sb = cos_ref[...], sa_ref[...], sb_ref[...]
    dh = B_HEAD_DIM
    for g in range(B_KV_HEADS):
        sl = slice(g * LANES, (g + 1) * LANES)
        x = _rope(kv_ref[:, sl], cos, sa, sb)
        o_ref[:, sl] = x.astype(BF16)
        vt_ref[g * dh:(g + 1) * dh, :] = x.T[dh:, :].astype(BF16)


def _rope_kv(p_b, tabs, col_block):
    b, t, _ = p_b.shape
    tm = 3 * ROW_TILE if t % (3 * ROW_TILE) == 0 else ROW_TILE
    wkv = 2 * B_KV_WIDTH
    tab = pl.BlockSpec((tm, LANES), lambda i, j: (j, 0))
    return pl.pallas_call(
        _rope_kv_kernel,
        grid=(b, t // tm),
        in_specs=[pl.BlockSpec((None, tm, wkv), lambda i, j: (i, j, col_block)), tab, tab, tab],
        out_specs=[pl.BlockSpec((None, tm, wkv), lambda i, j: (i, j, 0)),
                   pl.BlockSpec((None, B_KV_WIDTH, tm), lambda i, j: (i, 0, j))],
        out_shape=[jax.ShapeDtypeStruct((b, t, wkv), BF16),
                   jax.ShapeDtypeStruct((b, B_KV_WIDTH, t), BF16)],
        compiler_params=_params("parallel", "parallel"),
        name="rope_kv",
    )(p_b, *tabs)


def _attn_kernel(sink_ref, q_ref, kc_ref, kp_ref, ko_ref, kn_ref, vc_ref, vp_ref, vo_ref, vn_ref, gb_ref,
                 cos_ref, sa_ref, sb_ref, o_ref, s_scr, p_scr, bias_scr, *, n_ctx_blk, n_lat_blk):
    n = pl.program_id(1)
    is_lat = n >= n_ctx_blk

    @pl.when(is_lat)
    def _():
        _attend(True, n - n_ctx_blk, n_lat_blk, sink_ref, q_ref, (kc_ref, kp_ref, ko_ref, kn_ref),
                (vc_ref, vp_ref, vo_ref, vn_ref), gb_ref, cos_ref, sa_ref, sb_ref, o_ref, s_scr, p_scr, bias_scr)

    @pl.when(jnp.logical_not(is_lat))
    def _():
        _attend(False, None, None, sink_ref, q_ref, (kc_ref,), (vc_ref,), gb_ref, cos_ref, sa_ref, sb_ref,
                o_ref, s_scr, p_scr, bias_scr)


def _attend(window, m, n_lat_blk, sink_ref, q_ref, k_refs, v_refs, gb_ref, cos_ref, sa_ref, sb_ref,
            o_ref, s_scr, p_scr, bias_scr):
    w = WINDOW
    dh = B_HEAD_DIM
    lc = k_refs[0].shape[0]

    cos, sa, sb = cos_ref[...], sa_ref[...], sb_ref[...]
    low = lax.broadcasted_iota(jnp.int32, (w, LANES), 1) < dh
    segs = [(0, lc, None)]
    if window:
        kj = lax.broadcasted_iota(jnp.int32, (w, w), 0)
        qi = lax.broadcasted_iota(jnp.int32, (w, w), 1)
        neg = jnp.full((w, w), NEG_BIG, F32)
        zer = jnp.zeros((w, w), F32)
        bias_scr[0] = jnp.where(jnp.logical_and(m >= 1, kj >= qi), zer, neg)
        bias_scr[1] = jnp.where(jnp.logical_and(m <= n_lat_blk - 2, kj <= qi), zer, neg)
        segs += [(lc, w, 0), (lc + w, w, None), (lc + 2 * w, w, 1)]
    nk = segs[-1][0] + segs[-1][1]
    piece = 32
    pieces = [(r0, bi, p0) for r0, nr, bi in segs for p0 in range(0, nr, piece)]

    def scores(g, cb, r0, bi, p0):
        x = s_scr[g, r0 + p0:r0 + p0 + piece, cb]
        return x if bi is None else x + bias_scr[bi, p0:p0 + piece, :]

    def tree(xs, op):
        while len(xs) > 1:
            xs = [op(xs[i], xs[i + 1]) for i in range(0, len(xs) - 1, 2)] + ([xs[-1]] if len(xs) % 2 else [])
        return xs[0]

    def fold8(x, op):
        return op(x.reshape(x.shape[0] // SUBLANES, SUBLANES, x.shape[1]), axis=0)

    for g in range(B_KV_HEADS):
        parts = []
        for c in range(2):
            col0 = g * B_GROUP * dh + c * LANES
            qc = _rope(q_ref[:, col0:col0 + LANES], cos, sa, sb) * (ATTN_SCALE * LOG2E)
            parts.append(jnp.where(low, qc, 0.0).astype(BF16))
            parts.append(jnp.where(low, pltpu.roll(qc, dh, 1), 0.0).astype(BF16))
        qs = jnp.concatenate(parts, axis=0)
        ks = slice(g * LANES, (g + 1) * LANES)
        kv_all = jnp.concatenate([r[:, ks] for r in k_refs], axis=0)
        s_scr[g, 0:nk, :] = lax.dot_general(kv_all, qs, _NT, preferred_element_type=F32)

        rdens = []
        for h in range(B_GROUP):
            cb = slice(h * w, (h + 1) * w)
            snk = sink_ref[g * B_GROUP + h] * LOG2E
            mx8 = tree([fold8(scores(g, cb, r0, bi, p0), jnp.max) for r0, bi, p0 in pieces], jnp.maximum)
            mx = jnp.maximum(jnp.max(mx8, axis=0, keepdims=True), snk)
            sums = []
            for r0, bi, p0 in pieces:
                p = jnp.exp2(scores(g, cb, r0, bi, p0) - mx)
                sums.append(fold8(p, jnp.sum))
                p_scr[g, r0 + p0:r0 + p0 + piece, cb] = p.astype(BF16)
            den8 = tree(sums, jnp.add)
            rdens.append(1.0 / (jnp.sum(den8, axis=0, keepdims=True) + jnp.exp2(snk - mx)))

        vt_all = jnp.concatenate([r[g * dh:(g + 1) * dh, :] for r in v_refs], axis=1)
        ov = jnp.dot(vt_all, p_scr[g, 0:nk, :], preferred_element_type=F32) * jnp.concatenate(rdens, axis=1)
        for c in range(2):
            pair = jnp.concatenate([ov[:, (2 * c) * w:(2 * c + 1) * w], ov[:, (2 * c + 1) * w:(2 * c + 2) * w]],
                                   axis=0)
            col0 = g * B_GROUP * dh + c * LANES
            y = pair.T * _silu(gb_ref[:, col0:col0 + LANES])
            o_ref[:, col0:col0 + LANES] = y.astype(BF16)


def _attention(p_b, kvr, vt, sink, q_tabs, t_ctx, gate_col_block):
    b, t, _ = p_b.shape
    w = WINDOW
    n_ctx_blk, n_lat_blk = t_ctx // w, (t - t_ctx) // w
    nb = t // w
    wkv = 2 * B_KV_WIDTH
    prev = lambda n: jnp.maximum(n - 1, 0)
    nxt = lambda n: jnp.minimum(n + 1, nb - 1)
    kvb = lambda f: pl.BlockSpec((None, w, wkv), lambda i, n, s: (i, f(n), 0))
    vtb = lambda f: pl.BlockSpec((None, B_KV_WIDTH, w), lambda i, n, s: (i, 0, f(n)))
    tab = pl.BlockSpec((w, LANES), lambda i, n, s: (n, 0))
    grid_spec = pltpu.PrefetchScalarGridSpec(
        num_scalar_prefetch=1,
        grid=(b, nb),
        in_specs=[pl.BlockSpec((None, w, B_WIDTH), lambda i, n, s: (i, n, 0)),
                  pl.BlockSpec((None, t_ctx, wkv), lambda i, n, s: (i, 0, 0)),
                  kvb(prev), kvb(lambda n: n), kvb(nxt),
                  pl.BlockSpec((None, B_KV_WIDTH, t_ctx), lambda i, n, s: (i, 0, 0)),
                  vtb(prev), vtb(lambda n: n), vtb(nxt),
                  pl.BlockSpec((None, w, B_WIDTH), lambda i, n, s: (i, n, gate_col_block)),
                  tab, tab, tab],
        out_specs=pl.BlockSpec((None, w, B_WIDTH), lambda i, n, s: (i, n, 0)),
        scratch_shapes=[pltpu.VMEM((B_KV_HEADS, t_ctx + 3 * w, B_GROUP * w), F32),
                        pltpu.VMEM((B_KV_HEADS, t_ctx + 3 * w, B_GROUP * w), BF16),
                        pltpu.VMEM((2, w, w), F32)],
    )
    return pl.pallas_call(
        functools.partial(_attn_kernel, n_ctx_blk=n_ctx_blk, n_lat_blk=n_lat_blk),
        grid_spec=grid_spec,
        out_shape=jax.ShapeDtypeStruct((b, t, B_WIDTH), BF16),
        compiler_params=_params("parallel", "parallel"),
        name="window_attn",
    )(sink, p_b, kvr, kvr, kvr, kvr, vt, vt, vt, vt, p_b, *q_tabs)


GLA_DIAG = 4
GLA_LEVELS = (4, 8, 16, 32)
GLA_EXP_KINDS = 2 + len(GLA_LEVELS)


def _gla_region_ids(rev):
    c = C_CHUNK
    ri = lax.broadcasted_iota(jnp.int32, (c, c), 0)
    ci = lax.broadcasted_iota(jnp.int32, (c, c), 1)
    reg = jnp.full((c, c), len(GLA_LEVELS), jnp.int32)
    for li in range(len(GLA_LEVELS) - 1, -1, -1):
        m = GLA_LEVELS[li]
        reg = jnp.where(ri // m == ci // m, li, reg)
    seen = (ci >= ri) if rev else (ci <= ri)
    return jnp.where(seen, reg, -1)


def _gla_exponent_matrix(rev):
    c = C_CHUNK
    t = lax.broadcasted_iota(jnp.int32, (c, c), 0)
    s = lax.broadcasted_iota(jnp.int32, (c, c), 1)
    upto = (s >= t) if rev else (s <= t)
    after = jnp.logical_not(upto)
    as_f32 = lambda cond: jnp.where(cond, 1.0, 0.0).astype(F32)
    mats = [as_f32(upto), as_f32(after)]
    for m in GLA_LEVELS:
        later = ((t // m) % 2 == 0) if rev else ((t // m) % 2 == 1)
        pick = jnp.logical_or(jnp.logical_and(later, upto), jnp.logical_and(jnp.logical_not(later), after))
        mats.append(as_f32(jnp.logical_and(t // m == s // m, pick)))
    mat = jnp.concatenate(mats, axis=0).astype(BF16)
    return jnp.concatenate([mat, mat], axis=1)


def _grp(x, j):
    return x[j * SUBLANES:(j + 1) * SUBLANES, :]


def _gla_stage_a1(z, qr, lb, emat, rev, lane):
    c = C_CHUNK
    ng = c // SUBLANES

    half = 0.5 * (1.0 - lb)
    ht = half * jnp.tanh(0.5 * z)
    f = (lb + half) + ht
    kk = half - ht
    ft = jnp.maximum(f, F_MIN)
    q = _silu(qr)

    g2 = jnp.log2(ft)
    g_hi = g2.astype(BF16)
    g_lo = (g2 - g_hi.astype(F32)).astype(BF16)
    expo = jnp.dot(emat, jnp.concatenate([g_hi, g_lo], axis=0), preferred_element_type=F32)

    dparts = []
    for j in range(ng):
        fg, qg, kg = _grp(ft, j), _grp(q, j), _grp(kk, j)
        w_ = kg
        acc = jnp.where(lane == 0, jnp.sum(qg * kg, axis=-1, keepdims=True), 0.0)
        for dlt in range(1, GLA_DIAG):
            w_ = fg * pltpu.roll(w_, (SUBLANES - 1) if rev else 1, 0)
            red = jnp.sum(qg * w_, axis=-1, keepdims=True)
            acc = jnp.where(lane == (dlt if rev else LANES - dlt), red, acc)
        dparts.append(acc)
    diag = pltpu.roll(jnp.concatenate(dparts, axis=0), 0, 1, stride=1, stride_axis=0)[:, :c]
    return q, kk, expo, diag


def _gla_stage_a2(q, kk, expo, diag, v, rev, reg, row8):
    c = C_CHUNK
    ng = c // SUBLANES
    dec = jnp.exp2(expo)
    last = 0 if rev else c - 1
    total = dec[last:last + 1, :]
    qd = (q * dec[0:c]).astype(BF16)
    kd = (kk * dec[c:2 * c]).astype(BF16)
    grp = _grp

    zero = jnp.zeros((SUBLANES, LANES), F32)
    levels = []
    for li, m in enumerate(GLA_LEVELS):
        fac = dec[(2 + li) * c:(3 + li) * c]
        qparts, kparts = [], []
        for j in range(ng):
            if m >= SUBLANES:
                later = ((j * SUBLANES // m) % 2 == 1) != rev
                prod = (grp(q, j) if later else grp(kk, j)) * grp(fac, j)
                qparts.append(prod if later else zero)
                kparts.append(zero if later else prod)
            else:
                later = ((row8 // m) % 2 == 0) if rev else ((row8 // m) % 2 == 1)
                prod = jnp.where(later, grp(q, j), grp(kk, j)) * grp(fac, j)
                qparts.append(jnp.where(later, prod, 0.0))
                kparts.append(jnp.where(later, 0.0, prod))
        levels.append(lax.dot_general(jnp.concatenate(qparts, axis=0).astype(BF16),
                                      jnp.concatenate(kparts, axis=0).astype(BF16), _NT,
                                      preferred_element_type=F32))

    attn = jnp.zeros((c, c), F32)
    for li in range(len(GLA_LEVELS) - 1, -1, -1):
        attn = jnp.where(reg == li + 1, levels[li], attn)
    attn = jnp.where(reg == 0, diag, attn)
    return attn.astype(BF16), qd, kd, v.astype(BF16), total


def _gla_stage_b(attn, qd, kd, vb, total, st):
    o = jnp.dot(attn, vb, preferred_element_type=F32)
    o = o + lax.dot_general(qd, st.astype(BF16), _NT, preferred_element_type=F32)
    st_new = total * st + lax.dot_general(vb, kd, _TN, preferred_element_type=F32)
    return o, st_new


def _gla_kernel(zf_ref, zb_ref, v_ref, q_ref, og_ref, lb_ref, gw_ref, y_ref, of_scr, ob_scr, st_scr,
                emat_scr, attn_c, qd_c, kd_c, vb_c, tot_c, q_c, k_c, expo_c, diag_c,
                *, n_ctx, n_lat, heads):
    c = C_CHUNK
    row8 = lax.broadcasted_iota(jnp.int32, (SUBLANES, LANES), 0)
    lane = lax.broadcasted_iota(jnp.int32, (SUBLANES, LANES), 1)
    regs = (_gla_region_ids(False), _gla_region_ids(True))
    n_all = n_ctx + n_lat
    st_scr[...] = jnp.zeros_like(st_scr)
    emat_scr[0] = _gla_exponent_matrix(False)
    emat_scr[1] = _gla_exponent_matrix(True)
    streams = [(hh, rev) for hh in range(heads) for rev in (False, True)]

    def rows_of(i, rev):
        if isinstance(i, int):
            cidx = ((n_ctx - 1 - i) if i < n_ctx else (2 * n_ctx + n_lat - 1 - i)) if rev else i
            return pl.ds(cidx * c, c)
        cidx = jnp.where(i < n_ctx, n_ctx - 1 - i, 2 * n_ctx + n_lat - 1 - i) if rev else i
        return pl.ds(pl.multiple_of(cidx * c, c), c)

    def stage_a1(i):
        for k, (hh, rev) in enumerate(streams):
            ls = slice(hh * LANES, (hh + 1) * LANES)
            rows = rows_of(i, rev)
            z_ref = zb_ref if rev else zf_ref
            q, kk, expo, diag = _gla_stage_a1(z_ref[rows, ls], q_ref[rows, ls], lb_ref[:, ls],
                                              emat_scr[int(rev)], rev, lane)
            q_c[k] = q
            k_c[k] = kk
            expo_c[k] = expo
            diag_c[k] = diag

    def stage_a2(i):
        for k, (hh, rev) in enumerate(streams):
            ls = slice(hh * LANES, (hh + 1) * LANES)
            attn, qd, kd, vb, total = _gla_stage_a2(q_c[k], k_c[k], expo_c[k], diag_c[k],
                                                    v_ref[rows_of(i, rev), ls], rev, regs[int(rev)], row8)
            attn_c[k] = attn
            qd_c[k] = qd
            kd_c[k] = kd
            vb_c[k] = vb
            tot_c[k] = jnp.broadcast_to(total, (SUBLANES, LANES))

    def stage_b(i):
        for k, (hh, rev) in enumerate(streams):
            ls = slice(hh * LANES, (hh + 1) * LANES)
            o, st = _gla_stage_b(attn_c[k], qd_c[k], kd_c[k], vb_c[k], tot_c[k, 0:1, :], st_scr[k])
            (ob_scr if rev else of_scr)[rows_of(i, rev), ls] = o
            st_scr[k] = st

    stage_a1(0)
    stage_a2(0)
    stage_a1(1)

    def step(i):
        stage_b(i - 2)
        stage_a2(i - 1)
        stage_a1(i)

    def body(j, carry):
        step(2 + 2 * j)
        step(3 + 2 * j)
        return carry

    lax.fori_loop(0, (n_all - 2) // 2, body, 0)
    if (n_all - 2) % 2:
        step(n_all - 1)
    stage_b(n_all - 2)
    stage_a2(n_all - 1)
    stage_b(n_all - 1)

    gw = gw_ref[...]
    piece = 256

    def fin(i, carry):
        rows = pl.ds(pl.multiple_of(i * piece, piece), piece)
        for hh in range(heads):
            ls = slice(hh * LANES, (hh + 1) * LANES)
            o = of_scr[rows, ls] + ob_scr[rows, ls]
            y = o * lax.rsqrt(jnp.mean(o * o, axis=-1, keepdims=True) + RMS_EPS) * gw
            y_ref[rows, ls] = (y * _silu(og_ref[rows, ls])).astype(BF16)
        return carry

    lax.fori_loop(0, (n_all * c) // piece, fin, 0)


def _gla(p, lb, gw, t_ctx):
    b, t, _ = p.shape
    hps = GLA_HEADS_PER_STEP
    wb = hps * LANES
    nblk = C_HEADS // hps
    col = lambda k: pl.BlockSpec((None, t, wb), lambda i, h: (i, 0, k * nblk + h))
    return pl.pallas_call(
        functools.partial(_gla_kernel, n_ctx=t_ctx // C_CHUNK, n_lat=(t - t_ctx) // C_CHUNK, heads=hps),
        grid=(b, nblk),
        in_specs=[col(0), col(1), col(2), col(3), col(4),
                  pl.BlockSpec((None, 1, wb), lambda i, h: (h, 0, 0)),
                  pl.BlockSpec((1, LANES), lambda i, h: (0, 0))],
        out_specs=pl.BlockSpec((None, t, wb), lambda i, h: (i, 0, h)),
        out_shape=jax.ShapeDtypeStruct((b, t, C_HEADS * LANES), BF16),
        scratch_shapes=[pltpu.VMEM((t, wb), F32), pltpu.VMEM((t, wb), F32),
                        pltpu.VMEM((2 * hps, C_EXPAND, LANES), F32),
                        pltpu.VMEM((2, GLA_EXP_KINDS * C_CHUNK, 2 * C_CHUNK), BF16),
                        pltpu.VMEM((2 * hps, C_CHUNK, C_CHUNK), BF16),
                        pltpu.VMEM((2 * hps, C_CHUNK, LANES), BF16),
                        pltpu.VMEM((2 * hps, C_CHUNK, LANES), BF16),
                        pltpu.VMEM((2 * hps, C_CHUNK, LANES), BF16),
                        pltpu.VMEM((2 * hps, SUBLANES, LANES), F32),
                        pltpu.VMEM((2 * hps, C_CHUNK, LANES), F32),
                        pltpu.VMEM((2 * hps, C_CHUNK, LANES), F32),
                        pltpu.VMEM((2 * hps, GLA_EXP_KINDS * C_CHUNK, LANES), F32),
                        pltpu.VMEM((2 * hps, C_CHUNK, C_CHUNK), F32)],
        compiler_params=_params("parallel", "parallel"),
        name="hgrn2_gla",
    )(p, p, p, p, p, lb.reshape(nblk, 1, wb), gw.reshape(1, LANES))


def _outproj_kernel(*refs, n_in, final):
    y_refs = refs[:n_in]
    w_ref, h_ref, gt_ref = refs[n_in:n_in + 3]
    if final:
        fw_ref, o_ref = refs[n_in + 3:]
    else:
        (o_ref,) = refs[n_in + 3:]
    acc = None
    k0 = 0
    for y_ref in y_refs:
        kw = y_ref.shape[-1]
        part = jnp.dot(y_ref[...], w_ref[k0:k0 + kw, :], preferred_element_type=F32)
        acc = part if acc is None else acc + part
        k0 += kw
    tm, d = acc.shape
    hn = h_ref[...] + (acc.reshape(tm // SUBLANES, SUBLANES, d) * gt_ref[...][None]).reshape(tm, d)
    if final:
        hn = hn * lax.rsqrt(jnp.mean(hn * hn, axis=-1, keepdims=True) + RMS_EPS) * fw_ref[...]
    o_ref[...] = hn


def _outproj(ys, w, h, gate, n_ctx_tiles, final_w=None):
    b, t, d = h.shape
    tm = ROW_TILE
    final = final_w is not None
    off = n_ctx_tiles if final else 0
    nt = t // tm - off

    def grp(i, j):
        return jnp.where(j + off < n_ctx_tiles, b, i)

    in_specs = [pl.BlockSpec((None, tm, y.shape[-1]), lambda i, j: (i, j + off, 0)) for y in ys]
    in_specs += [pl.BlockSpec(w.shape, lambda i, j: (0, 0)),
                 pl.BlockSpec((None, tm, d), lambda i, j: (i, j + off, 0)),
                 pl.BlockSpec((None, SUBLANES, d), lambda i, j: (grp(i, j), 0, 0))]
    args = list(ys) + [w, h, gate]
    if final:
        in_specs.append(pl.BlockSpec((1, d), lambda i, j: (0, 0)))
        args.append(final_w.reshape(1, d))
    return pl.pallas_call(
        functools.partial(_outproj_kernel, n_in=len(ys), final=final),
        grid=(b, nt),
        in_specs=in_specs,
        out_specs=pl.BlockSpec((None, tm, d), lambda i, j: (i, j, 0)),
        out_shape=jax.ShapeDtypeStruct((b, nt * tm, d), F32),
        compiler_params=_params("parallel", "parallel"),
        name="outproj",
    )(*args)


def _rope_tables(s_lat, t_ctx):
    pos = np.arange(s_lat)
    row = (pos // GRID_W).astype(np.float32)
    colp = (pos % GRID_W).astype(np.float32)
    axis_dim = B_HEAD_DIM // 2
    inv = jnp.asarray(ROPE_BASE, F32) ** (-jnp.arange(0, axis_dim, 2, dtype=F32) / axis_dim)
    ang = jnp.concatenate([jnp.asarray(row)[:, None] * inv, jnp.asarray(colp)[:, None] * inv], axis=-1)
    cos, sin = jnp.cos(ang), jnp.sin(ang)
    zer = jnp.zeros_like(sin)
    pairs = lambda even, odd: jnp.stack([even, odd], axis=-1).reshape(s_lat, B_HEAD_DIM)
    cos64 = pairs(cos, cos)
    sa64 = pairs(-sin, zer)
    sb64 = pairs(zer, sin)
    one64 = jnp.ones_like(cos64)
    zer64 = jnp.zeros_like(cos64)

    def full(lat128, ident):
        return jnp.concatenate([jnp.broadcast_to(ident, (t_ctx, LANES)), lat128], axis=0)

    ident_c = jnp.ones((1, LANES), F32)
    ident_s = jnp.zeros((1, LANES), F32)
    q_tabs = (full(jnp.concatenate([cos64, cos64], -1), ident_c),
              full(jnp.concatenate([sa64, sa64], -1), ident_s),
              full(jnp.concatenate([sb64, sb64], -1), ident_s))
    kv_tabs = (full(jnp.concatenate([cos64, one64], -1), ident_c),
               full(jnp.concatenate([sa64, zer64], -1), ident_s),
               full(jnp.concatenate([sb64, zer64], -1), ident_s))
    return q_tabs, kv_tabs


def _even_weights(w_in):
    dh = B_HEAD_DIM
    n_state = A_WIDTH + 2 * B_KV_WIDTH
    q0 = n_state + A_WIDTH
    k0 = A_WIDTH
    v0 = A_WIDTH + B_KV_WIDTH
    parts = [w_in[:, q0:q0 + 2 * B_WIDTH]]
    for j in range(B_KV_HEADS):
        parts += [w_in[:, k0 + j * dh:k0 + (j + 1) * dh], w_in[:, v0 + j * dh:v0 + (j + 1) * dh]]
    w_b = jnp.concatenate(parts, axis=1).astype(BF16)
    w_a = jnp.concatenate([w_in[:, :A_WIDTH], w_in[:, n_state:n_state + A_WIDTH]], axis=1).astype(BF16)
    return w_b, w_a


def _groups(vec_b, vec_c):
    allv = jnp.concatenate([vec_b, vec_c[None, :]], axis=0)
    return jnp.broadcast_to(allv[:, None, :], (allv.shape[0], SUBLANES, allv.shape[1]))


def kernel(x, c, ctx, c_ctx, ada_w, ada_b, norm_w, ev_w_in, ev_conv_w, ev_conv_b, ev_rg_wx, ev_rg_bx,
           ev_rg_wa, ev_rg_ba, ev_rg_lambda, ev_sink, ev_w_out, od_w_in, od_lb_raw, od_gnorm_w, od_w_out,
           final_norm_w):
    b, s_lat, d = x.shape
    t_ctx = ctx.shape[1]
    t_all = t_ctx + s_lat
    assert b == SUBLANES and d == D_MODEL
    assert t_ctx % ROW_TILE == 0 and s_lat % ROW_TILE == 0
    n_ctx_tiles = t_ctx // ROW_TILE

    sc_rows = jnp.zeros((2 * SUBLANES, d), F32)
    sc_rows = sc_rows.at[:b].set(jax.nn.silu(c)).at[b].set(jax.nn.silu(c_ctx))
    mod = _ada_mod(sc_rows, ada_w, ada_b)

    lb_p = jax.nn.softmax(od_lb_raw.astype(F32), axis=0)
    lower_bounds = jnp.cumsum(lb_p, axis=0) - lb_p[0]
    q_tabs, kv_tabs = _rope_tables(s_lat, t_ctx)

    h = jnp.concatenate([ctx, x], axis=1)
    out = None
    for layer in range(DEPTH):
        last = layer == DEPTH - 1
        j = layer // 2
        ml = mod[layer]
        shift = _groups(ml[:b, :d], ml[b, :d])
        scale = _groups(ml[:b, d:2 * d], ml[b, d:2 * d])
        gate = _groups(ml[:b, 2 * d:], ml[b, 2 * d:])
        if layer % 2 == 0:
            w_b, w_a = _even_weights(ev_w_in[j])
            p_b, u = _inproj(h, shift, scale, norm_w[layer], w_b, n_ctx_tiles, emit_u=True)
            u_tm = jnp.transpose(u, (1, 0, 2)).reshape(t_all * b, d)
            p_a = _matmul(u_tm, w_a)
            la = -A_C * jax.nn.softplus(-ev_rg_lambda[j].astype(F32))
            w_cat = jnp.concatenate([ev_rg_wx[j], ev_rg_wa[j]], axis=-1).astype(BF16)
            rg = lambda dr, yf: _rglru_pass(
                p_a, yf, ev_conv_w[j], ev_conv_b[j].reshape(1, -1), w_cat[dr],
                ev_rg_bx[j, dr].reshape(1, -1), ev_rg_ba[j, dr].reshape(1, -1), la[dr].reshape(1, -1),
                t_ctx, s_lat, rev=bool(dr))
            ya = jnp.transpose(rg(1, rg(0, None)).reshape(t_all, b, A_WIDTH), (1, 0, 2))
            kvr, vt = _rope_kv(p_b, kv_tabs, col_block=2 * B_WIDTH // (2 * B_KV_WIDTH))
            yb = _attention(p_b, kvr, vt, ev_sink[j].astype(F32), q_tabs, t_ctx, gate_col_block=1)
            ys, w_out = [ya, yb], ev_w_out[j].astype(BF16)
        else:
            p = _inproj(h, shift, scale, norm_w[layer], od_w_in[j].astype(BF16), n_ctx_tiles, emit_u=False)
            ys, w_out = [_gla(p, lower_bounds[j], od_gnorm_w[j], t_ctx)], od_w_out[j].astype(BF16)
        if last:
            out = _outproj(ys, w_out, h, gate, n_ctx_tiles, final_w=final_norm_w)
        else:
            h = _outproj(ys, w_out, h, gate, n_ctx_tiles)
    return out
```

```python
import functools

import jax
import jax.numpy as jnp
import numpy as np
from jax import lax
from jax.experimental import pallas as pl
from jax.experimental.pallas import tpu as pltpu

F32 = jnp.float32
BF16 = jnp.bfloat16

D_MODEL = 1024
DEPTH = 4
GRID_W = 64
RMS_EPS = 1e-6
NEG_BIG = -1e30
F_MIN = 1e-30

A_WIDTH = D_MODEL
A_HEADS = 8
A_BLOCK = A_WIDTH // A_HEADS
A_CONV = 4
A_C = 8.0
B_HEADS = 16
B_KV_HEADS = 4
B_HEAD_DIM = 64
B_GROUP = B_HEADS // B_KV_HEADS
B_WIDTH = B_HEADS * B_HEAD_DIM
B_KV_WIDTH = B_KV_HEADS * B_HEAD_DIM
WINDOW = 128
ROPE_BASE = 10000.0
ATTN_SCALE = B_HEAD_DIM ** -0.5
LOG2E = 1.4426950408889634
C_EXPAND = 128
C_HEADS = D_MODEL // C_EXPAND
C_CHUNK = 64

SUBLANES = 8
LANES = 128
ROW_TILE = 256
PROJ_SUB = 3
GLA_HEADS_PER_STEP = 2
VMEM_LIMIT = 56 * 1024 * 1024

_NT = (((1,), (1,)), ((), ()))
_TN = (((0,), (0,)), ((), ()))


def _params(*sem):
    return pltpu.CompilerParams(dimension_semantics=sem, vmem_limit_bytes=VMEM_LIMIT)


def _sigmoid(x):
    return 0.5 * jnp.tanh(0.5 * x) + 0.5


def _silu(x):
    h = 0.5 * x
    return h + h * jnp.tanh(h)


def _ada_kernel(sc_ref, w_ref, b_ref, o_ref):
    o_ref[...] = jnp.dot(sc_ref[...], w_ref[...], preferred_element_type=F32,
                         precision=lax.Precision.HIGHEST) + b_ref[...]


def _ada_mod(sc, ada_w, ada_b):
    depth, d, n = ada_w.shape
    rows = sc.shape[0]
    tn = 1024
    return pl.pallas_call(
        _ada_kernel,
        grid=(depth, n // tn),
        in_specs=[pl.BlockSpec((rows, d), lambda l, j: (0, 0)),
                  pl.BlockSpec((None, d, tn), lambda l, j: (l, 0, j)),
                  pl.BlockSpec((None, 1, tn), lambda l, j: (l, 0, j))],
        out_specs=pl.BlockSpec((None, rows, tn), lambda l, j: (l, 0, j)),
        out_shape=jax.ShapeDtypeStruct((depth, rows, n), F32),
        compiler_params=_params("parallel", "parallel"),
        name="ada_mod",
    )(sc, ada_w, ada_b.reshape(depth, 1, n))


def _mod_specs(b, n_ctx_tiles, sub, off=0):
    def spec(k):
        return pl.BlockSpec((None, SUBLANES, D_MODEL),
                            lambda i, j: (jnp.where(sub * j + k + off < n_ctx_tiles, b, i), 0, 0))
    return [spec(k) for k in range(sub)]


def _rope(x, cos, sin_a, sin_b):
    return x * cos + pltpu.roll(x, LANES - 1, 1) * sin_a + pltpu.roll(x, 1, 1) * sin_b


def _inproj_kernel(*refs, sub, attn):
    h_ref = refs[0]
    sh_refs, sc_refs = refs[1:1 + sub], refs[1 + sub:1 + 2 * sub]
    nw_ref, w_ref = refs[1 + 2 * sub:3 + 2 * sub]
    rest = refs[3 + 2 * sub:]
    if attn:
        cos_ref, sa_ref, sb_ref, p_ref, u_ref, kv_ref, vt_ref, u_scr = rest
    else:
        p_ref, u_scr = rest
    d = h_ref.shape[-1]
    r = ROW_TILE
    for k in range(sub):
        x = h_ref[k * r:(k + 1) * r, :]
        y = x * lax.rsqrt(jnp.mean(x * x, axis=-1, keepdims=True) + RMS_EPS) * nw_ref[...]
        y3 = y.reshape(r // SUBLANES, SUBLANES, d)
        u = (y3 * (1.0 + sc_refs[k][...])[None] + sh_refs[k][...][None]).reshape(r, d)
        u_scr[k * r:(k + 1) * r, :] = u.astype(BF16)
    n = p_ref.shape[-1]
    p_ref[...] = jnp.dot(u_scr[...], w_ref[:, :n], preferred_element_type=F32)
    if attn:
        u_ref[...] = u_scr[...]
        kv = jnp.dot(u_scr[...], w_ref[:, n:], preferred_element_type=F32)
        cos, sa, sb = cos_ref[...], sa_ref[...], sb_ref[...]
        dh = B_HEAD_DIM
        for g in range(B_KV_HEADS):
            sl = slice(g * LANES, (g + 1) * LANES)
            x = _rope(kv[:, sl], cos, sa, sb)
            kv_ref[:, sl] = x.astype(BF16)
            vt_ref[g * dh:(g + 1) * dh, :] = x.T[dh:, :].astype(BF16)


def _inproj(h, shift, scale, norm_w, w, n_ctx_tiles, kv_tabs=None):
    b, t, d = h.shape
    attn = kv_tabs is not None
    wkv = 2 * B_KV_WIDTH
    n = w.shape[1] - (wkv if attn else 0)
    sub = PROJ_SUB
    tm = sub * ROW_TILE
    row = lambda width: pl.BlockSpec((None, tm, width), lambda i, j: (i, j, 0))
    in_specs = ([row(d)] + _mod_specs(b, n_ctx_tiles, sub) + _mod_specs(b, n_ctx_tiles, sub)
                + [pl.BlockSpec((1, d), lambda i, j: (0, 0)),
                   pl.BlockSpec(w.shape, lambda i, j: (0, 0), pipeline_mode=pl.Buffered(1))])
    args = [h] + [shift] * sub + [scale] * sub + [norm_w.reshape(1, d), w]
    out_shape = [jax.ShapeDtypeStruct((b, t, n), F32)]
    out_specs = [row(n)]
    if attn:
        in_specs += [pl.BlockSpec((tm, LANES), lambda i, j: (j, 0))] * 3
        args += list(kv_tabs)
        out_shape += [jax.ShapeDtypeStruct((b, t, d), BF16), jax.ShapeDtypeStruct((b, t, wkv), BF16),
                      jax.ShapeDtypeStruct((b, B_KV_WIDTH, t), BF16)]
        out_specs += [row(d), row(wkv), pl.BlockSpec((None, B_KV_WIDTH, tm), lambda i, j: (i, 0, j))]
    res = pl.pallas_call(
        functools.partial(_inproj_kernel, sub=sub, attn=attn),
        grid=(b, t // tm),
        in_specs=in_specs,
        out_specs=out_specs,
        out_shape=out_shape,
        scratch_shapes=[pltpu.VMEM((tm, d), BF16)],
        compiler_params=_params("parallel", "parallel"),
        name="inproj",
    )(*args)
    return res if attn else res[0]


def _matmul_kernel(u_ref, w_ref, p_ref):
    p_ref[...] = jnp.dot(u_ref[...], w_ref[...], preferred_element_type=F32)


def _matmul(u, w):
    r, d = u.shape
    n = w.shape[1]
    tm = 4 * ROW_TILE
    return pl.pallas_call(
        _matmul_kernel,
        grid=(r // tm,),
        in_specs=[pl.BlockSpec((tm, d), lambda i: (i, 0)),
                  pl.BlockSpec((d, n), lambda i: (0, 0), pipeline_mode=pl.Buffered(1))],
        out_specs=pl.BlockSpec((tm, n), lambda i: (i, 0)),
        out_shape=jax.ShapeDtypeStruct((r, n), F32),
        compiler_params=_params("parallel"),
        name="matmul_tm",
    )(u, w)


def _scan_tile(i, n_ctx, n_lat, rev):
    if not rev:
        return i
    return jnp.where(i < n_ctx, n_ctx - 1 - i, 2 * n_ctx + n_lat - 1 - i)


def _rglru_kernel(*refs, rev, tt, n_ctx, n_lat):
    if rev:
        (xa_ref, xp_ref, xn1_ref, xn2_ref, ga_ref, yf_ref, cw_ref, cb_ref, w_ref, bx_ref, ba_ref,
         la_ref, out_ref, h_scr, xpad_scr, u_scr, a_scr, b_scr) = refs
    else:
        (xa_ref, xp_ref, xn1_ref, xn2_ref, cw_ref, cb_ref, w_ref, bx_ref, ba_ref,
         la_ref, out_ref, h_scr, xpad_scr, u_scr, a_scr, b_scr) = refs
    i = pl.program_id(0)
    tile = _scan_tile(i, n_ctx, n_lat, rev)
    seg_first = jnp.logical_or(tile == 0, tile == n_ctx)
    seg_last = jnp.logical_or(tile == n_ctx - 1, tile == n_ctx + n_lat - 1)
    rows = tt * SUBLANES

    @pl.when(i == 0)
    def _():
        h_scr[...] = jnp.zeros_like(h_scr)

    xpad_scr[0:8, :] = jnp.where(seg_first, 0.0, xp_ref[...])
    xpad_scr[8:8 + rows, :] = xa_ref[...]
    xpad_scr[8 + rows:16 + rows, :] = jnp.where(seg_last, 0.0, xn1_ref[...])
    xpad_scr[16 + rows:24 + rows, :] = jnp.where(seg_last, 0.0, xn2_ref[...])
    u = cb_ref[...] + xpad_scr[0:rows, :] * cw_ref[0:1, :]
    for k in range(1, A_CONV):
        u = u + xpad_scr[8 * k:8 * k + rows, :] * cw_ref[k:k + 1, :]
    u_scr[...] = u

    for hd in range(A_HEADS):
        sl = slice(hd * A_BLOCK, (hd + 1) * A_BLOCK)
        uh = u_scr[:, sl]
        g = jnp.dot(uh.astype(BF16), w_ref[hd], preferred_element_type=F32)
        gate_x = 0.5 * jnp.tanh(g[:, :A_BLOCK] + bx_ref[:, sl]) + 0.5
        lah = la_ref[:, sl]
        log_a = lah + lah * jnp.tanh(g[:, A_BLOCK:] + ba_ref[:, sl])
        a = jnp.exp(log_a)
        a_scr[:, sl] = a
        b_scr[:, sl] = jnp.sqrt(-jnp.tanh(log_a) * (1.0 + a * a)) * gate_x * uh

    def step(s, h):
        t = (tt - 1 - s) if rev else s
        r0 = pl.multiple_of(t * SUBLANES, SUBLANES)
        h = a_scr[pl.ds(r0, SUBLANES), :] * h + b_scr[pl.ds(r0, SUBLANES), :]
        b_scr[pl.ds(r0, SUBLANES), :] = h
        return h

    h_scr[...] = lax.fori_loop(0, tt, step, h_scr[...], unroll=8)
    if rev:
        out_ref[...] = ((yf_ref[...] + b_scr[...]) * _silu(ga_ref[...])).astype(BF16)
    else:
        out_ref[...] = b_scr[...]


def _rglru_pass(p_a, yf, conv_w, conv_b, w_cat, bx, ba, la, t_ctx, t_lat, rev):
    w = A_WIDTH
    tt = 64
    rows = tt * SUBLANES
    n_ctx, n_lat = t_ctx // tt, t_lat // tt
    t_all = t_ctx + t_lat
    tile = functools.partial(_scan_tile, n_ctx=n_ctx, n_lat=n_lat, rev=rev)
    const = lambda i: (0, 0)
    in_specs = [pl.BlockSpec((rows, w), lambda i: (tile(i), 0)),
                pl.BlockSpec((SUBLANES, w), lambda i: (jnp.maximum(tile(i) * tt - 1, 0), 0)),
                pl.BlockSpec((SUBLANES, w), lambda i: (jnp.minimum(tile(i) * tt + tt, t_all - 1), 0)),
                pl.BlockSpec((SUBLANES, w), lambda i: (jnp.minimum(tile(i) * tt + tt + 1, t_all - 1), 0))]
    args = [p_a, p_a, p_a, p_a]
    if rev:
        in_specs += [pl.BlockSpec((rows, w), lambda i: (tile(i), 1)),
                     pl.BlockSpec((rows, w), lambda i: (tile(i), 0))]
        args += [p_a, yf]
    in_specs += [pl.BlockSpec((A_CONV, w), const), pl.BlockSpec((1, w), const),
                 pl.BlockSpec((A_HEADS, A_BLOCK, 2 * A_BLOCK), lambda i: (0, 0, 0)),
                 pl.BlockSpec((1, w), const), pl.BlockSpec((1, w), const), pl.BlockSpec((1, w), const)]
    args += [conv_w, conv_b, w_cat, bx, ba, la]
    return pl.pallas_call(
        functools.partial(_rglru_kernel, rev=rev, tt=tt, n_ctx=n_ctx, n_lat=n_lat),
        grid=(n_ctx + n_lat,),
        in_specs=in_specs,
        out_specs=pl.BlockSpec((rows, w), lambda i: (tile(i), 0)),
        out_shape=jax.ShapeDtypeStruct((t_all * SUBLANES, w), BF16 if rev else F32),
        scratch_shapes=[pltpu.VMEM((SUBLANES, w), F32),
                        pltpu.VMEM((rows + 3 * SUBLANES, w), F32),
                        pltpu.VMEM((rows, w), F32),
                        pltpu.VMEM((rows, w), F32),
                        pltpu.VMEM((rows, w), F32)],
        compiler_params=_params("arbitrary"),
        name="rglru_bwd" if rev else "rglru_fwd",
    )(*args)


def _attn_kernel(sink_ref, q_ref, kc_ref, kp_ref, ko_ref, kn_ref, vc_ref, vp_ref, vo_ref, vn_ref, gb_ref,
                 cos_ref, sa_ref, sb_ref, o_ref, s_scr, p_scr, bias_scr, *, n_ctx_blk, n_lat_blk):
    n = pl.program_id(1)
    is_lat = n >= n_ctx_blk

    @pl.when(is_lat)
    def _():
        _attend(True, n - n_ctx_blk, n_lat_blk, sink_ref, q_ref, (kc_ref, kp_ref, ko_ref, kn_ref),
                (vc_ref, vp_ref, vo_ref, vn_ref), gb_ref, cos_ref, sa_ref, sb_ref, o_ref, s_scr, p_scr, bias_scr)

    @pl.when(jnp.logical_not(is_lat))
    def _():
        _attend(False, None, None, sink_ref, q_ref, (kc_ref,), (vc_ref,), gb_ref, cos_ref, sa_ref, sb_ref,
                o_ref, s_scr, p_scr, bias_scr)


def _attend(window, m, n_lat_blk, sink_ref, q_ref, k_refs, v_refs, gb_ref, cos_ref, sa_ref, sb_ref,
            o_ref, s_scr, p_scr, bias_scr):
    w = WINDOW
    dh = B_HEAD_DIM
    lc = k_refs[0].shape[0]

    cos, sa, sb = cos_ref[...], sa_ref[...], sb_ref[...]
    low = lax.broadcasted_iota(jnp.int32, (w, LANES), 1) < dh
    segs = [(0, lc, None)]
    if window:
        kj = lax.broadcasted_iota(jnp.int32, (w, w), 0)
        qi = lax.broadcasted_iota(jnp.int32, (w, w), 1)
        neg = jnp.full((w, w), NEG_BIG, F32)
        zer = jnp.zeros((w, w), F32)
        bias_scr[0] = jnp.where(jnp.logical_and(m >= 1, kj >= qi), zer, neg)
        bias_scr[1] = jnp.where(jnp.logical_and(m <= n_lat_blk - 2, kj <= qi), zer, neg)
        segs += [(lc, w, 0), (lc + w, w, None), (lc + 2 * w, w, 1)]
    nk = segs[-1][0] + segs[-1][1]
    piece = 32
    pieces = [(r0, bi, p0) for r0, nr, bi in segs for p0 in range(0, nr, piece)]

    def scores(g, cb, r0, bi, p0):
        x = s_scr[g, r0 + p0:r0 + p0 + piece, cb]
        return x if bi is None else x + bias_scr[bi, p0:p0 + piece, :]

    def tree(xs, op):
        while len(xs) > 1:
            xs = [op(xs[i], xs[i + 1]) for i in range(0, len(xs) - 1, 2)] + ([xs[-1]] if len(xs) % 2 else [])
        return xs[0]

    def fold8(x, op):
        return op(x.reshape(x.shape[0] // SUBLANES, SUBLANES, x.shape[1]), axis=0)

    for g in range(B_KV_HEADS):
        parts = []
        for c in range(2):
            col0 = g * B_GROUP * dh + c * LANES
            qc = _rope(q_ref[:, col0:col0 + LANES], cos, sa, sb) * (ATTN_SCALE * LOG2E)
            parts.append(jnp.where(low, qc, 0.0).astype(BF16))
            parts.append(jnp.where(low, pltpu.roll(qc, dh, 1), 0.0).astype(BF16))
        qs = jnp.concatenate(parts, axis=0)
        ks = slice(g * LANES, (g + 1) * LANES)
        kv_all = jnp.concatenate([r[:, ks] for r in k_refs], axis=0)
        s_scr[g, 0:nk, :] = lax.dot_general(kv_all, qs, _NT, preferred_element_type=F32)

        rdens = []
        for h in range(B_GROUP):
            cb = slice(h * w, (h + 1) * w)
            snk = sink_ref[g * B_GROUP + h] * LOG2E
            mx8 = tree([fold8(scores(g, cb, r0, bi, p0), jnp.max) for r0, bi, p0 in pieces], jnp.maximum)
            mx = jnp.maximum(jnp.max(mx8, axis=0, keepdims=True), snk)
            sums = []
            for r0, bi, p0 in pieces:
                p = jnp.exp2(scores(g, cb, r0, bi, p0) - mx)
                sums.append(fold8(p, jnp.sum))
                p_scr[g, r0 + p0:r0 + p0 + piece, cb] = p.astype(BF16)
            den8 = tree(sums, jnp.add)
            rdens.append(1.0 / (jnp.sum(den8, axis=0, keepdims=True) + jnp.exp2(snk - mx)))

        vt_all = jnp.concatenate([r[g * dh:(g + 1) * dh, :] for r in v_refs], axis=1)
        ov = jnp.dot(vt_all, p_scr[g, 0:nk, :], preferred_element_type=F32) * jnp.concatenate(rdens, axis=1)
        for c in range(2):
            pair = jnp.concatenate([ov[:, (2 * c) * w:(2 * c + 1) * w], ov[:, (2 * c + 1) * w:(2 * c + 2) * w]],
                                   axis=0)
            col0 = g * B_GROUP * dh + c * LANES
            y = pair.T * _silu(gb_ref[:, col0:col0 + LANES])
            o_ref[:, col0:col0 + LANES] = y.astype(BF16)


def _attention(p_b, kvr, vt, sink, q_tabs, t_ctx, gate_col_block):
    b, t, _ = p_b.shape
    w = WINDOW
    n_ctx_blk, n_lat_blk = t_ctx // w, (t - t_ctx) // w
    nb = t // w
    wkv = 2 * B_KV_WIDTH
    prev = lambda n: jnp.maximum(n - 1, 0)
    nxt = lambda n: jnp.minimum(n + 1, nb - 1)
    kvb = lambda f: pl.BlockSpec((None, w, wkv), lambda i, n, s: (i, f(n), 0))
    vtb = lambda f: pl.BlockSpec((None, B_KV_WIDTH, w), lambda i, n, s: (i, 0, f(n)))
    tab = pl.BlockSpec((w, LANES), lambda i, n, s: (n, 0))
    grid_spec = pltpu.PrefetchScalarGridSpec(
        num_scalar_prefetch=1,
        grid=(b, nb),
        in_specs=[pl.BlockSpec((None, w, B_WIDTH), lambda i, n, s: (i, n, 0)),
                  pl.BlockSpec((None, t_ctx, wkv), lambda i, n, s: (i, 0, 0)),
                  kvb(prev), kvb(lambda n: n), kvb(nxt),
                  pl.BlockSpec((None, B_KV_WIDTH, t_ctx), lambda i, n, s: (i, 0, 0)),
                  vtb(prev), vtb(lambda n: n), vtb(nxt),
                  pl.BlockSpec((None, w, B_WIDTH), lambda i, n, s: (i, n, gate_col_block)),
                  tab, tab, tab],
        out_specs=pl.BlockSpec((None, w, B_WIDTH), lambda i, n, s: (i, n, 0)),
        scratch_shapes=[pltpu.VMEM((B_KV_HEADS, t_ctx + 3 * w, B_GROUP * w), F32),
                        pltpu.VMEM((B_KV_HEADS, t_ctx + 3 * w, B_GROUP * w), BF16),
                        pltpu.VMEM((2, w, w), F32)],
    )
    return pl.pallas_call(
        functools.partial(_attn_kernel, n_ctx_blk=n_ctx_blk, n_lat_blk=n_lat_blk),
        grid_spec=grid_spec,
        out_shape=jax.ShapeDtypeStruct((b, t, B_WIDTH), BF16),
        compiler_params=_params("parallel", "parallel"),
        name="window_attn",
    )(sink, p_b, kvr, kvr, kvr, kvr, vt, vt, vt, vt, p_b, *q_tabs)


GLA_DIAG = 4
GLA_LEVELS = (4, 8, 16, 32)
GLA_EXP_KINDS = 2 + len(GLA_LEVELS)
GLA_UNROLL = 2


def _gla_region_ids(rev):
    c = C_CHUNK
    ri = lax.broadcasted_iota(jnp.int32, (c, c), 0)
    ci = lax.broadcasted_iota(jnp.int32, (c, c), 1)
    reg = jnp.full((c, c), len(GLA_LEVELS), jnp.int32)
    for li in range(len(GLA_LEVELS) - 1, -1, -1):
        m = GLA_LEVELS[li]
        reg = jnp.where(ri // m == ci // m, li, reg)
    seen = (ci >= ri) if rev else (ci <= ri)
    return jnp.where(seen, reg, -1)


def _gla_exponent_matrix(rev):
    c = C_CHUNK
    t = np.arange(c)[:, None]
    s = np.arange(c)[None, :]
    upto = (s >= t) if rev else (s <= t)
    after = ~upto
    mats = [upto, after]
    for m in GLA_LEVELS:
        later = ((t // m) % 2 == 0) if rev else ((t // m) % 2 == 1)
        mats.append((t // m == s // m) & np.where(later, upto, after))
    mat = np.concatenate(mats, axis=0).astype(np.float32)
    return np.concatenate([mat, mat], axis=1)


def _grp(x, j):
    return x[j * SUBLANES:(j + 1) * SUBLANES, :]


def _gla_stage_a1(z, qr, lb, emat, rev, lane):
    c = C_CHUNK
    ng = c // SUBLANES

    half = 0.5 * (1.0 - lb)
    ht = half * jnp.tanh(z)
    f = (lb + half) + ht
    kk = half - ht
    ft = jnp.maximum(f, F_MIN)
    q = _silu(qr)

    g2 = jnp.log2(ft)
    g_hi = g2.astype(BF16)
    g_lo = (g2 - g_hi.astype(F32)).astype(BF16)
    expo = jnp.dot(emat, jnp.concatenate([g_hi, g_lo], axis=0), preferred_element_type=F32)

    dparts = []
    for j in range(ng):
        fg, qg, kg = _grp(ft, j), _grp(q, j), _grp(kk, j)
        w_ = kg
        acc = jnp.where(lane == 0, jnp.sum(qg * kg, axis=-1, keepdims=True), 0.0)
        for dlt in range(1, GLA_DIAG):
            w_ = fg * pltpu.roll(w_, (SUBLANES - 1) if rev else 1, 0)
            red = jnp.sum(qg * w_, axis=-1, keepdims=True)
            acc = jnp.where(lane == (dlt if rev else LANES - dlt), red, acc)
        dparts.append(acc)
    diag = pltpu.roll(jnp.concatenate(dparts, axis=0), 0, 1, stride=1, stride_axis=0)[:, :c]
    return q, kk, expo, diag


def _gla_stage_a2(q, kk, expo, diag, v, rev, reg, row8):
    c = C_CHUNK
    ng = c // SUBLANES
    dec = jnp.exp2(expo)
    last = 0 if rev else c - 1
    total = dec[last:last + 1, :]
    qd = (q * dec[0:c]).astype(BF16)
    kd = (kk * dec[c:2 * c]).astype(BF16)
    grp = _grp

    zero = jnp.zeros((SUBLANES, LANES), F32)
    levels = []
    for li, m in enumerate(GLA_LEVELS):
        fac = dec[(2 + li) * c:(3 + li) * c]
        qparts, kparts = [], []
        for j in range(ng):
            if m >= SUBLANES:
                later = ((j * SUBLANES // m) % 2 == 1) != rev
                prod = (grp(q, j) if later else grp(kk, j)) * grp(fac, j)
                qparts.append(prod if later else zero)
                kparts.append(zero if later else prod)
            else:
                later = ((row8 // m) % 2 == 0) if rev else ((row8 // m) % 2 == 1)
                prod = jnp.where(later, grp(q, j), grp(kk, j)) * grp(fac, j)
                qparts.append(jnp.where(later, prod, 0.0))
                kparts.append(jnp.where(later, 0.0, prod))
        levels.append(lax.dot_general(jnp.concatenate(qparts, axis=0).astype(BF16),
                                      jnp.concatenate(kparts, axis=0).astype(BF16), _NT,
                                      preferred_element_type=F32))

    attn = jnp.zeros((c, c), F32)
    for li in range(len(GLA_LEVELS) - 1, -1, -1):
        attn = jnp.where(reg == li + 1, levels[li], attn)
    attn = jnp.where(reg == 0, diag, attn)
    return attn.astype(BF16), qd, kd, v.astype(BF16), total


def _gla_stage_b(attn, qd, kd, vb, total, st):
    o = jnp.dot(attn, vb, preferred_element_type=F32)
    o = o + lax.dot_general(qd, st.astype(BF16), _NT, preferred_element_type=F32)
    st_new = total * st + lax.dot_general(vb, kd, _TN, preferred_element_type=F32)
    return o, st_new


def _gla_kernel(zf_ref, zb_ref, v_ref, q_ref, og_ref, lb_ref, gw_ref, emat_ref, y_ref, of_scr, ob_scr, st_scr,
                attn_c, qd_c, kd_c, vb_c, tot_c, q_c, k_c, expo_c, diag_c,
                *, n_ctx, n_lat, heads):
    c = C_CHUNK
    row8 = lax.broadcasted_iota(jnp.int32, (SUBLANES, LANES), 0)
    lane = lax.broadcasted_iota(jnp.int32, (SUBLANES, LANES), 1)
    regs = (_gla_region_ids(False), _gla_region_ids(True))
    n_all = n_ctx + n_lat
    st_scr[...] = jnp.zeros_like(st_scr)
    streams = [(hh, rev) for hh in range(heads) for rev in (False, True)]

    def rows_of(i, rev):
        if isinstance(i, int):
            cidx = ((n_ctx - 1 - i) if i < n_ctx else (2 * n_ctx + n_lat - 1 - i)) if rev else i
            return pl.ds(cidx * c, c)
        cidx = jnp.where(i < n_ctx, n_ctx - 1 - i, 2 * n_ctx + n_lat - 1 - i) if rev else i
        return pl.ds(pl.multiple_of(cidx * c, c), c)

    def stage_a1(i):
        for k, (hh, rev) in enumerate(streams):
            ls = slice(hh * LANES, (hh + 1) * LANES)
            rows = rows_of(i, rev)
            z_ref = zb_ref if rev else zf_ref
            q, kk, expo, diag = _gla_stage_a1(z_ref[rows, ls], q_ref[rows, ls], lb_ref[:, ls],
                                              emat_ref[int(rev)], rev, lane)
            q_c[k] = q
            k_c[k] = kk
            expo_c[k] = expo
            diag_c[k] = diag

    def stage_a2(i):
        for k, (hh, rev) in enumerate(streams):
            ls = slice(hh * LANES, (hh + 1) * LANES)
            attn, qd, kd, vb, total = _gla_stage_a2(q_c[k], k_c[k], expo_c[k], diag_c[k],
                                                    v_ref[rows_of(i, rev), ls], rev, regs[int(rev)], row8)
            attn_c[k] = attn
            qd_c[k] = qd
            kd_c[k] = kd
            vb_c[k] = vb
            tot_c[k] = jnp.broadcast_to(total, (SUBLANES, LANES))

    def stage_b(i):
        for k, (hh, rev) in enumerate(streams):
            ls = slice(hh * LANES, (hh + 1) * LANES)
            o, st = _gla_stage_b(attn_c[k], qd_c[k], kd_c[k], vb_c[k], tot_c[k, 0:1, :], st_scr[k])
            (ob_scr if rev else of_scr)[rows_of(i, rev), ls] = o
            st_scr[k] = st

    stage_a1(0)
    stage_a2(0)
    stage_a1(1)

    def step(i):
        stage_b(i - 2)
        stage_a2(i - 1)
        stage_a1(i)

    def body(j, carry):
        for r in range(GLA_UNROLL):
            step(2 + GLA_UNROLL * j + r)
        return carry

    trips = (n_all - 2) // GLA_UNROLL
    lax.fori_loop(0, trips, body, 0)
    for i in range(2 + trips * GLA_UNROLL, n_all):
        step(i)
    stage_b(n_all - 2)
    stage_a2(n_all - 1)
    stage_b(n_all - 1)

    gw = gw_ref[...]
    piece = 256

    def fin(i, carry):
        rows = pl.ds(pl.multiple_of(i * piece, piece), piece)
        for hh in range(heads):
            ls = slice(hh * LANES, (hh + 1) * LANES)
            o = of_scr[rows, ls] + ob_scr[rows, ls]
            y = o * lax.rsqrt(jnp.mean(o * o, axis=-1, keepdims=True) + RMS_EPS) * gw
            y_ref[rows, ls] = (y * _silu(og_ref[rows, ls])).astype(BF16)
        return carry

    lax.fori_loop(0, (n_all * c) // piece, fin, 0)


def _gla(p, lb, gw, t_ctx):
    b, t, _ = p.shape
    hps = GLA_HEADS_PER_STEP
    wb = hps * LANES
    nblk = C_HEADS // hps
    col = lambda k: pl.BlockSpec((None, t, wb), lambda i, h: (i, 0, k * nblk + h))
    emat = jnp.asarray(np.stack([_gla_exponent_matrix(False), _gla_exponent_matrix(True)]), BF16)
    return pl.pallas_call(
        functools.partial(_gla_kernel, n_ctx=t_ctx // C_CHUNK, n_lat=(t - t_ctx) // C_CHUNK, heads=hps),
        grid=(b, nblk),
        in_specs=[col(0), col(1), col(2), col(3), col(4),
                  pl.BlockSpec((None, 1, wb), lambda i, h: (h, 0, 0)),
                  pl.BlockSpec((1, LANES), lambda i, h: (0, 0)),
                  pl.BlockSpec(emat.shape, lambda i, h: (0, 0, 0))],
        out_specs=pl.BlockSpec((None, t, wb), lambda i, h: (i, 0, h)),
        out_shape=jax.ShapeDtypeStruct((b, t, C_HEADS * LANES), BF16),
        scratch_shapes=[pltpu.VMEM((t, wb), F32), pltpu.VMEM((t, wb), F32),
                        pltpu.VMEM((2 * hps, C_EXPAND, LANES), F32),
                        pltpu.VMEM((2 * hps, C_CHUNK, C_CHUNK), BF16),
                        pltpu.VMEM((2 * hps, C_CHUNK, LANES), BF16),
                        pltpu.VMEM((2 * hps, C_CHUNK, LANES), BF16),
                        pltpu.VMEM((2 * hps, C_CHUNK, LANES), BF16),
                        pltpu.VMEM((2 * hps, SUBLANES, LANES), F32),
                        pltpu.VMEM((2 * hps, C_CHUNK, LANES), F32),
                        pltpu.VMEM((2 * hps, C_CHUNK, LANES), F32),
                        pltpu.VMEM((2 * hps, GLA_EXP_KINDS * C_CHUNK, LANES), F32),
                        pltpu.VMEM((2 * hps, C_CHUNK, C_CHUNK), F32)],
        compiler_params=_params("parallel", "parallel"),
        name="hgrn2_gla",
    )(p, p, p, p, p, lb.reshape(nblk, 1, wb), gw.reshape(1, LANES), emat)


def _outproj_kernel(*refs, n_in, sub, final):
    y_refs = refs[:n_in]
    w_ref, h_ref = refs[n_in:n_in + 2]
    gt_refs = refs[n_in + 2:n_in + 2 + sub]
    if final:
        fw_ref, o_ref = refs[n_in + 2 + sub:]
    else:
        (o_ref,) = refs[n_in + 2 + sub:]
    acc = None
    k0 = 0
    for y_ref in y_refs:
        kw = y_ref.shape[-1]
        part = jnp.dot(y_ref[...], w_ref[k0:k0 + kw, :], preferred_element_type=F32)
        acc = part if acc is None else acc + part
        k0 += kw
    d = acc.shape[-1]
    r = ROW_TILE
    for k in range(sub):
        a3 = acc[k * r:(k + 1) * r, :].reshape(r // SUBLANES, SUBLANES, d)
        hn = h_ref[k * r:(k + 1) * r, :] + (a3 * gt_refs[k][...][None]).reshape(r, d)
        if final:
            hn = hn * lax.rsqrt(jnp.mean(hn * hn, axis=-1, keepdims=True) + RMS_EPS) * fw_ref[...]
        o_ref[k * r:(k + 1) * r, :] = hn


def _outproj(ys, w, h, gate, n_ctx_tiles, final_w=None):
    b, t, d = h.shape
    final = final_w is not None
    sub = 1 if final else PROJ_SUB
    tm = sub * ROW_TILE
    off = n_ctx_tiles if final else 0
    nt = t // tm - off
    in_specs = [pl.BlockSpec((None, tm, y.shape[-1]), lambda i, j: (i, j + off, 0)) for y in ys]
    in_specs += [pl.BlockSpec(w.shape, lambda i, j: (0, 0), pipeline_mode=pl.Buffered(1)),
                 pl.BlockSpec((None, tm, d), lambda i, j: (i, j + off, 0))]
    in_specs += _mod_specs(b, n_ctx_tiles, sub, off)
    args = list(ys) + [w, h] + [gate] * sub
    if final:
        in_specs.append(pl.BlockSpec((1, d), lambda i, j: (0, 0)))
        args.append(final_w.reshape(1, d))
    return pl.pallas_call(
        functools.partial(_outproj_kernel, n_in=len(ys), sub=sub, final=final),
        grid=(b, nt),
        in_specs=in_specs,
        out_specs=pl.BlockSpec((None, tm, d), lambda i, j: (i, j, 0)),
        out_shape=jax.ShapeDtypeStruct((b, nt * tm, d), F32),
        compiler_params=_params("parallel", "parallel"),
        name="outproj",
    )(*args)


def _rope_tables(s_lat, t_ctx):
    pos = np.arange(s_lat)
    row = (pos // GRID_W).astype(np.float32)
    colp = (pos % GRID_W).astype(np.float32)
    axis_dim = B_HEAD_DIM // 2
    inv = jnp.asarray(ROPE_BASE, F32) ** (-jnp.arange(0, axis_dim, 2, dtype=F32) / axis_dim)
    ang = jnp.concatenate([jnp.asarray(row)[:, None] * inv, jnp.asarray(colp)[:, None] * inv], axis=-1)
    cos, sin = jnp.cos(ang), jnp.sin(ang)
    zer = jnp.zeros_like(sin)
    pairs = lambda even, odd: jnp.stack([even, odd], axis=-1).reshape(s_lat, B_HEAD_DIM)
    cos64 = pairs(cos, cos)
    sa64 = pairs(-sin, zer)
    sb64 = pairs(zer, sin)
    one64 = jnp.ones_like(cos64)
    zer64 = jnp.zeros_like(cos64)

    def full(lat128, ident):
        return jnp.concatenate([jnp.broadcast_to(ident, (t_ctx, LANES)), lat128], axis=0)

    ident_c = jnp.ones((1, LANES), F32)
    ident_s = jnp.zeros((1, LANES), F32)
    q_tabs = (full(jnp.concatenate([cos64, cos64], -1), ident_c),
              full(jnp.concatenate([sa64, sa64], -1), ident_s),
              full(jnp.concatenate([sb64, sb64], -1), ident_s))
    kv_tabs = (full(jnp.concatenate([cos64, one64], -1), ident_c),
               full(jnp.concatenate([sa64, zer64], -1), ident_s),
               full(jnp.concatenate([sb64, zer64], -1), ident_s))
    return q_tabs, kv_tabs


def _even_weights(w_in):
    dh = B_HEAD_DIM
    n_state = A_WIDTH + 2 * B_KV_WIDTH
    q0 = n_state + A_WIDTH
    k0 = A_WIDTH
    v0 = A_WIDTH + B_KV_WIDTH
    parts = [w_in[:, q0:q0 + 2 * B_WIDTH]]
    for j in range(B_KV_HEADS):
        parts += [w_in[:, k0 + j * dh:k0 + (j + 1) * dh], w_in[:, v0 + j * dh:v0 + (j + 1) * dh]]
    w_b = jnp.concatenate(parts, axis=1).astype(BF16)
    w_a = jnp.concatenate([w_in[:, :A_WIDTH], w_in[:, n_state:n_state + A_WIDTH]], axis=1).astype(BF16)
    return w_b, w_a


def _groups(vec_b, vec_c):
    allv = jnp.concatenate([vec_b, vec_c[None, :]], axis=0)
    return jnp.broadcast_to(allv[:, None, :], (allv.shape[0], SUBLANES, allv.shape[1]))


def kernel(x, c, ctx, c_ctx, ada_w, ada_b, norm_w, ev_w_in, ev_conv_w, ev_conv_b, ev_rg_wx, ev_rg_bx,
           ev_rg_wa, ev_rg_ba, ev_rg_lambda, ev_sink, ev_w_out, od_w_in, od_lb_raw, od_gnorm_w, od_w_out,
           final_norm_w):
    b, s_lat, d = x.shape
    t_ctx = ctx.shape[1]
    t_all = t_ctx + s_lat
    assert b == SUBLANES and d == D_MODEL
    assert t_ctx % ROW_TILE == 0 and s_lat % ROW_TILE == 0 and t_all % (PROJ_SUB * ROW_TILE) == 0
    n_ctx_tiles = t_ctx // ROW_TILE

    sc_rows = jnp.zeros((2 * SUBLANES, d), F32)
    sc_rows = sc_rows.at[:b].set(jax.nn.silu(c)).at[b].set(jax.nn.silu(c_ctx))
    mod = _ada_mod(sc_rows, ada_w, ada_b)

    lb_p = jax.nn.softmax(od_lb_raw.astype(F32), axis=0)
    lower_bounds = jnp.cumsum(lb_p, axis=0) - lb_p[0]
    q_tabs, kv_tabs = _rope_tables(s_lat, t_ctx)

    h = jnp.concatenate([ctx, x], axis=1)
    out = None
    for layer in range(DEPTH):
        last = layer == DEPTH - 1
        j = layer // 2
        ml = mod[layer]
        shift = _groups(ml[:b, :d], ml[b, :d])
        scale = _groups(ml[:b, d:2 * d], ml[b, d:2 * d])
        gate = _groups(ml[:b, 2 * d:], ml[b, 2 * d:])
        if layer % 2 == 0:
            w_b, w_a = _even_weights(ev_w_in[j])
            p_b, u, kvr, vt = _inproj(h, shift, scale, norm_w[layer], w_b, n_ctx_tiles, kv_tabs=kv_tabs)
            u_tm = jnp.transpose(u, (1, 0, 2)).reshape(t_all * b, d)
            p_a = _matmul(u_tm, w_a)
            lah = -0.5 * A_C * jax.nn.softplus(-ev_rg_lambda[j].astype(F32))
            w_cat = (0.5 * jnp.concatenate([ev_rg_wx[j], ev_rg_wa[j]], axis=-1)).astype(BF16)
            rg = lambda dr, yf: _rglru_pass(
                p_a, yf, ev_conv_w[j], ev_conv_b[j].reshape(1, -1), w_cat[dr],
                0.5 * ev_rg_bx[j, dr].reshape(1, -1), 0.5 * ev_rg_ba[j, dr].reshape(1, -1), lah[dr].reshape(1, -1),
                t_ctx, s_lat, rev=bool(dr))
            ya = jnp.transpose(rg(1, rg(0, None)).reshape(t_all, b, A_WIDTH), (1, 0, 2))
            yb = _attention(p_b, kvr, vt, ev_sink[j].astype(F32), q_tabs, t_ctx, gate_col_block=1)
            ys, w_out = [ya, yb], ev_w_out[j].astype(BF16)
        else:
            w_odd = od_w_in[j].at[:, :2 * C_HEADS * C_EXPAND].multiply(0.5).astype(BF16)
            p = _inproj(h, shift, scale, norm_w[layer], w_odd, n_ctx_tiles)
            ys, w_out = [_gla(p, lower_bounds[j], od_gnorm_w[j], t_ctx)], od_w_out[j].astype(BF16)
        if last:
            out = _outproj(ys, w_out, h, gate, n_ctx_tiles, final_w=final_norm_w)
        else:
            h = _outproj(ys, w_out, h, gate, n_ctx_tiles)
    return out
```

```python
import functools

import jax
import jax.numpy as jnp
import numpy as np
from jax import lax
from jax.experimental import pallas as pl
from jax.experimental.pallas import tpu as pltpu

F32 = jnp.float32
BF16 = jnp.bfloat16

D_MODEL = 1024
DEPTH = 4
GRID_W = 64
RMS_EPS = 1e-6
NEG_BIG = -1e30
F_MIN = 1e-30

A_WIDTH = D_MODEL
A_HEADS = 8
A_BLOCK = A_WIDTH // A_HEADS
A_CONV = 4
A_C = 8.0
B_HEADS = 16
B_KV_HEADS = 4
B_HEAD_DIM = 64
B_GROUP = B_HEADS // B_KV_HEADS
B_WIDTH = B_HEADS * B_HEAD_DIM
B_KV_WIDTH = B_KV_HEADS * B_HEAD_DIM
WINDOW = 128
ROPE_BASE = 10000.0
ATTN_SCALE = B_HEAD_DIM ** -0.5
LOG2E = 1.4426950408889634
C_EXPAND = 128
C_HEADS = D_MODEL // C_EXPAND
C_CHUNK = 64

SUBLANES = 8
LANES = 128
ROW_TILE = 256
PROJ_SUB = 3
GLA_HEADS_PER_STEP = 2
VMEM_LIMIT = 56 * 1024 * 1024

_NT = (((1,), (1,)), ((), ()))
_TN = (((0,), (0,)), ((), ()))


def _params(*sem):
    return pltpu.CompilerParams(dimension_semantics=sem, vmem_limit_bytes=VMEM_LIMIT)


def _sigmoid(x):
    return 0.5 * jnp.tanh(0.5 * x) + 0.5


def _silu(x):
    h = 0.5 * x
    return h + h * jnp.tanh(h)


def _ada_kernel(sc_ref, w_ref, b_ref, o_ref):
    o_ref[...] = jnp.dot(sc_ref[...], w_ref[...], preferred_element_type=F32,
                         precision=lax.Precision.HIGHEST) + b_ref[...]


def _ada_mod(sc, ada_w, ada_b):
    depth, d, n = ada_w.shape
    rows = sc.shape[0]
    tn = 1024
    return pl.pallas_call(
        _ada_kernel,
        grid=(depth, n // tn),
        in_specs=[pl.BlockSpec((rows, d), lambda l, j: (0, 0)),
                  pl.BlockSpec((None, d, tn), lambda l, j: (l, 0, j)),
                  pl.BlockSpec((None, 1, tn), lambda l, j: (l, 0, j))],
        out_specs=pl.BlockSpec((None, rows, tn), lambda l, j: (l, 0, j)),
        out_shape=jax.ShapeDtypeStruct((depth, rows, n), F32),
        compiler_params=_params("parallel", "parallel"),
        name="ada_mod",
    )(sc, ada_w, ada_b.reshape(depth, 1, n))


def _mod_specs(b, n_ctx_tiles, sub, off=0):
    def spec(k):
        return pl.BlockSpec((None, SUBLANES, D_MODEL),
                            lambda i, j: (jnp.where(sub * j + k + off < n_ctx_tiles, b, i), 0, 0))
    return [spec(k) for k in range(sub)]


def _rope(x, cos, sin_a, sin_b):
    return x * cos + pltpu.roll(x, LANES - 1, 1) * sin_a + pltpu.roll(x, 1, 1) * sin_b


def _inproj_kernel(*refs, sub, attn):
    h_ref = refs[0]
    sh_refs, sc_refs = refs[1:1 + sub], refs[1 + sub:1 + 2 * sub]
    nw_ref, w_ref = refs[1 + 2 * sub:3 + 2 * sub]
    rest = refs[3 + 2 * sub:]
    if attn:
        cos_ref, sa_ref, sb_ref, p_ref, u_ref, kv_ref, vt_ref, u_scr = rest
    else:
        p_ref, u_scr = rest
    d = h_ref.shape[-1]
    r = ROW_TILE
    for k in range(sub):
        x = h_ref[k * r:(k + 1) * r, :]
        y = x * lax.rsqrt(jnp.mean(x * x, axis=-1, keepdims=True) + RMS_EPS) * nw_ref[...]
        y3 = y.reshape(r // SUBLANES, SUBLANES, d)
        u = (y3 * (1.0 + sc_refs[k][...])[None] + sh_refs[k][...][None]).reshape(r, d)
        u_scr[k * r:(k + 1) * r, :] = u.astype(BF16)
    n = p_ref.shape[-1]
    p_ref[...] = jnp.dot(u_scr[...], w_ref[:, :n], preferred_element_type=F32)
    if attn:
        u_ref[...] = u_scr[...]
        kv = jnp.dot(u_scr[...], w_ref[:, n:], preferred_element_type=F32)
        cos, sa, sb = cos_ref[...], sa_ref[...], sb_ref[...]
        dh = B_HEAD_DIM
        for g in range(B_KV_HEADS):
            sl = slice(g * LANES, (g + 1) * LANES)
            x = _rope(kv[:, sl], cos, sa, sb)
            kv_ref[:, sl] = x.astype(BF16)
            vt_ref[g * dh:(g + 1) * dh, :] = x.T[dh:, :].astype(BF16)


def _inproj(h, shift, scale, norm_w, w, n_ctx_tiles, kv_tabs=None):
    b, t, d = h.shape
    attn = kv_tabs is not None
    wkv = 2 * B_KV_WIDTH
    n = w.shape[1] - (wkv if attn else 0)
    sub = PROJ_SUB
    tm = sub * ROW_TILE
    row = lambda width: pl.BlockSpec((None, tm, width), lambda i, j: (i, j, 0))
    in_specs = ([row(d)] + _mod_specs(b, n_ctx_tiles, sub) + _mod_specs(b, n_ctx_tiles, sub)
                + [pl.BlockSpec((1, d), lambda i, j: (0, 0)),
                   pl.BlockSpec(w.shape, lambda i, j: (0, 0), pipeline_mode=pl.Buffered(1))])
    args = [h] + [shift] * sub + [scale] * sub + [norm_w.reshape(1, d), w]
    out_shape = [jax.ShapeDtypeStruct((b, t, n), F32)]
    out_specs = [row(n)]
    if attn:
        in_specs += [pl.BlockSpec((tm, LANES), lambda i, j: (j, 0))] * 3
        args += list(kv_tabs)
        out_shape += [jax.ShapeDtypeStruct((b, t, d), BF16), jax.ShapeDtypeStruct((b, t, wkv), BF16),
                      jax.ShapeDtypeStruct((b, B_KV_WIDTH, t), BF16)]
        out_specs += [row(d), row(wkv), pl.BlockSpec((None, B_KV_WIDTH, tm), lambda i, j: (i, 0, j))]
    res = pl.pallas_call(
        functools.partial(_inproj_kernel, sub=sub, attn=attn),
        grid=(b, t // tm),
        in_specs=in_specs,
        out_specs=out_specs,
        out_shape=out_shape,
        scratch_shapes=[pltpu.VMEM((tm, d), BF16)],
        compiler_params=_params("parallel", "parallel"),
        name="inproj",
    )(*args)
    return res if attn else res[0]


def _matmul_kernel(u_ref, w_ref, p_ref):
    p_ref[...] = jnp.dot(u_ref[...], w_ref[...], preferred_element_type=F32)


def _matmul(u, w):
    r, d = u.shape
    n = w.shape[1]
    tm = 4 * ROW_TILE
    return pl.pallas_call(
        _matmul_kernel,
        grid=(r // tm,),
        in_specs=[pl.BlockSpec((tm, d), lambda i: (i, 0)),
                  pl.BlockSpec((d, n), lambda i: (0, 0), pipeline_mode=pl.Buffered(1))],
        out_specs=pl.BlockSpec((tm, n), lambda i: (i, 0)),
        out_shape=jax.ShapeDtypeStruct((r, n), F32),
        compiler_params=_params("parallel"),
        name="matmul_tm",
    )(u, w)


def _scan_tile(i, n_ctx, n_lat, rev):
    if not rev:
        return i
    return jnp.where(i < n_ctx, n_ctx - 1 - i, 2 * n_ctx + n_lat - 1 - i)


def _rglru_kernel(*refs, rev, tt, n_ctx, n_lat):
    if rev:
        (xa_ref, xp_ref, xn1_ref, xn2_ref, ga_ref, yf_ref, cw_ref, cb_ref, w_ref, bx_ref, ba_ref,
         la_ref, out_ref, h_scr, xpad_scr, u_scr, a_scr, b_scr) = refs
    else:
        (xa_ref, xp_ref, xn1_ref, xn2_ref, cw_ref, cb_ref, w_ref, bx_ref, ba_ref,
         la_ref, out_ref, h_scr, xpad_scr, u_scr, a_scr, b_scr) = refs
    i = pl.program_id(0)
    tile = _scan_tile(i, n_ctx, n_lat, rev)
    seg_first = jnp.logical_or(tile == 0, tile == n_ctx)
    seg_last = jnp.logical_or(tile == n_ctx - 1, tile == n_ctx + n_lat - 1)
    rows = tt * SUBLANES

    @pl.when(i == 0)
    def _():
        h_scr[...] = jnp.zeros_like(h_scr)

    xpad_scr[0:8, :] = jnp.where(seg_first, 0.0, xp_ref[...])
    xpad_scr[8:8 + rows, :] = xa_ref[...]
    xpad_scr[8 + rows:16 + rows, :] = jnp.where(seg_last, 0.0, xn1_ref[...])
    xpad_scr[16 + rows:24 + rows, :] = jnp.where(seg_last, 0.0, xn2_ref[...])
    u = cb_ref[...] + xpad_scr[0:rows, :] * cw_ref[0:1, :]
    for k in range(1, A_CONV):
        u = u + xpad_scr[8 * k:8 * k + rows, :] * cw_ref[k:k + 1, :]
    u_scr[...] = u

    for hd in range(A_HEADS):
        sl = slice(hd * A_BLOCK, (hd + 1) * A_BLOCK)
        uh = u_scr[:, sl]
        g = jnp.dot(uh.astype(BF16), w_ref[hd], preferred_element_type=F32)
        gate_x = 0.5 * jnp.tanh(g[:, :A_BLOCK] + bx_ref[:, sl]) + 0.5
        lah = la_ref[:, sl]
        log_a = lah + lah * jnp.tanh(g[:, A_BLOCK:] + ba_ref[:, sl])
        a = jnp.exp(log_a)
        a_scr[:, sl] = a
        b_scr[:, sl] = jnp.sqrt(-jnp.tanh(log_a) * (1.0 + a * a)) * gate_x * uh

    def step(s, h):
        t = (tt - 1 - s) if rev else s
        r0 = pl.multiple_of(t * SUBLANES, SUBLANES)
        h = a_scr[pl.ds(r0, SUBLANES), :] * h + b_scr[pl.ds(r0, SUBLANES), :]
        b_scr[pl.ds(r0, SUBLANES), :] = h
        return h

    h_scr[...] = lax.fori_loop(0, tt, step, h_scr[...], unroll=8)
    if rev:
        out_ref[...] = ((yf_ref[...] + b_scr[...]) * _silu(ga_ref[...])).astype(BF16)
    else:
        out_ref[...] = b_scr[...]


def _rglru_pass(p_a, yf, conv_w, conv_b, w_cat, bx, ba, la, t_ctx, t_lat, rev):
    w = A_WIDTH
    tt = 64
    rows = tt * SUBLANES
    n_ctx, n_lat = t_ctx // tt, t_lat // tt
    t_all = t_ctx + t_lat
    tile = functools.partial(_scan_tile, n_ctx=n_ctx, n_lat=n_lat, rev=rev)
    const = lambda i: (0, 0)
    in_specs = [pl.BlockSpec((rows, w), lambda i: (tile(i), 0)),
                pl.BlockSpec((SUBLANES, w), lambda i: (jnp.maximum(tile(i) * tt - 1, 0), 0)),
                pl.BlockSpec((SUBLANES, w), lambda i: (jnp.minimum(tile(i) * tt + tt, t_all - 1), 0)),
                pl.BlockSpec((SUBLANES, w), lambda i: (jnp.minimum(tile(i) * tt + tt + 1, t_all - 1), 0))]
    args = [p_a, p_a, p_a, p_a]
    if rev:
        in_specs += [pl.BlockSpec((rows, w), lambda i: (tile(i), 1)),
                     pl.BlockSpec((rows, w), lambda i: (tile(i), 0))]
        args += [p_a, yf]
    in_specs += [pl.BlockSpec((A_CONV, w), const), pl.BlockSpec((1, w), const),
                 pl.BlockSpec((A_HEADS, A_BLOCK, 2 * A_BLOCK), lambda i: (0, 0, 0)),
                 pl.BlockSpec((1, w), const), pl.BlockSpec((1, w), const), pl.BlockSpec((1, w), const)]
    args += [conv_w, conv_b, w_cat, bx, ba, la]
    return pl.pallas_call(
        functools.partial(_rglru_kernel, rev=rev, tt=tt, n_ctx=n_ctx, n_lat=n_lat),
        grid=(n_ctx + n_lat,),
        in_specs=in_specs,
        out_specs=pl.BlockSpec((rows, w), lambda i: (tile(i), 0)),
        out_shape=jax.ShapeDtypeStruct((t_all * SUBLANES, w), BF16 if rev else F32),
        scratch_shapes=[pltpu.VMEM((SUBLANES, w), F32),
                        pltpu.VMEM((rows + 3 * SUBLANES, w), F32),
                        pltpu.VMEM((rows, w), F32),
                        pltpu.VMEM((rows, w), F32),
                        pltpu.VMEM((rows, w), F32)],
        compiler_params=_params("arbitrary"),
        name="rglru_bwd" if rev else "rglru_fwd",
    )(*args)


def _attn_kernel(sink_ref, q_ref, kc_ref, kp_ref, ko_ref, kn_ref, vc_ref, vp_ref, vo_ref, vn_ref, gb_ref,
                 cos_ref, sa_ref, sb_ref, o_ref, s_scr, p_scr, bias_scr, *, n_ctx_blk, n_lat_blk):
    n = pl.program_id(1)
    is_lat = n >= n_ctx_blk

    @pl.when(is_lat)
    def _():
        _attend(True, n - n_ctx_blk, n_lat_blk, sink_ref, q_ref, (kc_ref, kp_ref, ko_ref, kn_ref),
                (vc_ref, vp_ref, vo_ref, vn_ref), gb_ref, cos_ref, sa_ref, sb_ref, o_ref, s_scr, p_scr, bias_scr)

    @pl.when(jnp.logical_not(is_lat))
    def _():
        _attend(False, None, None, sink_ref, q_ref, (kc_ref,), (vc_ref,), gb_ref, cos_ref, sa_ref, sb_ref,
                o_ref, s_scr, p_scr, bias_scr)


def _attend(window, m, n_lat_blk, sink_ref, q_ref, k_refs, v_refs, gb_ref, cos_ref, sa_ref, sb_ref,
            o_ref, s_scr, p_scr, bias_scr):
    w = WINDOW
    dh = B_HEAD_DIM
    lc = k_refs[0].shape[0]

    cos, sa, sb = cos_ref[...], sa_ref[...], sb_ref[...]
    low = lax.broadcasted_iota(jnp.int32, (w, LANES), 1) < dh
    segs = [(0, lc, None)]
    if window:
        kj = lax.broadcasted_iota(jnp.int32, (w, w), 0)
        qi = lax.broadcasted_iota(jnp.int32, (w, w), 1)
        neg = jnp.full((w, w), NEG_BIG, F32)
        zer = jnp.zeros((w, w), F32)
        bias_scr[0] = jnp.where(jnp.logical_and(m >= 1, kj >= qi), zer, neg)
        bias_scr[1] = jnp.where(jnp.logical_and(m <= n_lat_blk - 2, kj <= qi), zer, neg)
        segs += [(lc, w, 0), (lc + w, w, None), (lc + 2 * w, w, 1)]
    nk = segs[-1][0] + segs[-1][1]
    piece = 32
    pieces = [(r0, bi, p0) for r0, nr, bi in segs for p0 in range(0, nr, piece)]

    def scores(g, cb, r0, bi, p0):
        x = s_scr[g, r0 + p0:r0 + p0 + piece, cb]
        return x if bi is None else x + bias_scr[bi, p0:p0 + piece, :]

    def tree(xs, op):
        while len(xs) > 1:
            xs = [op(xs[i], xs[i + 1]) for i in range(0, len(xs) - 1, 2)] + ([xs[-1]] if len(xs) % 2 else [])
        return xs[0]

    def fold8(x, op):
        return op(x.reshape(x.shape[0] // SUBLANES, SUBLANES, x.shape[1]), axis=0)

    def score_stage(g):
        parts = []
        for c in range(2):
            col0 = g * B_GROUP * dh + c * LANES
            qc = _rope(q_ref[:, col0:col0 + LANES], cos, sa, sb) * (ATTN_SCALE * LOG2E)
            parts.append(jnp.where(low, qc, 0.0).astype(BF16))
            parts.append(jnp.where(low, pltpu.roll(qc, dh, 1), 0.0).astype(BF16))
        qs = jnp.concatenate(parts, axis=0)
        ks = slice(g * LANES, (g + 1) * LANES)
        kv_all = jnp.concatenate([r[:, ks] for r in k_refs], axis=0)
        s_scr[g, 0:nk, :] = lax.dot_general(kv_all, qs, _NT, preferred_element_type=F32)

    def softmax_stage(g):
        rdens = []
        for h in range(B_GROUP):
            cb = slice(h * w, (h + 1) * w)
            snk = sink_ref[g * B_GROUP + h] * LOG2E
            mx8 = tree([fold8(scores(g, cb, r0, bi, p0), jnp.max) for r0, bi, p0 in pieces], jnp.maximum)
            mx = jnp.maximum(jnp.max(mx8, axis=0, keepdims=True), snk)
            sums = []
            for r0, bi, p0 in pieces:
                p = jnp.exp2(scores(g, cb, r0, bi, p0) - mx)
                sums.append(fold8(p, jnp.sum))
                p_scr[g, r0 + p0:r0 + p0 + piece, cb] = p.astype(BF16)
            den8 = tree(sums, jnp.add)
            rdens.append(1.0 / (jnp.sum(den8, axis=0, keepdims=True) + jnp.exp2(snk - mx)))
        return rdens

    def value_stage(g, rdens):
        vt_all = jnp.concatenate([r[g * dh:(g + 1) * dh, :] for r in v_refs], axis=1)
        ov = jnp.dot(vt_all, p_scr[g, 0:nk, :], preferred_element_type=F32) * jnp.concatenate(rdens, axis=1)
        for c in range(2):
            pair = jnp.concatenate([ov[:, (2 * c) * w:(2 * c + 1) * w], ov[:, (2 * c + 1) * w:(2 * c + 2) * w]],
                                   axis=0)
            col0 = g * B_GROUP * dh + c * LANES
            y = pair.T * _silu(gb_ref[:, col0:col0 + LANES])
            o_ref[:, col0:col0 + LANES] = y.astype(BF16)

    rd = {}
    for step in range(B_KV_HEADS + 2):
        if step < B_KV_HEADS:
            score_stage(step)
        if 0 <= step - 1 < B_KV_HEADS:
            rd[step - 1] = softmax_stage(step - 1)
        if 0 <= step - 2 < B_KV_HEADS:
            value_stage(step - 2, rd[step - 2])


def _attention(p_b, kvr, vt, sink, q_tabs, t_ctx, gate_col_block):
    b, t, _ = p_b.shape
    w = WINDOW
    n_ctx_blk, n_lat_blk = t_ctx // w, (t - t_ctx) // w
    nb = t // w
    wkv = 2 * B_KV_WIDTH
    prev = lambda n: jnp.maximum(n - 1, 0)
    nxt = lambda n: jnp.minimum(n + 1, nb - 1)
    kvb = lambda f: pl.BlockSpec((None, w, wkv), lambda i, n, s: (i, f(n), 0))
    vtb = lambda f: pl.BlockSpec((None, B_KV_WIDTH, w), lambda i, n, s: (i, 0, f(n)))
    tab = pl.BlockSpec((w, LANES), lambda i, n, s: (n, 0))
    grid_spec = pltpu.PrefetchScalarGridSpec(
        num_scalar_prefetch=1,
        grid=(b, nb),
        in_specs=[pl.BlockSpec((None, w, B_WIDTH), lambda i, n, s: (i, n, 0)),
                  pl.BlockSpec((None, t_ctx, wkv), lambda i, n, s: (i, 0, 0)),
                  kvb(prev), kvb(lambda n: n), kvb(nxt),
                  pl.BlockSpec((None, B_KV_WIDTH, t_ctx), lambda i, n, s: (i, 0, 0)),
                  vtb(prev), vtb(lambda n: n), vtb(nxt),
                  pl.BlockSpec((None, w, B_WIDTH), lambda i, n, s: (i, n, gate_col_block)),
                  tab, tab, tab],
        out_specs=pl.BlockSpec((None, w, B_WIDTH), lambda i, n, s: (i, n, 0)),
        scratch_shapes=[pltpu.VMEM((B_KV_HEADS, t_ctx + 3 * w, B_GROUP * w), F32),
                        pltpu.VMEM((B_KV_HEADS, t_ctx + 3 * w, B_GROUP * w), BF16),
                        pltpu.VMEM((2, w, w), F32)],
    )
    return pl.pallas_call(
        functools.partial(_attn_kernel, n_ctx_blk=n_ctx_blk, n_lat_blk=n_lat_blk),
        grid_spec=grid_spec,
        out_shape=jax.ShapeDtypeStruct((b, t, B_WIDTH), BF16),
        compiler_params=_params("parallel", "parallel"),
        name="window_attn",
    )(sink, p_b, kvr, kvr, kvr, kvr, vt, vt, vt, vt, p_b, *q_tabs)


GLA_DIAG = 4
GLA_LEVELS = (4, 8, 16, 32)
GLA_EXP_KINDS = 2 + len(GLA_LEVELS)
GLA_UNROLL = 2


def _gla_region_ids(rev):
    c = C_CHUNK
    ri = lax.broadcasted_iota(jnp.int32, (c, c), 0)
    ci = lax.broadcasted_iota(jnp.int32, (c, c), 1)
    reg = jnp.full((c, c), len(GLA_LEVELS), jnp.int32)
    for li in range(len(GLA_LEVELS) - 1, -1, -1):
        m = GLA_LEVELS[li]
        reg = jnp.where(ri // m == ci // m, li, reg)
    seen = (ci >= ri) if rev else (ci <= ri)
    return jnp.where(seen, reg, -1)


def _gla_exponent_matrix(rev):
    c = C_CHUNK
    t = np.arange(c)[:, None]
    s = np.arange(c)[None, :]
    upto = (s >= t) if rev else (s <= t)
    after = ~upto
    mats = [upto, after]
    for m in GLA_LEVELS:
        later = ((t // m) % 2 == 0) if rev else ((t // m) % 2 == 1)
        mats.append((t // m == s // m) & np.where(later, upto, after))
    mat = np.concatenate(mats, axis=0).astype(np.float32)
    return np.concatenate([mat, mat], axis=1)


def _grp(x, j):
    return x[j * SUBLANES:(j + 1) * SUBLANES, :]


def _gla_stage_a1(z, qr, lb, emat, rev, lane):
    c = C_CHUNK
    ng = c // SUBLANES

    half = 0.5 * (1.0 - lb)
    ht = half * jnp.tanh(z)
    f = (lb + half) + ht
    kk = half - ht
    ft = jnp.maximum(f, F_MIN)
    q = _silu(qr)

    g2 = jnp.log2(ft)
    g_hi = g2.astype(BF16)
    g_lo = (g2 - g_hi.astype(F32)).astype(BF16)
    expo = jnp.dot(emat, jnp.concatenate([g_hi, g_lo], axis=0), preferred_element_type=F32)

    dparts = []
    for j in range(ng):
        fg, qg, kg = _grp(ft, j), _grp(q, j), _grp(kk, j)
        w_ = kg
        acc = jnp.where(lane == 0, jnp.sum(qg * kg, axis=-1, keepdims=True), 0.0)
        for dlt in range(1, GLA_DIAG):
            w_ = fg * pltpu.roll(w_, (SUBLANES - 1) if rev else 1, 0)
            red = jnp.sum(qg * w_, axis=-1, keepdims=True)
            acc = jnp.where(lane == (dlt if rev else LANES - dlt), red, acc)
        dparts.append(acc)
    diag = pltpu.roll(jnp.concatenate(dparts, axis=0), 0, 1, stride=1, stride_axis=0)[:, :c]
    return q, kk, expo, diag


def _gla_stage_a2(q, kk, expo, diag, v, rev, reg, row8):
    c = C_CHUNK
    ng = c // SUBLANES
    dec = jnp.exp2(expo)
    last = 0 if rev else c - 1
    total = dec[last:last + 1, :]
    qd = (q * dec[0:c]).astype(BF16)
    kd = (kk * dec[c:2 * c]).astype(BF16)
    grp = _grp

    zero = jnp.zeros((SUBLANES, LANES), F32)
    levels = []
    for li, m in enumerate(GLA_LEVELS):
        fac = dec[(2 + li) * c:(3 + li) * c]
        qparts, kparts = [], []
        for j in range(ng):
            if m >= SUBLANES:
                later = ((j * SUBLANES // m) % 2 == 1) != rev
                prod = (grp(q, j) if later else grp(kk, j)) * grp(fac, j)
                qparts.append(prod if later else zero)
                kparts.append(zero if later else prod)
            else:
                later = ((row8 // m) % 2 == 0) if rev else ((row8 // m) % 2 == 1)
                prod = jnp.where(later, grp(q, j), grp(kk, j)) * grp(fac, j)
                qparts.append(jnp.where(later, prod, 0.0))
                kparts.append(jnp.where(later, 0.0, prod))
        levels.append(lax.dot_general(jnp.concatenate(qparts, axis=0).astype(BF16),
                                      jnp.concatenate(kparts, axis=0).astype(BF16), _NT,
                                      preferred_element_type=F32))

    attn = jnp.zeros((c, c), F32)
    for li in range(len(GLA_LEVELS) - 1, -1, -1):
        attn = jnp.where(reg == li + 1, levels[li], attn)
    attn = jnp.where(reg == 0, diag, attn)
    return attn.astype(BF16), qd, kd, v.astype(BF16), total


def _gla_stage_b(attn, qd, kd, vb, total, st):
    o = jnp.dot(attn, vb, preferred_element_type=F32)
    o = o + lax.dot_general(qd, st.astype(BF16), _NT, preferred_element_type=F32)
    st_new = total * st + lax.dot_general(vb, kd, _TN, preferred_element_type=F32)
    return o, st_new


def _gla_kernel(zf_ref, zb_ref, v_ref, q_ref, og_ref, lb_ref, gw_ref, emat_ref, y_ref, of_scr, ob_scr, st_scr,
                attn_c, qd_c, kd_c, vb_c, tot_c, q_c, k_c, expo_c, diag_c,
                *, n_ctx, n_lat, heads):
    c = C_CHUNK
    row8 = lax.broadcasted_iota(jnp.int32, (SUBLANES, LANES), 0)
    lane = lax.broadcasted_iota(jnp.int32, (SUBLANES, LANES), 1)
    regs = (_gla_region_ids(False), _gla_region_ids(True))
    n_all = n_ctx + n_lat
    st_scr[...] = jnp.zeros_like(st_scr)
    streams = [(hh, rev) for hh in range(heads) for rev in (False, True)]

    def rows_of(i, rev):
        if isinstance(i, int):
            cidx = ((n_ctx - 1 - i) if i < n_ctx else (2 * n_ctx + n_lat - 1 - i)) if rev else i
            return pl.ds(cidx * c, c)
        cidx = jnp.where(i < n_ctx, n_ctx - 1 - i, 2 * n_ctx + n_lat - 1 - i) if rev else i
        return pl.ds(pl.multiple_of(cidx * c, c), c)

    def stage_a1(i):
        for k, (hh, rev) in enumerate(streams):
            ls = slice(hh * LANES, (hh + 1) * LANES)
            rows = rows_of(i, rev)
            z_ref = zb_ref if rev else zf_ref
            q, kk, expo, diag = _gla_stage_a1(z_ref[rows, ls], q_ref[rows, ls], lb_ref[:, ls],
                                              emat_ref[int(rev)], rev, lane)
            q_c[k] = q
            k_c[k] = kk
            expo_c[k] = expo
            diag_c[k] = diag

    def stage_a2(i):
        for k, (hh, rev) in enumerate(streams):
            ls = slice(hh * LANES, (hh + 1) * LANES)
            attn, qd, kd, vb, total = _gla_stage_a2(q_c[k], k_c[k], expo_c[k], diag_c[k],
                                                    v_ref[rows_of(i, rev), ls], rev, regs[int(rev)], row8)
            attn_c[k] = attn
            qd_c[k] = qd
            kd_c[k] = kd
            vb_c[k] = vb
            tot_c[k] = jnp.broadcast_to(total, (SUBLANES, LANES))

    def stage_b(i):
        for k, (hh, rev) in enumerate(streams):
            ls = slice(hh * LANES, (hh + 1) * LANES)
            o, st = _gla_stage_b(attn_c[k], qd_c[k], kd_c[k], vb_c[k], tot_c[k, 0:1, :], st_scr[k])
            (ob_scr if rev else of_scr)[rows_of(i, rev), ls] = o
            st_scr[k] = st

    stage_a1(0)
    stage_a2(0)
    stage_a1(1)

    def step(i):
        stage_b(i - 2)
        stage_a2(i - 1)
        stage_a1(i)

    def body(j, carry):
        for r in range(GLA_UNROLL):
            step(2 + GLA_UNROLL * j + r)
        return carry

    trips = (n_all - 2) // GLA_UNROLL
    lax.fori_loop(0, trips, body, 0)
    for i in range(2 + trips * GLA_UNROLL, n_all):
        step(i)
    stage_b(n_all - 2)
    stage_a2(n_all - 1)
    stage_b(n_all - 1)

    gw = gw_ref[...]
    piece = 256

    def fin(i, carry):
        rows = pl.ds(pl.multiple_of(i * piece, piece), piece)
        for hh in range(heads):
            ls = slice(hh * LANES, (hh + 1) * LANES)
            o = of_scr[rows, ls] + ob_scr[rows, ls]
            y = o * lax.rsqrt(jnp.mean(o * o, axis=-1, keepdims=True) + RMS_EPS) * gw
            y_ref[rows, ls] = (y * _silu(og_ref[rows, ls])).astype(BF16)
        return carry

    lax.fori_loop(0, (n_all * c) // piece, fin, 0)


def _gla(p, lb, gw, t_ctx):
    b, t, _ = p.shape
    hps = GLA_HEADS_PER_STEP
    wb = hps * LANES
    nblk = C_HEADS // hps
    col = lambda k: pl.BlockSpec((None, t, wb), lambda i, h: (i, 0, k * nblk + h))
    emat = jnp.asarray(np.stack([_gla_exponent_matrix(False), _gla_exponent_matrix(True)]), BF16)
    return pl.pallas_call(
        functools.partial(_gla_kernel, n_ctx=t_ctx // C_CHUNK, n_lat=(t - t_ctx) // C_CHUNK, heads=hps),
        grid=(b, nblk),
        in_specs=[col(0), col(1), col(2), col(3), col(4),
                  pl.BlockSpec((None, 1, wb), lambda i, h: (h, 0, 0)),
                  pl.BlockSpec((1, LANES), lambda i, h: (0, 0)),
                  pl.BlockSpec(emat.shape, lambda i, h: (0, 0, 0))],
        out_specs=pl.BlockSpec((None, t, wb), lambda i, h: (i, 0, h)),
        out_shape=jax.ShapeDtypeStruct((b, t, C_HEADS * LANES), BF16),
        scratch_shapes=[pltpu.VMEM((t, wb), F32), pltpu.VMEM((t, wb), F32),
                        pltpu.VMEM((2 * hps, C_EXPAND, LANES), F32),
                        pltpu.VMEM((2 * hps, C_CHUNK, C_CHUNK), BF16),
                        pltpu.VMEM((2 * hps, C_CHUNK, LANES), BF16),
                        pltpu.VMEM((2 * hps, C_CHUNK, LANES), BF16),
                        pltpu.VMEM((2 * hps, C_CHUNK, LANES), BF16),
                        pltpu.VMEM((2 * hps, SUBLANES, LANES), F32),
                        pltpu.VMEM((2 * hps, C_CHUNK, LANES), F32),
                        pltpu.VMEM((2 * hps, C_CHUNK, LANES), F32),
                        pltpu.VMEM((2 * hps, GLA_EXP_KINDS * C_CHUNK, LANES), F32),
                        pltpu.VMEM((2 * hps, C_CHUNK, C_CHUNK), F32)],
        compiler_params=_params("parallel", "parallel"),
        name="hgrn2_gla",
    )(p, p, p, p, p, lb.reshape(nblk, 1, wb), gw.reshape(1, LANES), emat)


def _outproj_kernel(*refs, n_in, sub, final):
    y_refs = refs[:n_in]
    w_ref, h_ref = refs[n_in:n_in + 2]
    gt_refs = refs[n_in + 2:n_in + 2 + sub]
    if final:
        fw_ref, o_ref = refs[n_in + 2 + sub:]
    else:
        (o_ref,) = refs[n_in + 2 + sub:]
    acc = None
    k0 = 0
    for y_ref in y_refs:
        kw = y_ref.shape[-1]
        part = jnp.dot(y_ref[...], w_ref[k0:k0 + kw, :], preferred_element_type=F32)
        acc = part if acc is None else acc + part
        k0 += kw
    d = acc.shape[-1]
    r = ROW_TILE
    for k in range(sub):
        a3 = acc[k * r:(k + 1) * r, :].reshape(r // SUBLANES, SUBLANES, d)
        hn = h_ref[k * r:(k + 1) * r, :] + (a3 * gt_refs[k][...][None]).reshape(r, d)
        if final:
            hn = hn * lax.rsqrt(jnp.mean(hn * hn, axis=-1, keepdims=True) + RMS_EPS) * fw_ref[...]
        o_ref[k * r:(k + 1) * r, :] = hn


def _outproj(ys, w, h, gate, n_ctx_tiles, final_w=None):
    b, t, d = h.shape
    final = final_w is not None
    sub = 1 if final else PROJ_SUB
    tm = sub * ROW_TILE
    off = n_ctx_tiles if final else 0
    nt = t // tm - off
    in_specs = [pl.BlockSpec((None, tm, y.shape[-1]), lambda i, j: (i, j + off, 0)) for y in ys]
    in_specs += [pl.BlockSpec(w.shape, lambda i, j: (0, 0), pipeline_mode=pl.Buffered(1)),
                 pl.BlockSpec((None, tm, d), lambda i, j: (i, j + off, 0))]
    in_specs += _mod_specs(b, n_ctx_tiles, sub, off)
    args = list(ys) + [w, h] + [gate] * sub
    if final:
        in_specs.append(pl.BlockSpec((1, d), lambda i, j: (0, 0)))
        args.append(final_w.reshape(1, d))
    return pl.pallas_call(
        functools.partial(_outproj_kernel, n_in=len(ys), sub=sub, final=final),
        grid=(b, nt),
        in_specs=in_specs,
        out_specs=pl.BlockSpec((None, tm, d), lambda i, j: (i, j, 0)),
        out_shape=jax.ShapeDtypeStruct((b, nt * tm, d), F32),
        compiler_params=_params("parallel", "parallel"),
        name="outproj",
    )(*args)


def _rope_tables(s_lat, t_ctx):
    pos = np.arange(s_lat)
    row = (pos // GRID_W).astype(np.float32)
    colp = (pos % GRID_W).astype(np.float32)
    axis_dim = B_HEAD_DIM // 2
    inv = jnp.asarray(ROPE_BASE, F32) ** (-jnp.arange(0, axis_dim, 2, dtype=F32) / axis_dim)
    ang = jnp.concatenate([jnp.asarray(row)[:, None] * inv, jnp.asarray(colp)[:, None] * inv], axis=-1)
    cos, sin = jnp.cos(ang), jnp.sin(ang)
    zer = jnp.zeros_like(sin)
    pairs = lambda even, odd: jnp.stack([even, odd], axis=-1).reshape(s_lat, B_HEAD_DIM)
    cos64 = pairs(cos, cos)
    sa64 = pairs(-sin, zer)
    sb64 = pairs(zer, sin)
    one64 = jnp.ones_like(cos64)
    zer64 = jnp.zeros_like(cos64)

    def full(lat128, ident):
        return jnp.concatenate([jnp.broadcast_to(ident, (t_ctx, LANES)), lat128], axis=0)

    ident_c = jnp.ones((1, LANES), F32)
    ident_s = jnp.zeros((1, LANES), F32)
    q_tabs = (full(jnp.concatenate([cos64, cos64], -1), ident_c),
              full(jnp.concatenate([sa64, sa64], -1), ident_s),
              full(jnp.concatenate([sb64, sb64], -1), ident_s))
    kv_tabs = (full(jnp.concatenate([cos64, one64], -1), ident_c),
               full(jnp.concatenate([sa64, zer64], -1), ident_s),
               full(jnp.concatenate([sb64, zer64], -1), ident_s))
    return q_tabs, kv_tabs


def _even_weights(w_in):
    dh = B_HEAD_DIM
    n_state = A_WIDTH + 2 * B_KV_WIDTH
    q0 = n_state + A_WIDTH
    k0 = A_WIDTH
    v0 = A_WIDTH + B_KV_WIDTH
    parts = [w_in[:, q0:q0 + 2 * B_WIDTH]]
    for j in range(B_KV_HEADS):
        parts += [w_in[:, k0 + j * dh:k0 + (j + 1) * dh], w_in[:, v0 + j * dh:v0 + (j + 1) * dh]]
    w_b = jnp.concatenate(parts, axis=1).astype(BF16)
    w_a = jnp.concatenate([w_in[:, :A_WIDTH], w_in[:, n_state:n_state + A_WIDTH]], axis=1).astype(BF16)
    return w_b, w_a


def _groups(vec_b, vec_c):
    allv = jnp.concatenate([vec_b, vec_c[None, :]], axis=0)
    return jnp.broadcast_to(allv[:, None, :], (allv.shape[0], SUBLANES, allv.shape[1]))


def kernel(x, c, ctx, c_ctx, ada_w, ada_b, norm_w, ev_w_in, ev_conv_w, ev_conv_b, ev_rg_wx, ev_rg_bx,
           ev_rg_wa, ev_rg_ba, ev_rg_lambda, ev_sink, ev_w_out, od_w_in, od_lb_raw, od_gnorm_w, od_w_out,
           final_norm_w):
    b, s_lat, d = x.shape
    t_ctx = ctx.shape[1]
    t_all = t_ctx + s_lat
    assert b == SUBLANES and d == D_MODEL
    assert t_ctx % ROW_TILE == 0 and s_lat % ROW_TILE == 0 and t_all % (PROJ_SUB * ROW_TILE) == 0
    n_ctx_tiles = t_ctx // ROW_TILE

    sc_rows = jnp.zeros((2 * SUBLANES, d), F32)
    sc_rows = sc_rows.at[:b].set(jax.nn.silu(c)).at[b].set(jax.nn.silu(c_ctx))
    mod = _ada_mod(sc_rows, ada_w, ada_b)

    lb_p = jax.nn.softmax(od_lb_raw.astype(F32), axis=0)
    lower_bounds = jnp.cumsum(lb_p, axis=0) - lb_p[0]
    q_tabs, kv_tabs = _rope_tables(s_lat, t_ctx)

    h = jnp.concatenate([ctx, x], axis=1)
    out = None
    for layer in range(DEPTH):
        last = layer == DEPTH - 1
        j = layer // 2
        ml = mod[layer]
        shift = _groups(ml[:b, :d], ml[b, :d])
        scale = _groups(ml[:b, d:2 * d], ml[b, d:2 * d])
        gate = _groups(ml[:b, 2 * d:], ml[b, 2 * d:])
        if layer % 2 == 0:
            w_b, w_a = _even_weights(ev_w_in[j])
            p_b, u, kvr, vt = _inproj(h, shift, scale, norm_w[layer], w_b, n_ctx_tiles, kv_tabs=kv_tabs)
            u_tm = jnp.transpose(u, (1, 0, 2)).reshape(t_all * b, d)
            p_a = _matmul(u_tm, w_a)
            lah = -0.5 * A_C * jax.nn.softplus(-ev_rg_lambda[j].astype(F32))
            w_cat = (0.5 * jnp.concatenate([ev_rg_wx[j], ev_rg_wa[j]], axis=-1)).astype(BF16)
            rg = lambda dr, yf: _rglru_pass(
                p_a, yf, ev_conv_w[j], ev_conv_b[j].reshape(1, -1), w_cat[dr],
                0.5 * ev_rg_bx[j, dr].reshape(1, -1), 0.5 * ev_rg_ba[j, dr].reshape(1, -1), lah[dr].reshape(1, -1),
                t_ctx, s_lat, rev=bool(dr))
            ya = jnp.transpose(rg(1, rg(0, None)).reshape(t_all, b, A_WIDTH), (1, 0, 2))
            yb = _attention(p_b, kvr, vt, ev_sink[j].astype(F32), q_tabs, t_ctx, gate_col_block=1)
            ys, w_out = [ya, yb], ev_w_out[j].astype(BF16)
        else:
            w_odd = od_w_in[j].at[:, :2 * C_HEADS * C_EXPAND].multiply(0.5).astype(BF16)
            p = _inproj(h, shift, scale, norm_w[layer], w_odd, n_ctx_tiles)
            ys, w_out = [_gla(p, lower_bounds[j], od_gnorm_w[j], t_ctx)], od_w_out[j].astype(BF16)
        if last:
            out = _outproj(ys, w_out, h, gate, n_ctx_tiles, final_w=final_norm_w)
        else:
            h = _outproj(ys, w_out, h, gate, n_ctx_tiles)
    return out
```

```python
import functools

import jax
import jax.numpy as jnp
import numpy as np
from jax import lax
from jax.experimental import pallas as pl
from jax.experimental.pallas import tpu as pltpu

F32 = jnp.float32
BF16 = jnp.bfloat16

D_MODEL = 1024
DEPTH = 4
GRID_W = 64
RMS_EPS = 1e-6
NEG_BIG = -1e30
F_MIN = 1e-30

A_WIDTH = D_MODEL
A_HEADS = 8
A_BLOCK = A_WIDTH // A_HEADS
A_CONV = 4
A_C = 8.0
B_HEADS = 16
B_KV_HEADS = 4
B_HEAD_DIM = 64
B_GROUP = B_HEADS // B_KV_HEADS
B_WIDTH = B_HEADS * B_HEAD_DIM
B_KV_WIDTH = B_KV_HEADS * B_HEAD_DIM
WINDOW = 128
ROPE_BASE = 10000.0
ATTN_SCALE = B_HEAD_DIM ** -0.5
LOG2E = 1.4426950408889634
C_EXPAND = 128
C_HEADS = D_MODEL // C_EXPAND
C_CHUNK = 64

SUBLANES = 8
LANES = 128
ROW_TILE = 256
PROJ_SUB = 3
GLA_HEADS_PER_STEP = 2
VMEM_LIMIT = 56 * 1024 * 1024

_NT = (((1,), (1,)), ((), ()))
_TN = (((0,), (0,)), ((), ()))


def _params(*sem):
    return pltpu.CompilerParams(dimension_semantics=sem, vmem_limit_bytes=VMEM_LIMIT)


def _sigmoid(x):
    return 0.5 * jnp.tanh(0.5 * x) + 0.5


def _silu(x):
    h = 0.5 * x
    return h + h * jnp.tanh(h)


def _ada_kernel(sc_ref, w_ref, b_ref, o_ref):
    o_ref[...] = jnp.dot(sc_ref[...], w_ref[...], preferred_element_type=F32,
                         precision=lax.Precision.HIGHEST) + b_ref[...]


def _ada_mod(sc, ada_w, ada_b):
    depth, d, n = ada_w.shape
    rows = sc.shape[0]
    tn = 1024
    return pl.pallas_call(
        _ada_kernel,
        grid=(depth, n // tn),
        in_specs=[pl.BlockSpec((rows, d), lambda l, j: (0, 0)),
                  pl.BlockSpec((None, d, tn), lambda l, j: (l, 0, j)),
                  pl.BlockSpec((None, 1, tn), lambda l, j: (l, 0, j))],
        out_specs=pl.BlockSpec((None, rows, tn), lambda l, j: (l, 0, j)),
        out_shape=jax.ShapeDtypeStruct((depth, rows, n), F32),
        compiler_params=_params("parallel", "parallel"),
        name="ada_mod",
    )(sc, ada_w, ada_b.reshape(depth, 1, n))


def _mod_specs(b, n_ctx_tiles, sub, off=0):
    def spec(k):
        return pl.BlockSpec((None, SUBLANES, D_MODEL),
                            lambda i, j: (jnp.where(sub * j + k + off < n_ctx_tiles, b, i), 0, 0))
    return [spec(k) for k in range(sub)]


def _rope(x, cos, sin_a, sin_b):
    return x * cos + pltpu.roll(x, LANES - 1, 1) * sin_a + pltpu.roll(x, 1, 1) * sin_b


def _inproj_kernel(*refs, sub, attn):
    h_ref = refs[0]
    sh_refs, sc_refs = refs[1:1 + sub], refs[1 + sub:1 + 2 * sub]
    nw_ref, w_ref = refs[1 + 2 * sub:3 + 2 * sub]
    rest = refs[3 + 2 * sub:]
    if attn:
        cos_ref, sa_ref, sb_ref, qcos_ref, qsa_ref, qsb_ref, p_ref, u_ref, kv_ref, vt_ref, q_ref, u_scr = rest
    else:
        p_ref, u_scr = rest
    d = h_ref.shape[-1]
    r = ROW_TILE
    for k in range(sub):
        x = h_ref[k * r:(k + 1) * r, :]
        y = x * lax.rsqrt(jnp.mean(x * x, axis=-1, keepdims=True) + RMS_EPS) * nw_ref[...]
        y3 = y.reshape(r // SUBLANES, SUBLANES, d)
        u = (y3 * (1.0 + sc_refs[k][...])[None] + sh_refs[k][...][None]).reshape(r, d)
        u_scr[k * r:(k + 1) * r, :] = u.astype(BF16)
    n = p_ref.shape[-1]
    if attn:
        u_ref[...] = u_scr[...]
        nq = q_ref.shape[-1]
        kv = jnp.dot(u_scr[...], w_ref[:, nq + n:], preferred_element_type=F32)
        cos, sa, sb = cos_ref[...], sa_ref[...], sb_ref[...]
        dh = B_HEAD_DIM
        for g in range(B_KV_HEADS):
            sl = slice(g * LANES, (g + 1) * LANES)
            x = _rope(kv[:, sl], cos, sa, sb)
            kv_ref[:, sl] = x.astype(BF16)
            vt_ref[g * dh:(g + 1) * dh, :] = x.T[dh:, :].astype(BF16)
        qf = jnp.dot(u_scr[...], w_ref[:, :nq], preferred_element_type=F32)
        cos, sa, sb = qcos_ref[...], qsa_ref[...], qsb_ref[...]
        for c0 in range(0, nq, LANES):
            q_ref[:, c0:c0 + LANES] = (_rope(qf[:, c0:c0 + LANES], cos, sa, sb) * (ATTN_SCALE * LOG2E)).astype(BF16)
        p_ref[...] = jnp.dot(u_scr[...], w_ref[:, nq:nq + n], preferred_element_type=F32)
    else:
        p_ref[...] = jnp.dot(u_scr[...], w_ref[:, :n], preferred_element_type=F32)


def _inproj(h, shift, scale, norm_w, w, n_ctx_tiles, kv_tabs=None, q_tabs=None):
    b, t, d = h.shape
    attn = kv_tabs is not None
    wkv = 2 * B_KV_WIDTH
    n = w.shape[1] - ((wkv + B_WIDTH) if attn else 0)
    sub = PROJ_SUB
    tm = sub * ROW_TILE
    row = lambda width: pl.BlockSpec((None, tm, width), lambda i, j: (i, j, 0))
    in_specs = ([row(d)] + _mod_specs(b, n_ctx_tiles, sub) + _mod_specs(b, n_ctx_tiles, sub)
                + [pl.BlockSpec((1, d), lambda i, j: (0, 0)),
                   pl.BlockSpec(w.shape, lambda i, j: (0, 0), pipeline_mode=pl.Buffered(1))])
    args = [h] + [shift] * sub + [scale] * sub + [norm_w.reshape(1, d), w]
    out_shape = [jax.ShapeDtypeStruct((b, t, n), F32)]
    out_specs = [row(n)]
    if attn:
        in_specs += [pl.BlockSpec((tm, LANES), lambda i, j: (j, 0))] * 6
        args += list(kv_tabs) + list(q_tabs)
        out_shape += [jax.ShapeDtypeStruct((b, t, d), BF16), jax.ShapeDtypeStruct((b, t, wkv), BF16),
                      jax.ShapeDtypeStruct((b, B_KV_WIDTH, t), BF16), jax.ShapeDtypeStruct((b, t, B_WIDTH), BF16)]
        out_specs += [row(d), row(wkv), pl.BlockSpec((None, B_KV_WIDTH, tm), lambda i, j: (i, 0, j)), row(B_WIDTH)]
    res = pl.pallas_call(
        functools.partial(_inproj_kernel, sub=sub, attn=attn),
        grid=(b, t // tm),
        in_specs=in_specs,
        out_specs=out_specs,
        out_shape=out_shape,
        scratch_shapes=[pltpu.VMEM((tm, d), BF16)],
        compiler_params=_params("parallel", "parallel"),
        name="inproj",
    )(*args)
    return res if attn else res[0]


def _matmul_kernel(u_ref, w_ref, p_ref):
    p_ref[...] = jnp.dot(u_ref[...], w_ref[...], preferred_element_type=F32)


def _matmul(u, w):
    r, d = u.shape
    n = w.shape[1]
    tm = 4 * ROW_TILE
    return pl.pallas_call(
        _matmul_kernel,
        grid=(r // tm,),
        in_specs=[pl.BlockSpec((tm, d), lambda i: (i, 0)),
                  pl.BlockSpec((d, n), lambda i: (0, 0), pipeline_mode=pl.Buffered(1))],
        out_specs=pl.BlockSpec((tm, n), lambda i: (i, 0)),
        out_shape=jax.ShapeDtypeStruct((r, n), F32),
        compiler_params=_params("parallel"),
        name="matmul_tm",
    )(u, w)


def _scan_tile(i, n_ctx, n_lat, rev):
    if not rev:
        return i
    return jnp.where(i < n_ctx, n_ctx - 1 - i, 2 * n_ctx + n_lat - 1 - i)


def _rglru_kernel(*refs, rev, tt, n_ctx, n_lat):
    if rev:
        (xa_ref, xp_ref, xn1_ref, xn2_ref, ga_ref, yf_ref, cw_ref, cb_ref, w_ref, bx_ref, ba_ref,
         la_ref, out_ref, h_scr, xpad_scr, u_scr, a_scr, b_scr) = refs
    else:
        (xa_ref, xp_ref, xn1_ref, xn2_ref, cw_ref, cb_ref, w_ref, bx_ref, ba_ref,
         la_ref, out_ref, h_scr, xpad_scr, u_scr, a_scr, b_scr) = refs
    i = pl.program_id(0)
    tile = _scan_tile(i, n_ctx, n_lat, rev)
    seg_first = jnp.logical_or(tile == 0, tile == n_ctx)
    seg_last = jnp.logical_or(tile == n_ctx - 1, tile == n_ctx + n_lat - 1)
    rows = tt * SUBLANES

    @pl.when(i == 0)
    def _():
        h_scr[...] = jnp.zeros_like(h_scr)

    xpad_scr[0:8, :] = jnp.where(seg_first, 0.0, xp_ref[...])
    xpad_scr[8:8 + rows, :] = xa_ref[...]
    xpad_scr[8 + rows:16 + rows, :] = jnp.where(seg_last, 0.0, xn1_ref[...])
    xpad_scr[16 + rows:24 + rows, :] = jnp.where(seg_last, 0.0, xn2_ref[...])
    u = cb_ref[...] + xpad_scr[0:rows, :] * cw_ref[0:1, :]
    for k in range(1, A_CONV):
        u = u + xpad_scr[8 * k:8 * k + rows, :] * cw_ref[k:k + 1, :]
    u_scr[...] = u

    for hd in range(A_HEADS):
        sl = slice(hd * A_BLOCK, (hd + 1) * A_BLOCK)
        uh = u_scr[:, sl]
        g = jnp.dot(uh.astype(BF16), w_ref[hd], preferred_element_type=F32)
        gate_x = 0.5 * jnp.tanh(g[:, :A_BLOCK] + bx_ref[:, sl]) + 0.5
        lah = la_ref[:, sl]
        log_a = lah + lah * jnp.tanh(g[:, A_BLOCK:] + ba_ref[:, sl])
        a = jnp.exp(log_a)
        a_scr[:, sl] = a
        b_scr[:, sl] = jnp.sqrt(-jnp.tanh(log_a) * (1.0 + a * a)) * gate_x * uh

    def step(s, h):
        t = (tt - 1 - s) if rev else s
        r0 = pl.multiple_of(t * SUBLANES, SUBLANES)
        h = a_scr[pl.ds(r0, SUBLANES), :] * h + b_scr[pl.ds(r0, SUBLANES), :]
        b_scr[pl.ds(r0, SUBLANES), :] = h
        return h

    h_scr[...] = lax.fori_loop(0, tt, step, h_scr[...], unroll=8)
    if rev:
        out_ref[...] = ((yf_ref[...] + b_scr[...]) * _silu(ga_ref[...])).astype(BF16)
    else:
        out_ref[...] = b_scr[...]


def _rglru_pass(p_a, yf, conv_w, conv_b, w_cat, bx, ba, la, t_ctx, t_lat, rev):
    w = A_WIDTH
    tt = 64
    rows = tt * SUBLANES
    n_ctx, n_lat = t_ctx // tt, t_lat // tt
    t_all = t_ctx + t_lat
    tile = functools.partial(_scan_tile, n_ctx=n_ctx, n_lat=n_lat, rev=rev)
    const = lambda i: (0, 0)
    in_specs = [pl.BlockSpec((rows, w), lambda i: (tile(i), 0)),
                pl.BlockSpec((SUBLANES, w), lambda i: (jnp.maximum(tile(i) * tt - 1, 0), 0)),
                pl.BlockSpec((SUBLANES, w), lambda i: (jnp.minimum(tile(i) * tt + tt, t_all - 1), 0)),
                pl.BlockSpec((SUBLANES, w), lambda i: (jnp.minimum(tile(i) * tt + tt + 1, t_all - 1), 0))]
    args = [p_a, p_a, p_a, p_a]
    if rev:
        in_specs += [pl.BlockSpec((rows, w), lambda i: (tile(i), 1)),
                     pl.BlockSpec((rows, w), lambda i: (tile(i), 0))]
        args += [p_a, yf]
    in_specs += [pl.BlockSpec((A_CONV, w), const), pl.BlockSpec((1, w), const),
                 pl.BlockSpec((A_HEADS, A_BLOCK, 2 * A_BLOCK), lambda i: (0, 0, 0)),
                 pl.BlockSpec((1, w), const), pl.BlockSpec((1, w), const), pl.BlockSpec((1, w), const)]
    args += [conv_w, conv_b, w_cat, bx, ba, la]
    return pl.pallas_call(
        functools.partial(_rglru_kernel, rev=rev, tt=tt, n_ctx=n_ctx, n_lat=n_lat),
        grid=(n_ctx + n_lat,),
        in_specs=in_specs,
        out_specs=pl.BlockSpec((rows, w), lambda i: (tile(i), 0)),
        out_shape=jax.ShapeDtypeStruct((t_all * SUBLANES, w), BF16 if rev else F32),
        scratch_shapes=[pltpu.VMEM((SUBLANES, w), F32),
                        pltpu.VMEM((rows + 3 * SUBLANES, w), F32),
                        pltpu.VMEM((rows, w), F32),
                        pltpu.VMEM((rows, w), F32),
                        pltpu.VMEM((rows, w), F32)],
        compiler_params=_params("arbitrary"),
        name="rglru_bwd" if rev else "rglru_fwd",
    )(*args)


def _attn_kernel(sink_ref, q_ref, kc_ref, kp_ref, ko_ref, kn_ref, vc_ref, vp_ref, vo_ref, vn_ref, gb_ref,
                 o_ref, s_scr, p_scr, bias_scr, *, n_ctx_blk, n_lat_blk):
    n = pl.program_id(1)
    is_lat = n >= n_ctx_blk

    @pl.when(is_lat)
    def _():
        _attend(True, n - n_ctx_blk, n_lat_blk, sink_ref, q_ref, (kc_ref, kp_ref, ko_ref, kn_ref),
                (vc_ref, vp_ref, vo_ref, vn_ref), gb_ref, o_ref, s_scr, p_scr, bias_scr)

    @pl.when(jnp.logical_not(is_lat))
    def _():
        _attend(False, None, None, sink_ref, q_ref, (kc_ref,), (vc_ref,), gb_ref, o_ref, s_scr, p_scr, bias_scr)


def _attend(window, m, n_lat_blk, sink_ref, q_ref, k_refs, v_refs, gb_ref, o_ref, s_scr, p_scr, bias_scr):
    w = WINDOW
    dh = B_HEAD_DIM
    lc = k_refs[0].shape[0]

    low = lax.broadcasted_iota(jnp.int32, (w, LANES), 1) < dh
    segs = [(0, lc, None)]
    if window:
        kj = lax.broadcasted_iota(jnp.int32, (w, w), 0)
        qi = lax.broadcasted_iota(jnp.int32, (w, w), 1)
        neg = jnp.full((w, w), NEG_BIG, F32)
        zer = jnp.zeros((w, w), F32)
        bias_scr[0] = jnp.where(jnp.logical_and(m >= 1, kj >= qi), zer, neg)
        bias_scr[1] = jnp.where(jnp.logical_and(m <= n_lat_blk - 2, kj <= qi), zer, neg)
        segs += [(lc, w, 0), (lc + w, w, None), (lc + 2 * w, w, 1)]
    nk = segs[-1][0] + segs[-1][1]
    piece = 32
    pieces = [(r0, bi, p0) for r0, nr, bi in segs for p0 in range(0, nr, piece)]

    def scores(g, cb, r0, bi, p0):
        x = s_scr[g, r0 + p0:r0 + p0 + piece, cb]
        return x if bi is None else x + bias_scr[bi, p0:p0 + piece, :]

    def tree(xs, op):
        while len(xs) > 1:
            xs = [op(xs[i], xs[i + 1]) for i in range(0, len(xs) - 1, 2)] + ([xs[-1]] if len(xs) % 2 else [])
        return xs[0]

    def fold8(x, op):
        return op(x.reshape(x.shape[0] // SUBLANES, SUBLANES, x.shape[1]), axis=0)

    def score_stage(g):
        parts = []
        for c in range(2):
            col0 = g * B_GROUP * dh + c * LANES
            qc = q_ref[:, col0:col0 + LANES].astype(F32)
            parts.append(jnp.where(low, qc, 0.0).astype(BF16))
            parts.append(jnp.where(low, pltpu.roll(qc, dh, 1), 0.0).astype(BF16))
        qs = jnp.concatenate(parts, axis=0)
        ks = slice(g * LANES, (g + 1) * LANES)
        kv_all = jnp.concatenate([r[:, ks] for r in k_refs], axis=0)
        s_scr[g, 0:nk, :] = lax.dot_general(kv_all, qs, _NT, preferred_element_type=F32)

    def softmax_stage(g):
        rdens = []
        for h in range(B_GROUP):
            cb = slice(h * w, (h + 1) * w)
            snk = sink_ref[g * B_GROUP + h] * LOG2E
            mx8 = tree([fold8(scores(g, cb, r0, bi, p0), jnp.max) for r0, bi, p0 in pieces], jnp.maximum)
            mx = jnp.maximum(jnp.max(mx8, axis=0, keepdims=True), snk)
            sums = []
            for r0, bi, p0 in pieces:
                p = jnp.exp2(scores(g, cb, r0, bi, p0) - mx)
                sums.append(fold8(p, jnp.sum))
                p_scr[g, r0 + p0:r0 + p0 + piece, cb] = p.astype(BF16)
            den8 = tree(sums, jnp.add)
            rdens.append(1.0 / (jnp.sum(den8, axis=0, keepdims=True) + jnp.exp2(snk - mx)))
        return rdens

    def value_stage(g, rdens):
        vt_all = jnp.concatenate([r[g * dh:(g + 1) * dh, :] for r in v_refs], axis=1)
        ov = jnp.dot(vt_all, p_scr[g, 0:nk, :], preferred_element_type=F32) * jnp.concatenate(rdens, axis=1)
        for c in range(2):
            pair = jnp.concatenate([ov[:, (2 * c) * w:(2 * c + 1) * w], ov[:, (2 * c + 1) * w:(2 * c + 2) * w]],
                                   axis=0)
            col0 = g * B_GROUP * dh + c * LANES
            y = pair.T * _silu(gb_ref[:, col0:col0 + LANES])
            o_ref[:, col0:col0 + LANES] = y.astype(BF16)

    rd = {}
    for step in range(B_KV_HEADS + 2):
        if step < B_KV_HEADS:
            score_stage(step)
        if 0 <= step - 1 < B_KV_HEADS:
            rd[step - 1] = softmax_stage(step - 1)
        if 0 <= step - 2 < B_KV_HEADS:
            value_stage(step - 2, rd[step - 2])


def _attention(q, gate_b, kvr, vt, sink, t_ctx):
    b, t, _ = q.shape
    w = WINDOW
    n_ctx_blk, n_lat_blk = t_ctx // w, (t - t_ctx) // w
    nb = t // w
    wkv = 2 * B_KV_WIDTH
    prev = lambda n: jnp.maximum(n - 1, 0)
    nxt = lambda n: jnp.minimum(n + 1, nb - 1)
    kvb = lambda f: pl.BlockSpec((None, w, wkv), lambda i, n, s: (i, f(n), 0))
    vtb = lambda f: pl.BlockSpec((None, B_KV_WIDTH, w), lambda i, n, s: (i, 0, f(n)))
    grid_spec = pltpu.PrefetchScalarGridSpec(
        num_scalar_prefetch=1,
        grid=(b, nb),
        in_specs=[pl.BlockSpec((None, w, B_WIDTH), lambda i, n, s: (i, n, 0)),
                  pl.BlockSpec((None, t_ctx, wkv), lambda i, n, s: (i, 0, 0)),
                  kvb(prev), kvb(lambda n: n), kvb(nxt),
                  pl.BlockSpec((None, B_KV_WIDTH, t_ctx), lambda i, n, s: (i, 0, 0)),
                  vtb(prev), vtb(lambda n: n), vtb(nxt),
                  pl.BlockSpec((None, w, B_WIDTH), lambda i, n, s: (i, n, 0))],
        out_specs=pl.BlockSpec((None, w, B_WIDTH), lambda i, n, s: (i, n, 0)),
        scratch_shapes=[pltpu.VMEM((B_KV_HEADS, t_ctx + 3 * w, B_GROUP * w), F32),
                        pltpu.VMEM((B_KV_HEADS, t_ctx + 3 * w, B_GROUP * w), BF16),
                        pltpu.VMEM((2, w, w), F32)],
    )
    return pl.pallas_call(
        functools.partial(_attn_kernel, n_ctx_blk=n_ctx_blk, n_lat_blk=n_lat_blk),
        grid_spec=grid_spec,
        out_shape=jax.ShapeDtypeStruct((b, t, B_WIDTH), BF16),
        compiler_params=_params("parallel", "parallel"),
        name="window_attn",
    )(sink, q, kvr, kvr, kvr, kvr, vt, vt, vt, vt, gate_b)


GLA_DIAG = 4
GLA_LEVELS = (4, 8, 16, 32)
GLA_EXP_KINDS = 2 + len(GLA_LEVELS)
GLA_UNROLL = 2


def _gla_region_ids(rev):
    c = C_CHUNK
    ri = lax.broadcasted_iota(jnp.int32, (c, c), 0)
    ci = lax.broadcasted_iota(jnp.int32, (c, c), 1)
    reg = jnp.full((c, c), len(GLA_LEVELS), jnp.int32)
    for li in range(len(GLA_LEVELS) - 1, -1, -1):
        m = GLA_LEVELS[li]
        reg = jnp.where(ri // m == ci // m, li, reg)
    seen = (ci >= ri) if rev else (ci <= ri)
    return jnp.where(seen, reg, -1)


def _gla_exponent_matrix(rev):
    c = C_CHUNK
    t = np.arange(c)[:, None]
    s = np.arange(c)[None, :]
    upto = (s >= t) if rev else (s <= t)
    after = ~upto
    mats = [upto, after]
    for m in GLA_LEVELS:
        later = ((t // m) % 2 == 0) if rev else ((t // m) % 2 == 1)
        mats.append((t // m == s // m) & np.where(later, upto, after))
    mat = np.concatenate(mats, axis=0).astype(np.float32)
    return np.concatenate([mat, mat], axis=1)


def _grp(x, j):
    return x[j * SUBLANES:(j + 1) * SUBLANES, :]


def _gla_stage_a1(z, qr, lb, emat, rev, lane):
    c = C_CHUNK
    ng = c // SUBLANES

    half = 0.5 * (1.0 - lb)
    ht = half * jnp.tanh(z)
    f = (lb + half) + ht
    kk = half - ht
    ft = jnp.maximum(f, F_MIN)
    q = _silu(qr)

    g2 = jnp.log2(ft)
    g_hi = g2.astype(BF16)
    g_lo = (g2 - g_hi.astype(F32)).astype(BF16)
    expo = jnp.dot(emat, jnp.concatenate([g_hi, g_lo], axis=0), preferred_element_type=F32)

    dparts = []
    for j in range(ng):
        fg, qg, kg = _grp(ft, j), _grp(q, j), _grp(kk, j)
        w_ = kg
        acc = jnp.where(lane == 0, jnp.sum(qg * kg, axis=-1, keepdims=True), 0.0)
        for dlt in range(1, GLA_DIAG):
            w_ = fg * pltpu.roll(w_, (SUBLANES - 1) if rev else 1, 0)
            red = jnp.sum(qg * w_, axis=-1, keepdims=True)
            acc = jnp.where(lane == (dlt if rev else LANES - dlt), red, acc)
        dparts.append(acc)
    diag = pltpu.roll(jnp.concatenate(dparts, axis=0), 0, 1, stride=1, stride_axis=0)[:, :c]
    return q, kk, expo, diag


def _gla_stage_a2(q, kk, expo, diag, v, rev, reg, row8):
    c = C_CHUNK
    ng = c // SUBLANES
    dec = jnp.exp2(expo)
    last = 0 if rev else c - 1
    total = dec[last:last + 1, :]
    qd = (q * dec[0:c]).astype(BF16)
    kd = (kk * dec[c:2 * c]).astype(BF16)
    grp = _grp

    zero = jnp.zeros((SUBLANES, LANES), F32)
    levels = []
    for li, m in enumerate(GLA_LEVELS):
        fac = dec[(2 + li) * c:(3 + li) * c]
        qparts, kparts = [], []
        for j in range(ng):
            if m >= SUBLANES:
                later = ((j * SUBLANES // m) % 2 == 1) != rev
                prod = (grp(q, j) if later else grp(kk, j)) * grp(fac, j)
                qparts.append(prod if later else zero)
                kparts.append(zero if later else prod)
            else:
                later = ((row8 // m) % 2 == 0) if rev else ((row8 // m) % 2 == 1)
                prod = jnp.where(later, grp(q, j), grp(kk, j)) * grp(fac, j)
                qparts.append(jnp.where(later, prod, 0.0))
                kparts.append(jnp.where(later, 0.0, prod))
        levels.append(lax.dot_general(jnp.concatenate(qparts, axis=0).astype(BF16),
                                      jnp.concatenate(kparts, axis=0).astype(BF16), _NT,
                                      preferred_element_type=F32))

    attn = jnp.zeros((c, c), F32)
    for li in range(len(GLA_LEVELS) - 1, -1, -1):
        attn = jnp.where(reg == li + 1, levels[li], attn)
    attn = jnp.where(reg == 0, diag, attn)
    return attn.astype(BF16), qd, kd, v.astype(BF16), total


def _gla_stage_b(attn, qd, kd, vb, total, st):
    o = jnp.dot(attn, vb, preferred_element_type=F32)
    o = o + lax.dot_general(qd, st.astype(BF16), _NT, preferred_element_type=F32)
    st_new = total * st + lax.dot_general(vb, kd, _TN, preferred_element_type=F32)
    return o, st_new


def _gla_kernel(zf_ref, zb_ref, v_ref, q_ref, og_ref, lb_ref, gw_ref, emat_ref, y_ref, of_scr, ob_scr, st_scr,
                attn_c, qd_c, kd_c, vb_c, tot_c, q_c, k_c, expo_c, diag_c,
                *, n_ctx, n_lat, heads):
    c = C_CHUNK
    row8 = lax.broadcasted_iota(jnp.int32, (SUBLANES, LANES), 0)
    lane = lax.broadcasted_iota(jnp.int32, (SUBLANES, LANES), 1)
    regs = (_gla_region_ids(False), _gla_region_ids(True))
    n_all = n_ctx + n_lat
    st_scr[...] = jnp.zeros_like(st_scr)
    streams = [(hh, rev) for hh in range(heads) for rev in (False, True)]

    def rows_of(i, rev):
        if isinstance(i, int):
            cidx = ((n_ctx - 1 - i) if i < n_ctx else (2 * n_ctx + n_lat - 1 - i)) if rev else i
            return pl.ds(cidx * c, c)
        cidx = jnp.where(i < n_ctx, n_ctx - 1 - i, 2 * n_ctx + n_lat - 1 - i) if rev else i
        return pl.ds(pl.multiple_of(cidx * c, c), c)

    def stage_a1(i):
        for k, (hh, rev) in enumerate(streams):
            ls = slice(hh * LANES, (hh + 1) * LANES)
            rows = rows_of(i, rev)
            z_ref = zb_ref if rev else zf_ref
            q, kk, expo, diag = _gla_stage_a1(z_ref[rows, ls], q_ref[rows, ls], lb_ref[:, ls],
                                              emat_ref[int(rev)], rev, lane)
            q_c[k] = q
            k_c[k] = kk
            expo_c[k] = expo
            diag_c[k] = diag

    def stage_a2(i):
        for k, (hh, rev) in enumerate(streams):
            ls = slice(hh * LANES, (hh + 1) * LANES)
            attn, qd, kd, vb, total = _gla_stage_a2(q_c[k], k_c[k], expo_c[k], diag_c[k],
                                                    v_ref[rows_of(i, rev), ls], rev, regs[int(rev)], row8)
            attn_c[k] = attn
            qd_c[k] = qd
            kd_c[k] = kd
            vb_c[k] = vb
            tot_c[k] = jnp.broadcast_to(total, (SUBLANES, LANES))

    def stage_b(i):
        for k, (hh, rev) in enumerate(streams):
            ls = slice(hh * LANES, (hh + 1) * LANES)
            o, st = _gla_stage_b(attn_c[k], qd_c[k], kd_c[k], vb_c[k], tot_c[k, 0:1, :], st_scr[k])
            (ob_scr if rev else of_scr)[rows_of(i, rev), ls] = o
            st_scr[k] = st

    stage_a1(0)
    stage_a2(0)
    stage_a1(1)

    def step(i):
        stage_b(i - 2)
        stage_a2(i - 1)
        stage_a1(i)

    def body(j, carry):
        for r in range(GLA_UNROLL):
            step(2 + GLA_UNROLL * j + r)
        return carry

    trips = (n_all - 2) // GLA_UNROLL
    lax.fori_loop(0, trips, body, 0)
    for i in range(2 + trips * GLA_UNROLL, n_all):
        step(i)
    stage_b(n_all - 2)
    stage_a2(n_all - 1)
    stage_b(n_all - 1)

    gw = gw_ref[...]
    piece = 256

    def fin(i, carry):
        rows = pl.ds(pl.multiple_of(i * piece, piece), piece)
        for hh in range(heads):
            ls = slice(hh * LANES, (hh + 1) * LANES)
            o = of_scr[rows, ls] + ob_scr[rows, ls]
            y = o * lax.rsqrt(jnp.mean(o * o, axis=-1, keepdims=True) + RMS_EPS) * gw
            y_ref[rows, ls] = (y * _silu(og_ref[rows, ls])).astype(BF16)
        return carry

    lax.fori_loop(0, (n_all * c) // piece, fin, 0)


def _gla(p, lb, gw, t_ctx):
    b, t, _ = p.shape
    hps = GLA_HEADS_PER_STEP
    wb = hps * LANES
    nblk = C_HEADS // hps
    col = lambda k: pl.BlockSpec((None, t, wb), lambda i, h: (i, 0, k * nblk + h))
    emat = jnp.asarray(np.stack([_gla_exponent_matrix(False), _gla_exponent_matrix(True)]), BF16)
    return pl.pallas_call(
        functools.partial(_gla_kernel, n_ctx=t_ctx // C_CHUNK, n_lat=(t - t_ctx) // C_CHUNK, heads=hps),
        grid=(b, nblk),
        in_specs=[col(0), col(1), col(2), col(3), col(4),
                  pl.BlockSpec((None, 1, wb), lambda i, h: (h, 0, 0)),
                  pl.BlockSpec((1, LANES), lambda i, h: (0, 0)),
                  pl.BlockSpec(emat.shape, lambda i, h: (0, 0, 0))],
        out_specs=pl.BlockSpec((None, t, wb), lambda i, h: (i, 0, h)),
        out_shape=jax.ShapeDtypeStruct((b, t, C_HEADS * LANES), BF16),
        scratch_shapes=[pltpu.VMEM((t, wb), F32), pltpu.VMEM((t, wb), F32),
                        pltpu.VMEM((2 * hps, C_EXPAND, LANES), F32),
                        pltpu.VMEM((2 * hps, C_CHUNK, C_CHUNK), BF16),
                        pltpu.VMEM((2 * hps, C_CHUNK, LANES), BF16),
                        pltpu.VMEM((2 * hps, C_CHUNK, LANES), BF16),
                        pltpu.VMEM((2 * hps, C_CHUNK, LANES), BF16),
                        pltpu.VMEM((2 * hps, SUBLANES, LANES), F32),
                        pltpu.VMEM((2 * hps, C_CHUNK, LANES), F32),
                        pltpu.VMEM((2 * hps, C_CHUNK, LANES), F32),
                        pltpu.VMEM((2 * hps, GLA_EXP_KINDS * C_CHUNK, LANES), F32),
                        pltpu.VMEM((2 * hps, C_CHUNK, C_CHUNK), F32)],
        compiler_params=_params("parallel", "parallel"),
        name="hgrn2_gla",
    )(p, p, p, p, p, lb.reshape(nblk, 1, wb), gw.reshape(1, LANES), emat)


def _outproj_kernel(*refs, n_in, sub, final):
    y_refs = refs[:n_in]
    w_ref, h_ref = refs[n_in:n_in + 2]
    gt_refs = refs[n_in + 2:n_in + 2 + sub]
    if final:
        fw_ref, o_ref = refs[n_in + 2 + sub:]
    else:
        (o_ref,) = refs[n_in + 2 + sub:]
    acc = None
    k0 = 0
    for y_ref in y_refs:
        kw = y_ref.shape[-1]
        part = jnp.dot(y_ref[...], w_ref[k0:k0 + kw, :], preferred_element_type=F32)
        acc = part if acc is None else acc + part
        k0 += kw
    d = acc.shape[-1]
    r = ROW_TILE
    for k in range(sub):
        a3 = acc[k * r:(k + 1) * r, :].reshape(r // SUBLANES, SUBLANES, d)
        hn = h_ref[k * r:(k + 1) * r, :] + (a3 * gt_refs[k][...][None]).reshape(r, d)
        if final:
            hn = hn * lax.rsqrt(jnp.mean(hn * hn, axis=-1, keepdims=True) + RMS_EPS) * fw_ref[...]
        o_ref[k * r:(k + 1) * r, :] = hn


def _outproj(ys, w, h, gate, n_ctx_tiles, final_w=None):
    b, t, d = h.shape
    final = final_w is not None
    sub = 1 if final else PROJ_SUB
    tm = sub * ROW_TILE
    off = n_ctx_tiles if final else 0
    nt = t // tm - off
    in_specs = [pl.BlockSpec((None, tm, y.shape[-1]), lambda i, j: (i, j + off, 0)) for y in ys]
    in_specs += [pl.BlockSpec(w.shape, lambda i, j: (0, 0), pipeline_mode=pl.Buffered(1)),
                 pl.BlockSpec((None, tm, d), lambda i, j: (i, j + off, 0))]
    in_specs += _mod_specs(b, n_ctx_tiles, sub, off)
    args = list(ys) + [w, h] + [gate] * sub
    if final:
        in_specs.append(pl.BlockSpec((1, d), lambda i, j: (0, 0)))
        args.append(final_w.reshape(1, d))
    return pl.pallas_call(
        functools.partial(_outproj_kernel, n_in=len(ys), sub=sub, final=final),
        grid=(b, nt),
        in_specs=in_specs,
        out_specs=pl.BlockSpec((None, tm, d), lambda i, j: (i, j, 0)),
        out_shape=jax.ShapeDtypeStruct((b, nt * tm, d), F32),
        compiler_params=_params("parallel", "parallel"),
        name="outproj",
    )(*args)


def _rope_tables(s_lat, t_ctx):
    pos = np.arange(s_lat)
    row = (pos // GRID_W).astype(np.float32)
    colp = (pos % GRID_W).astype(np.float32)
    axis_dim = B_HEAD_DIM // 2
    inv = jnp.asarray(ROPE_BASE, F32) ** (-jnp.arange(0, axis_dim, 2, dtype=F32) / axis_dim)
    ang = jnp.concatenate([jnp.asarray(row)[:, None] * inv, jnp.asarray(colp)[:, None] * inv], axis=-1)
    cos, sin = jnp.cos(ang), jnp.sin(ang)
    zer = jnp.zeros_like(sin)
    pairs = lambda even, odd: jnp.stack([even, odd], axis=-1).reshape(s_lat, B_HEAD_DIM)
    cos64 = pairs(cos, cos)
    sa64 = pairs(-sin, zer)
    sb64 = pairs(zer, sin)
    one64 = jnp.ones_like(cos64)
    zer64 = jnp.zeros_like(cos64)

    def full(lat128, ident):
        return jnp.concatenate([jnp.broadcast_to(ident, (t_ctx, LANES)), lat128], axis=0)

    ident_c = jnp.ones((1, LANES), F32)
    ident_s = jnp.zeros((1, LANES), F32)
    q_tabs = (full(jnp.concatenate([cos64, cos64], -1), ident_c),
              full(jnp.concatenate([sa64, sa64], -1), ident_s),
              full(jnp.concatenate([sb64, sb64], -1), ident_s))
    kv_tabs = (full(jnp.concatenate([cos64, one64], -1), ident_c),
               full(jnp.concatenate([sa64, zer64], -1), ident_s),
               full(jnp.concatenate([sb64, zer64], -1), ident_s))
    return q_tabs, kv_tabs


def _even_weights(w_in):
    dh = B_HEAD_DIM
    n_state = A_WIDTH + 2 * B_KV_WIDTH
    q0 = n_state + A_WIDTH
    k0 = A_WIDTH
    v0 = A_WIDTH + B_KV_WIDTH
    parts = [w_in[:, q0:q0 + 2 * B_WIDTH]]
    for j in range(B_KV_HEADS):
        parts += [w_in[:, k0 + j * dh:k0 + (j + 1) * dh], w_in[:, v0 + j * dh:v0 + (j + 1) * dh]]
    w_b = jnp.concatenate(parts, axis=1).astype(BF16)
    w_a = jnp.concatenate([w_in[:, :A_WIDTH], w_in[:, n_state:n_state + A_WIDTH]], axis=1).astype(BF16)
    return w_b, w_a


def _groups(vec_b, vec_c):
    allv = jnp.concatenate([vec_b, vec_c[None, :]], axis=0)
    return jnp.broadcast_to(allv[:, None, :], (allv.shape[0], SUBLANES, allv.shape[1]))


def kernel(x, c, ctx, c_ctx, ada_w, ada_b, norm_w, ev_w_in, ev_conv_w, ev_conv_b, ev_rg_wx, ev_rg_bx,
           ev_rg_wa, ev_rg_ba, ev_rg_lambda, ev_sink, ev_w_out, od_w_in, od_lb_raw, od_gnorm_w, od_w_out,
           final_norm_w):
    b, s_lat, d = x.shape
    t_ctx = ctx.shape[1]
    t_all = t_ctx + s_lat
    assert b == SUBLANES and d == D_MODEL
    assert t_ctx % ROW_TILE == 0 and s_lat % ROW_TILE == 0 and t_all % (PROJ_SUB * ROW_TILE) == 0
    n_ctx_tiles = t_ctx // ROW_TILE

    sc_rows = jnp.zeros((2 * SUBLANES, d), F32)
    sc_rows = sc_rows.at[:b].set(jax.nn.silu(c)).at[b].set(jax.nn.silu(c_ctx))
    mod = _ada_mod(sc_rows, ada_w, ada_b)

    lb_p = jax.nn.softmax(od_lb_raw.astype(F32), axis=0)
    lower_bounds = jnp.cumsum(lb_p, axis=0) - lb_p[0]
    q_tabs, kv_tabs = _rope_tables(s_lat, t_ctx)

    h = jnp.concatenate([ctx, x], axis=1)
    out = None
    for layer in range(DEPTH):
        last = layer == DEPTH - 1
        j = layer // 2
        ml = mod[layer]
        shift = _groups(ml[:b, :d], ml[b, :d])
        scale = _groups(ml[:b, d:2 * d], ml[b, d:2 * d])
        gate = _groups(ml[:b, 2 * d:], ml[b, 2 * d:])
        if layer % 2 == 0:
            w_b, w_a = _even_weights(ev_w_in[j])
            gate_b, u, kvr, vt, q = _inproj(h, shift, scale, norm_w[layer], w_b, n_ctx_tiles,
                                            kv_tabs=kv_tabs, q_tabs=q_tabs)
            u_tm = jnp.transpose(u, (1, 0, 2)).reshape(t_all * b, d)
            p_a = _matmul(u_tm, w_a)
            lah = -0.5 * A_C * jax.nn.softplus(-ev_rg_lambda[j].astype(F32))
            w_cat = (0.5 * jnp.concatenate([ev_rg_wx[j], ev_rg_wa[j]], axis=-1)).astype(BF16)
            rg = lambda dr, yf: _rglru_pass(
                p_a, yf, ev_conv_w[j], ev_conv_b[j].reshape(1, -1), w_cat[dr],
                0.5 * ev_rg_bx[j, dr].reshape(1, -1), 0.5 * ev_rg_ba[j, dr].reshape(1, -1), lah[dr].reshape(1, -1),
                t_ctx, s_lat, rev=bool(dr))
            ya = jnp.transpose(rg(1, rg(0, None)).reshape(t_all, b, A_WIDTH), (1, 0, 2))
            yb = _attention(q, gate_b, kvr, vt, ev_sink[j].astype(F32), t_ctx)
            ys, w_out = [ya, yb], ev_w_out[j].astype(BF16)
        else:
            w_odd = od_w_in[j].at[:, :2 * C_HEADS * C_EXPAND].multiply(0.5).astype(BF16)
            p = _inproj(h, shift, scale, norm_w[layer], w_odd, n_ctx_tiles)
            ys, w_out = [_gla(p, lower_bounds[j], od_gnorm_w[j], t_ctx)], od_w_out[j].astype(BF16)
        if last:
            out = _outproj(ys, w_out, h, gate, n_ctx_tiles, final_w=final_norm_w)
        else:
            h = _outproj(ys, w_out, h, gate, n_ctx_tiles)
    return out
```

```python
import functools

import jax
import jax.numpy as jnp
import numpy as np
from jax import lax
from jax.experimental import pallas as pl
from jax.experimental.pallas import tpu as pltpu

F32 = jnp.float32
BF16 = jnp.bfloat16

D_MODEL = 1024
DEPTH = 4
GRID_W = 64
RMS_EPS = 1e-6
NEG_BIG = -1e30
F_MIN = 1e-30

A_WIDTH = D_MODEL
A_HEADS = 8
A_BLOCK = A_WIDTH // A_HEADS
A_CONV = 4
A_C = 8.0
B_HEADS = 16
B_KV_HEADS = 4
B_HEAD_DIM = 64
B_GROUP = B_HEADS // B_KV_HEADS
B_WIDTH = B_HEADS * B_HEAD_DIM
B_KV_WIDTH = B_KV_HEADS * B_HEAD_DIM
WINDOW = 128
ROPE_BASE = 10000.0
ATTN_SCALE = B_HEAD_DIM ** -0.5
LOG2E = 1.4426950408889634
C_EXPAND = 128
C_HEADS = D_MODEL // C_EXPAND
C_CHUNK = 64

SUBLANES = 8
LANES = 128
ROW_TILE = 256
PROJ_SUB = 3
GLA_HEADS_PER_STEP = 2
VMEM_LIMIT = 56 * 1024 * 1024

_NT = (((1,), (1,)), ((), ()))
_TN = (((0,), (0,)), ((), ()))


def _params(*sem):
    return pltpu.CompilerParams(dimension_semantics=sem, vmem_limit_bytes=VMEM_LIMIT)


def _sigmoid(x):
    return 0.5 * jnp.tanh(0.5 * x) + 0.5


def _silu(x):
    h = 0.5 * x
    return h + h * jnp.tanh(h)


def _ada_kernel(sc_ref, w_ref, b_ref, o_ref):
    o_ref[...] = jnp.dot(sc_ref[...], w_ref[...], preferred_element_type=F32,
                         precision=lax.Precision.HIGHEST) + b_ref[...]


def _ada_mod(sc, ada_w, ada_b):
    depth, d, n = ada_w.shape
    rows = sc.shape[0]
    tn = 1024
    return pl.pallas_call(
        _ada_kernel,
        grid=(depth, n // tn),
        in_specs=[pl.BlockSpec((rows, d), lambda l, j: (0, 0)),
                  pl.BlockSpec((None, d, tn), lambda l, j: (l, 0, j)),
                  pl.BlockSpec((None, 1, tn), lambda l, j: (l, 0, j))],
        out_specs=pl.BlockSpec((None, rows, tn), lambda l, j: (l, 0, j)),
        out_shape=jax.ShapeDtypeStruct((depth, rows, n), F32),
        compiler_params=_params("parallel", "parallel"),
        name="ada_mod",
    )(sc, ada_w, ada_b.reshape(depth, 1, n))


def _mod_specs(b, n_ctx_tiles, sub, off=0):
    def spec(k):
        return pl.BlockSpec((None, SUBLANES, D_MODEL),
                            lambda i, j: (jnp.where(sub * j + k + off < n_ctx_tiles, b, i), 0, 0))
    return [spec(k) for k in range(sub)]


def _rope(x, cos, sin_a, sin_b):
    return x * cos + pltpu.roll(x, LANES - 1, 1) * sin_a + pltpu.roll(x, 1, 1) * sin_b


def _inproj_kernel(*refs, sub, attn):
    h_ref = refs[0]
    sh_refs, sc_refs = refs[1:1 + sub], refs[1 + sub:1 + 2 * sub]
    nw_ref, w_ref = refs[1 + 2 * sub:3 + 2 * sub]
    rest = refs[3 + 2 * sub:]
    if attn:
        cos_ref, sa_ref, sb_ref, qcos_ref, qsa_ref, qsb_ref, p_ref, u_ref, kv_ref, vt_ref, q_ref, u_scr = rest
    else:
        p_ref, u_scr = rest
    d = h_ref.shape[-1]
    r = ROW_TILE
    for k in range(sub):
        x = h_ref[k * r:(k + 1) * r, :]
        y = x * lax.rsqrt(jnp.mean(x * x, axis=-1, keepdims=True) + RMS_EPS) * nw_ref[...]
        y3 = y.reshape(r // SUBLANES, SUBLANES, d)
        u = (y3 * (1.0 + sc_refs[k][...])[None] + sh_refs[k][...][None]).reshape(r, d)
        u_scr[k * r:(k + 1) * r, :] = u.astype(BF16)
        if not attn:
            p_ref[k * r:(k + 1) * r, :] = jnp.dot(u_scr[k * r:(k + 1) * r, :], w_ref[...],
                                                  preferred_element_type=F32)
    n = p_ref.shape[-1]
    if attn:
        u_ref[...] = u_scr[...]
        nq = q_ref.shape[-1]
        kv = jnp.dot(u_scr[...], w_ref[:, nq + n:], preferred_element_type=F32)
        cos, sa, sb = cos_ref[...], sa_ref[...], sb_ref[...]
        dh = B_HEAD_DIM
        for g in range(B_KV_HEADS):
            sl = slice(g * LANES, (g + 1) * LANES)
            x = _rope(kv[:, sl], cos, sa, sb)
            kv_ref[:, sl] = x.astype(BF16)
            vt_ref[g * dh:(g + 1) * dh, :] = x.T[dh:, :].astype(BF16)
        qf = jnp.dot(u_scr[...], w_ref[:, :nq], preferred_element_type=F32)
        cos, sa, sb = qcos_ref[...], qsa_ref[...], qsb_ref[...]
        for c0 in range(0, nq, LANES):
            q_ref[:, c0:c0 + LANES] = (_rope(qf[:, c0:c0 + LANES], cos, sa, sb) * (ATTN_SCALE * LOG2E)).astype(BF16)
        p_ref[...] = jnp.dot(u_scr[...], w_ref[:, nq:nq + n], preferred_element_type=F32)


def _inproj(h, shift, scale, norm_w, w, n_ctx_tiles, kv_tabs=None, q_tabs=None):
    b, t, d = h.shape
    attn = kv_tabs is not None
    wkv = 2 * B_KV_WIDTH
    n = w.shape[1] - ((wkv + B_WIDTH) if attn else 0)
    sub = PROJ_SUB
    tm = sub * ROW_TILE
    row = lambda width: pl.BlockSpec((None, tm, width), lambda i, j: (i, j, 0))
    in_specs = ([row(d)] + _mod_specs(b, n_ctx_tiles, sub) + _mod_specs(b, n_ctx_tiles, sub)
                + [pl.BlockSpec((1, d), lambda i, j: (0, 0)),
                   pl.BlockSpec(w.shape, lambda i, j: (0, 0), pipeline_mode=pl.Buffered(1))])
    args = [h] + [shift] * sub + [scale] * sub + [norm_w.reshape(1, d), w]
    out_shape = [jax.ShapeDtypeStruct((b, t, n), F32)]
    out_specs = [row(n)]
    if attn:
        in_specs += [pl.BlockSpec((tm, LANES), lambda i, j: (j, 0))] * 6
        args += list(kv_tabs) + list(q_tabs)
        out_shape += [jax.ShapeDtypeStruct((b, t, d), BF16), jax.ShapeDtypeStruct((b, t, wkv), BF16),
                      jax.ShapeDtypeStruct((b, B_KV_WIDTH, t), BF16), jax.ShapeDtypeStruct((b, t, B_WIDTH), BF16)]
        out_specs += [row(d), row(wkv), pl.BlockSpec((None, B_KV_WIDTH, tm), lambda i, j: (i, 0, j)), row(B_WIDTH)]
    res = pl.pallas_call(
        functools.partial(_inproj_kernel, sub=sub, attn=attn),
        grid=(b, t // tm),
        in_specs=in_specs,
        out_specs=out_specs,
        out_shape=out_shape,
        scratch_shapes=[pltpu.VMEM((tm, d), BF16)],
        compiler_params=_params("parallel", "parallel"),
        name="inproj",
    )(*args)
    return res if attn else res[0]


def _matmul_kernel(u_ref, w_ref, p_ref):
    p_ref[...] = jnp.dot(u_ref[...], w_ref[...], preferred_element_type=F32)


def _matmul(u, w):
    r, d = u.shape
    n = w.shape[1]
    tm = 4 * ROW_TILE
    return pl.pallas_call(
        _matmul_kernel,
        grid=(r // tm,),
        in_specs=[pl.BlockSpec((tm, d), lambda i: (i, 0)),
                  pl.BlockSpec((d, n), lambda i: (0, 0), pipeline_mode=pl.Buffered(1))],
        out_specs=pl.BlockSpec((tm, n), lambda i: (i, 0)),
        out_shape=jax.ShapeDtypeStruct((r, n), F32),
        compiler_params=_params("parallel"),
        name="matmul_tm",
    )(u, w)


def _scan_tile(i, n_ctx, n_lat, rev):
    if not rev:
        return i
    return jnp.where(i < n_ctx, n_ctx - 1 - i, 2 * n_ctx + n_lat - 1 - i)


def _rglru_kernel(*refs, rev, tt, n_ctx, n_lat):
    if rev:
        (xa_ref, xp_ref, xn1_ref, xn2_ref, ga_ref, yf_ref, cw_ref, cb_ref, w_ref, bx_ref, ba_ref,
         la_ref, out_ref, h_scr, xpad_scr, u_scr, a_scr, b_scr) = refs
    else:
        (xa_ref, xp_ref, xn1_ref, xn2_ref, cw_ref, cb_ref, w_ref, bx_ref, ba_ref,
         la_ref, out_ref, h_scr, xpad_scr, u_scr, a_scr, b_scr) = refs
    i = pl.program_id(0)
    tile = _scan_tile(i, n_ctx, n_lat, rev)
    seg_first = jnp.logical_or(tile == 0, tile == n_ctx)
    seg_last = jnp.logical_or(tile == n_ctx - 1, tile == n_ctx + n_lat - 1)
    rows = tt * SUBLANES

    @pl.when(i == 0)
    def _():
        h_scr[...] = jnp.zeros_like(h_scr)

    xpad_scr[0:8, :] = jnp.where(seg_first, 0.0, xp_ref[...])
    xpad_scr[8:8 + rows, :] = xa_ref[...]
    xpad_scr[8 + rows:16 + rows, :] = jnp.where(seg_last, 0.0, xn1_ref[...])
    xpad_scr[16 + rows:24 + rows, :] = jnp.where(seg_last, 0.0, xn2_ref[...])
    u = cb_ref[...] + xpad_scr[0:rows, :] * cw_ref[0:1, :]
    for k in range(1, A_CONV):
        u = u + xpad_scr[8 * k:8 * k + rows, :] * cw_ref[k:k + 1, :]
    u_scr[...] = u

    for hd in range(A_HEADS):
        sl = slice(hd * A_BLOCK, (hd + 1) * A_BLOCK)
        uh = u_scr[:, sl]
        g = jnp.dot(uh.astype(BF16), w_ref[hd], preferred_element_type=F32)
        gate_x = 0.5 * jnp.tanh(g[:, :A_BLOCK] + bx_ref[:, sl]) + 0.5
        lah = la_ref[:, sl]
        log_a = lah + lah * jnp.tanh(g[:, A_BLOCK:] + ba_ref[:, sl])
        a = jnp.exp(log_a)
        a_scr[:, sl] = a
        b_scr[:, sl] = jnp.sqrt(-jnp.tanh(log_a) * (1.0 + a * a)) * gate_x * uh

    def step(s, h):
        t = (tt - 1 - s) if rev else s
        r0 = pl.multiple_of(t * SUBLANES, SUBLANES)
        h = a_scr[pl.ds(r0, SUBLANES), :] * h + b_scr[pl.ds(r0, SUBLANES), :]
        b_scr[pl.ds(r0, SUBLANES), :] = h
        return h

    h_scr[...] = lax.fori_loop(0, tt, step, h_scr[...], unroll=8)
    if rev:
        out_ref[...] = ((yf_ref[...] + b_scr[...]) * _silu(ga_ref[...])).astype(BF16)
    else:
        out_ref[...] = b_scr[...]


def _rglru_pass(p_a, yf, conv_w, conv_b, w_cat, bx, ba, la, t_ctx, t_lat, rev):
    w = A_WIDTH
    tt = 128
    rows = tt * SUBLANES
    n_ctx, n_lat = t_ctx // tt, t_lat // tt
    t_all = t_ctx + t_lat
    tile = functools.partial(_scan_tile, n_ctx=n_ctx, n_lat=n_lat, rev=rev)
    const = lambda i: (0, 0)
    in_specs = [pl.BlockSpec((rows, w), lambda i: (tile(i), 0)),
                pl.BlockSpec((SUBLANES, w), lambda i: (jnp.maximum(tile(i) * tt - 1, 0), 0)),
                pl.BlockSpec((SUBLANES, w), lambda i: (jnp.minimum(tile(i) * tt + tt, t_all - 1), 0)),
                pl.BlockSpec((SUBLANES, w), lambda i: (jnp.minimum(tile(i) * tt + tt + 1, t_all - 1), 0))]
    args = [p_a, p_a, p_a, p_a]
    if rev:
        in_specs += [pl.BlockSpec((rows, w), lambda i: (tile(i), 1)),
                     pl.BlockSpec((rows, w), lambda i: (tile(i), 0))]
        args += [p_a, yf]
    in_specs += [pl.BlockSpec((A_CONV, w), const), pl.BlockSpec((1, w), const),
                 pl.BlockSpec((A_HEADS, A_BLOCK, 2 * A_BLOCK), lambda i: (0, 0, 0)),
                 pl.BlockSpec((1, w), const), pl.BlockSpec((1, w), const), pl.BlockSpec((1, w), const)]
    args += [conv_w, conv_b, w_cat, bx, ba, la]
    return pl.pallas_call(
        functools.partial(_rglru_kernel, rev=rev, tt=tt, n_ctx=n_ctx, n_lat=n_lat),
        grid=(n_ctx + n_lat,),
        in_specs=in_specs,
        out_specs=pl.BlockSpec((rows, w), lambda i: (tile(i), 0)),
        out_shape=jax.ShapeDtypeStruct((t_all * SUBLANES, w), BF16 if rev else F32),
        scratch_shapes=[pltpu.VMEM((SUBLANES, w), F32),
                        pltpu.VMEM((rows + 3 * SUBLANES, w), F32),
                        pltpu.VMEM((rows, w), F32),
                        pltpu.VMEM((rows, w), F32),
                        pltpu.VMEM((rows, w), F32)],
        compiler_params=_params("arbitrary"),
        name="rglru_bwd" if rev else "rglru_fwd",
    )(*args)


def _attn_kernel(sink_ref, q_ref, kc_ref, kp_ref, ko_ref, kn_ref, vc_ref, vp_ref, vo_ref, vn_ref, gb_ref,
                 o_ref, s_scr, p_scr, bias_scr, *, n_ctx_blk, n_lat_blk):
    n = pl.program_id(1)
    is_lat = n >= n_ctx_blk

    @pl.when(is_lat)
    def _():
        _attend(True, n - n_ctx_blk, n_lat_blk, sink_ref, q_ref, (kc_ref, kp_ref, ko_ref, kn_ref),
                (vc_ref, vp_ref, vo_ref, vn_ref), gb_ref, o_ref, s_scr, p_scr, bias_scr)

    @pl.when(jnp.logical_not(is_lat))
    def _():
        _attend(False, None, None, sink_ref, q_ref, (kc_ref,), (vc_ref,), gb_ref, o_ref, s_scr, p_scr, bias_scr)


def _attend(window, m, n_lat_blk, sink_ref, q_ref, k_refs, v_refs, gb_ref, o_ref, s_scr, p_scr, bias_scr):
    w = WINDOW
    dh = B_HEAD_DIM
    lc = k_refs[0].shape[0]

    low = lax.broadcasted_iota(jnp.int32, (w, LANES), 1) < dh
    segs = [(0, lc, None)]
    if window:
        kj = lax.broadcasted_iota(jnp.int32, (w, w), 0)
        qi = lax.broadcasted_iota(jnp.int32, (w, w), 1)
        neg = jnp.full((w, w), NEG_BIG, F32)
        zer = jnp.zeros((w, w), F32)
        bias_scr[0] = jnp.where(jnp.logical_and(m >= 1, kj >= qi), zer, neg)
        bias_scr[1] = jnp.where(jnp.logical_and(m <= n_lat_blk - 2, kj <= qi), zer, neg)
        segs += [(lc, w, 0), (lc + w, w, None), (lc + 2 * w, w, 1)]
    nk = segs[-1][0] + segs[-1][1]
    piece = 32
    pieces = [(r0, bi, p0) for r0, nr, bi in segs for p0 in range(0, nr, piece)]

    def scores(g, cb, r0, bi, p0):
        x = s_scr[g, r0 + p0:r0 + p0 + piece, cb]
        return x if bi is None else x + bias_scr[bi, p0:p0 + piece, :]

    def tree(xs, op):
        while len(xs) > 1:
            xs = [op(xs[i], xs[i + 1]) for i in range(0, len(xs) - 1, 2)] + ([xs[-1]] if len(xs) % 2 else [])
        return xs[0]

    def fold8(x, op):
        return op(x.reshape(x.shape[0] // SUBLANES, SUBLANES, x.shape[1]), axis=0)

    def score_stage(g):
        parts = []
        for c in range(2):
            col0 = g * B_GROUP * dh + c * LANES
            qc = q_ref[:, col0:col0 + LANES].astype(F32)
            parts.append(jnp.where(low, qc, 0.0).astype(BF16))
            parts.append(jnp.where(low, pltpu.roll(qc, dh, 1), 0.0).astype(BF16))
        qs = jnp.concatenate(parts, axis=0)
        ks = slice(g * LANES, (g + 1) * LANES)
        kv_all = jnp.concatenate([r[:, ks] for r in k_refs], axis=0)
        s_scr[g, 0:nk, :] = lax.dot_general(kv_all, qs, _NT, preferred_element_type=F32)

    def softmax_stage(g):
        rdens = []
        for h in range(B_GROUP):
            cb = slice(h * w, (h + 1) * w)
            snk = sink_ref[g * B_GROUP + h] * LOG2E
            mx8 = tree([fold8(scores(g, cb, r0, bi, p0), jnp.max) for r0, bi, p0 in pieces], jnp.maximum)
            mx = jnp.maximum(jnp.max(mx8, axis=0, keepdims=True), snk)
            sums = []
            for r0, bi, p0 in pieces:
                p = jnp.exp2(scores(g, cb, r0, bi, p0) - mx)
                sums.append(fold8(p, jnp.sum))
                p_scr[g, r0 + p0:r0 + p0 + piece, cb] = p.astype(BF16)
            den8 = tree(sums, jnp.add)
            rdens.append(1.0 / (jnp.sum(den8, axis=0, keepdims=True) + jnp.exp2(snk - mx)))
        return rdens

    def value_stage(g, rdens):
        vt_all = jnp.concatenate([r[g * dh:(g + 1) * dh, :] for r in v_refs], axis=1)
        ov = jnp.dot(vt_all, p_scr[g, 0:nk, :], preferred_element_type=F32) * jnp.concatenate(rdens, axis=1)
        for c in range(2):
            pair = jnp.concatenate([ov[:, (2 * c) * w:(2 * c + 1) * w], ov[:, (2 * c + 1) * w:(2 * c + 2) * w]],
                                   axis=0)
            col0 = g * B_GROUP * dh + c * LANES
            y = pair.T * _silu(gb_ref[:, col0:col0 + LANES])
            o_ref[:, col0:col0 + LANES] = y.astype(BF16)

    rd = {}
    for step in range(B_KV_HEADS + 2):
        if step < B_KV_HEADS:
            score_stage(step)
        if 0 <= step - 1 < B_KV_HEADS:
            rd[step - 1] = softmax_stage(step - 1)
        if 0 <= step - 2 < B_KV_HEADS:
            value_stage(step - 2, rd[step - 2])


def _attention(q, gate_b, kvr, vt, sink, t_ctx):
    b, t, _ = q.shape
    w = WINDOW
    n_ctx_blk, n_lat_blk = t_ctx // w, (t - t_ctx) // w
    nb = t // w
    wkv = 2 * B_KV_WIDTH
    prev = lambda n: jnp.maximum(n - 1, 0)
    nxt = lambda n: jnp.minimum(n + 1, nb - 1)
    kvb = lambda f: pl.BlockSpec((None, w, wkv), lambda i, n, s: (i, f(n), 0))
    vtb = lambda f: pl.BlockSpec((None, B_KV_WIDTH, w), lambda i, n, s: (i, 0, f(n)))
    grid_spec = pltpu.PrefetchScalarGridSpec(
        num_scalar_prefetch=1,
        grid=(b, nb),
        in_specs=[pl.BlockSpec((None, w, B_WIDTH), lambda i, n, s: (i, n, 0)),
                  pl.BlockSpec((None, t_ctx, wkv), lambda i, n, s: (i, 0, 0)),
                  kvb(prev), kvb(lambda n: n), kvb(nxt),
                  pl.BlockSpec((None, B_KV_WIDTH, t_ctx), lambda i, n, s: (i, 0, 0)),
                  vtb(prev), vtb(lambda n: n), vtb(nxt),
                  pl.BlockSpec((None, w, B_WIDTH), lambda i, n, s: (i, n, 0))],
        out_specs=pl.BlockSpec((None, w, B_WIDTH), lambda i, n, s: (i, n, 0)),
        scratch_shapes=[pltpu.VMEM((B_KV_HEADS, t_ctx + 3 * w, B_GROUP * w), F32),
                        pltpu.VMEM((B_KV_HEADS, t_ctx + 3 * w, B_GROUP * w), BF16),
                        pltpu.VMEM((2, w, w), F32)],
    )
    return pl.pallas_call(
        functools.partial(_attn_kernel, n_ctx_blk=n_ctx_blk, n_lat_blk=n_lat_blk),
        grid_spec=grid_spec,
        out_shape=jax.ShapeDtypeStruct((b, t, B_WIDTH), BF16),
        compiler_params=_params("parallel", "parallel"),
        name="window_attn",
    )(sink, q, kvr, kvr, kvr, kvr, vt, vt, vt, vt, gate_b)


GLA_DIAG = 4
GLA_LEVELS = (4, 8, 16, 32)
GLA_EXP_KINDS = 2 + len(GLA_LEVELS)
GLA_UNROLL = 2


def _gla_region_ids(rev):
    c = C_CHUNK
    ri = lax.broadcasted_iota(jnp.int32, (c, c), 0)
    ci = lax.broadcasted_iota(jnp.int32, (c, c), 1)
    reg = jnp.full((c, c), len(GLA_LEVELS), jnp.int32)
    for li in range(len(GLA_LEVELS) - 1, -1, -1):
        m = GLA_LEVELS[li]
        reg = jnp.where(ri // m == ci // m, li, reg)
    seen = (ci >= ri) if rev else (ci <= ri)
    return jnp.where(seen, reg, -1)


def _gla_exponent_matrix(rev):
    c = C_CHUNK
    t = np.arange(c)[:, None]
    s = np.arange(c)[None, :]
    upto = (s >= t) if rev else (s <= t)
    after = ~upto
    mats = [upto, after]
    for m in GLA_LEVELS:
        later = ((t // m) % 2 == 0) if rev else ((t // m) % 2 == 1)
        mats.append((t // m == s // m) & np.where(later, upto, after))
    mat = np.concatenate(mats, axis=0).astype(np.float32)
    return np.concatenate([mat, mat], axis=1)


def _grp(x, j):
    return x[j * SUBLANES:(j + 1) * SUBLANES, :]


def _gla_stage_a1(z, qr, lb, emat, rev, lane):
    c = C_CHUNK
    ng = c // SUBLANES

    half = 0.5 * (1.0 - lb)
    ht = half * jnp.tanh(z)
    f = (lb + half) + ht
    kk = half - ht
    ft = jnp.maximum(f, F_MIN)
    q = _silu(qr)

    g2 = jnp.log2(ft)
    g_hi = g2.astype(BF16)
    g_lo = (g2 - g_hi.astype(F32)).astype(BF16)
    expo = jnp.dot(emat, jnp.concatenate([g_hi, g_lo], axis=0), preferred_element_type=F32)

    dparts = []
    for j in range(ng):
        fg, qg, kg = _grp(ft, j), _grp(q, j), _grp(kk, j)
        w_ = kg
        acc = jnp.where(lane == 0, jnp.sum(qg * kg, axis=-1, keepdims=True), 0.0)
        for dlt in range(1, GLA_DIAG):
            w_ = fg * pltpu.roll(w_, (SUBLANES - 1) if rev else 1, 0)
            red = jnp.sum(qg * w_, axis=-1, keepdims=True)
            acc = jnp.where(lane == (dlt if rev else LANES - dlt), red, acc)
        dparts.append(acc)
    diag = pltpu.roll(jnp.concatenate(dparts, axis=0), 0, 1, stride=1, stride_axis=0)[:, :c]
    return q, kk, expo, diag


def _gla_stage_a2(q, kk, expo, diag, v, rev, reg, row8):
    c = C_CHUNK
    ng = c // SUBLANES
    dec = jnp.exp2(expo)
    last = 0 if rev else c - 1
    total = dec[last:last + 1, :]
    qd = (q * dec[0:c]).astype(BF16)
    kd = (kk * dec[c:2 * c]).astype(BF16)
    grp = _grp

    zero = jnp.zeros((SUBLANES, LANES), F32)
    levels = []
    for li, m in enumerate(GLA_LEVELS):
        fac = dec[(2 + li) * c:(3 + li) * c]
        qparts, kparts = [], []
        for j in range(ng):
            if m >= SUBLANES:
                later = ((j * SUBLANES // m) % 2 == 1) != rev
                prod = (grp(q, j) if later else grp(kk, j)) * grp(fac, j)
                qparts.append(prod if later else zero)
                kparts.append(zero if later else prod)
            else:
                later = ((row8 // m) % 2 == 0) if rev else ((row8 // m) % 2 == 1)
                prod = jnp.where(later, grp(q, j), grp(kk, j)) * grp(fac, j)
                qparts.append(jnp.where(later, prod, 0.0))
                kparts.append(jnp.where(later, 0.0, prod))
        levels.append(lax.dot_general(jnp.concatenate(qparts, axis=0).astype(BF16),
                                      jnp.concatenate(kparts, axis=0).astype(BF16), _NT,
                                      preferred_element_type=F32))

    attn = jnp.zeros((c, c), F32)
    for li in range(len(GLA_LEVELS) - 1, -1, -1):
        attn = jnp.where(reg == li + 1, levels[li], attn)
    attn = jnp.where(reg == 0, diag, attn)
    return attn.astype(BF16), qd, kd, v.astype(BF16), total


def _gla_stage_b(attn, qd, kd, vb, total, st):
    o = jnp.dot(attn, vb, preferred_element_type=F32)
    o = o + lax.dot_general(qd, st.astype(BF16), _NT, preferred_element_type=F32)
    st_new = total * st + lax.dot_general(vb, kd, _TN, preferred_element_type=F32)
    return o, st_new


def _gla_kernel(zf_ref, zb_ref, v_ref, q_ref, og_ref, lb_ref, gw_ref, emat_ref, y_ref, of_scr, ob_scr, st_scr,
                attn_c, qd_c, kd_c, vb_c, tot_c, q_c, k_c, expo_c, diag_c,
                *, n_ctx, n_lat, heads):
    c = C_CHUNK
    row8 = lax.broadcasted_iota(jnp.int32, (SUBLANES, LANES), 0)
    lane = lax.broadcasted_iota(jnp.int32, (SUBLANES, LANES), 1)
    regs = (_gla_region_ids(False), _gla_region_ids(True))
    n_all = n_ctx + n_lat
    st_scr[...] = jnp.zeros_like(st_scr)
    streams = [(hh, rev) for hh in range(heads) for rev in (False, True)]

    def rows_of(i, rev):
        if isinstance(i, int):
            cidx = ((n_ctx - 1 - i) if i < n_ctx else (2 * n_ctx + n_lat - 1 - i)) if rev else i
            return pl.ds(cidx * c, c)
        cidx = jnp.where(i < n_ctx, n_ctx - 1 - i, 2 * n_ctx + n_lat - 1 - i) if rev else i
        return pl.ds(pl.multiple_of(cidx * c, c), c)

    def stage_a1(i):
        for k, (hh, rev) in enumerate(streams):
            ls = slice(hh * LANES, (hh + 1) * LANES)
            rows = rows_of(i, rev)
            z_ref = zb_ref if rev else zf_ref
            q, kk, expo, diag = _gla_stage_a1(z_ref[rows, ls], q_ref[rows, ls], lb_ref[:, ls],
                                              emat_ref[int(rev)], rev, lane)
            q_c[k] = q
            k_c[k] = kk
            expo_c[k] = expo
            diag_c[k] = diag

    def stage_a2(i):
        for k, (hh, rev) in enumerate(streams):
            ls = slice(hh * LANES, (hh + 1) * LANES)
            attn, qd, kd, vb, total = _gla_stage_a2(q_c[k], k_c[k], expo_c[k], diag_c[k],
                                                    v_ref[rows_of(i, rev), ls], rev, regs[int(rev)], row8)
            attn_c[k] = attn
            qd_c[k] = qd
            kd_c[k] = kd
            vb_c[k] = vb
            tot_c[k] = jnp.broadcast_to(total, (SUBLANES, LANES))

    def stage_b(i):
        for k, (hh, rev) in enumerate(streams):
            ls = slice(hh * LANES, (hh + 1) * LANES)
            o, st = _gla_stage_b(attn_c[k], qd_c[k], kd_c[k], vb_c[k], tot_c[k, 0:1, :], st_scr[k])
            (ob_scr if rev else of_scr)[rows_of(i, rev), ls] = o
            st_scr[k] = st

    stage_a1(0)
    stage_a2(0)
    stage_a1(1)

    def step(i):
        stage_b(i - 2)
        stage_a2(i - 1)
        stage_a1(i)

    def body(j, carry):
        for r in range(GLA_UNROLL):
            step(2 + GLA_UNROLL * j + r)
        return carry

    trips = (n_all - 2) // GLA_UNROLL
    lax.fori_loop(0, trips, body, 0)
    for i in range(2 + trips * GLA_UNROLL, n_all):
        step(i)
    stage_b(n_all - 2)
    stage_a2(n_all - 1)
    stage_b(n_all - 1)

    gw = gw_ref[...]
    piece = PROJ_SUB * ROW_TILE

    def fin(i, carry):
        rows = pl.ds(pl.multiple_of(i * piece, piece), piece)
        for hh in range(heads):
            ls = slice(hh * LANES, (hh + 1) * LANES)
            o = of_scr[rows, ls] + ob_scr[rows, ls]
            y = o * lax.rsqrt(jnp.mean(o * o, axis=-1, keepdims=True) + RMS_EPS) * gw
            y_ref[rows, ls] = (y * _silu(og_ref[rows, ls])).astype(BF16)
        return carry

    lax.fori_loop(0, (n_all * c) // piece, fin, 0)


def _gla(p, lb, gw, t_ctx):
    b, t, _ = p.shape
    hps = GLA_HEADS_PER_STEP
    wb = hps * LANES
    nblk = C_HEADS // hps
    col = lambda k: pl.BlockSpec((None, t, wb), lambda i, h: (i, 0, k * nblk + h))
    emat = jnp.asarray(np.stack([_gla_exponent_matrix(False), _gla_exponent_matrix(True)]), BF16)
    return pl.pallas_call(
        functools.partial(_gla_kernel, n_ctx=t_ctx // C_CHUNK, n_lat=(t - t_ctx) // C_CHUNK, heads=hps),
        grid=(b, nblk),
        in_specs=[col(0), col(1), col(2), col(3), col(4),
                  pl.BlockSpec((None, 1, wb), lambda i, h: (h, 0, 0)),
                  pl.BlockSpec((1, LANES), lambda i, h: (0, 0)),
                  pl.BlockSpec(emat.shape, lambda i, h: (0, 0, 0))],
        out_specs=pl.BlockSpec((None, t, wb), lambda i, h: (i, 0, h)),
        out_shape=jax.ShapeDtypeStruct((b, t, C_HEADS * LANES), BF16),
        scratch_shapes=[pltpu.VMEM((t, wb), F32), pltpu.VMEM((t, wb), F32),
                        pltpu.VMEM((2 * hps, C_EXPAND, LANES), F32),
                        pltpu.VMEM((2 * hps, C_CHUNK, C_CHUNK), BF16),
                        pltpu.VMEM((2 * hps, C_CHUNK, LANES), BF16),
                        pltpu.VMEM((2 * hps, C_CHUNK, LANES), BF16),
                        pltpu.VMEM((2 * hps, C_CHUNK, LANES), BF16),
                        pltpu.VMEM((2 * hps, SUBLANES, LANES), F32),
                        pltpu.VMEM((2 * hps, C_CHUNK, LANES), F32),
                        pltpu.VMEM((2 * hps, C_CHUNK, LANES), F32),
                        pltpu.VMEM((2 * hps, GLA_EXP_KINDS * C_CHUNK, LANES), F32),
                        pltpu.VMEM((2 * hps, C_CHUNK, C_CHUNK), F32)],
        compiler_params=_params("parallel", "parallel"),
        name="hgrn2_gla",
    )(p, p, p, p, p, lb.reshape(nblk, 1, wb), gw.reshape(1, LANES), emat)


def _outproj_kernel(*refs, n_in, sub, final):
    y_refs = refs[:n_in]
    w_ref, h_ref = refs[n_in:n_in + 2]
    gt_refs = refs[n_in + 2:n_in + 2 + sub]
    if final:
        fw_ref, o_ref = refs[n_in + 2 + sub:]
    else:
        (o_ref,) = refs[n_in + 2 + sub:]
    acc = None
    k0 = 0
    for y_ref in y_refs:
        kw = y_ref.shape[-1]
        part = jnp.dot(y_ref[...], w_ref[k0:k0 + kw, :], preferred_element_type=F32)
        acc = part if acc is None else acc + part
        k0 += kw
    d = acc.shape[-1]
    r = ROW_TILE
    for k in range(sub):
        a3 = acc[k * r:(k + 1) * r, :].reshape(r // SUBLANES, SUBLANES, d)
        hn = h_ref[k * r:(k + 1) * r, :] + (a3 * gt_refs[k][...][None]).reshape(r, d)
        if final:
            hn = hn * lax.rsqrt(jnp.mean(hn * hn, axis=-1, keepdims=True) + RMS_EPS) * fw_ref[...]
        o_ref[k * r:(k + 1) * r, :] = hn


def _outproj(ys, w, h, gate, n_ctx_tiles, final_w=None):
    b, t, d = h.shape
    final = final_w is not None
    sub = 1 if final else PROJ_SUB
    tm = sub * ROW_TILE
    off = n_ctx_tiles if final else 0
    nt = t // tm - off
    in_specs = [pl.BlockSpec((None, tm, y.shape[-1]), lambda i, j: (i, j + off, 0)) for y in ys]
    in_specs += [pl.BlockSpec(w.shape, lambda i, j: (0, 0), pipeline_mode=pl.Buffered(1)),
                 pl.BlockSpec((None, tm, d), lambda i, j: (i, j + off, 0))]
    in_specs += _mod_specs(b, n_ctx_tiles, sub, off)
    args = list(ys) + [w, h] + [gate] * sub
    if final:
        in_specs.append(pl.BlockSpec((1, d), lambda i, j: (0, 0)))
        args.append(final_w.reshape(1, d))
    return pl.pallas_call(
        functools.partial(_outproj_kernel, n_in=len(ys), sub=sub, final=final),
        grid=(b, nt),
        in_specs=in_specs,
        out_specs=pl.BlockSpec((None, tm, d), lambda i, j: (i, j, 0)),
        out_shape=jax.ShapeDtypeStruct((b, nt * tm, d), F32),
        compiler_params=_params("parallel", "parallel"),
        name="outproj",
    )(*args)


def _rope_tables(s_lat, t_ctx):
    pos = np.arange(s_lat)
    row = (pos // GRID_W).astype(np.float32)
    colp = (pos % GRID_W).astype(np.float32)
    axis_dim = B_HEAD_DIM // 2
    inv = jnp.asarray(ROPE_BASE, F32) ** (-jnp.arange(0, axis_dim, 2, dtype=F32) / axis_dim)
    ang = jnp.concatenate([jnp.asarray(row)[:, None] * inv, jnp.asarray(colp)[:, None] * inv], axis=-1)
    cos, sin = jnp.cos(ang), jnp.sin(ang)
    zer = jnp.zeros_like(sin)
    pairs = lambda even, odd: jnp.stack([even, odd], axis=-1).reshape(s_lat, B_HEAD_DIM)
    cos64 = pairs(cos, cos)
    sa64 = pairs(-sin, zer)
    sb64 = pairs(zer, sin)
    one64 = jnp.ones_like(cos64)
    zer64 = jnp.zeros_like(cos64)

    def full(lat128, ident):
        return jnp.concatenate([jnp.broadcast_to(ident, (t_ctx, LANES)), lat128], axis=0)

    ident_c = jnp.ones((1, LANES), F32)
    ident_s = jnp.zeros((1, LANES), F32)
    q_tabs = (full(jnp.concatenate([cos64, cos64], -1), ident_c),
              full(jnp.concatenate([sa64, sa64], -1), ident_s),
              full(jnp.concatenate([sb64, sb64], -1), ident_s))
    kv_tabs = (full(jnp.concatenate([cos64, one64], -1), ident_c),
               full(jnp.concatenate([sa64, zer64], -1), ident_s),
               full(jnp.concatenate([sb64, zer64], -1), ident_s))
    return q_tabs, kv_tabs


def _even_weights(w_in):
    dh = B_HEAD_DIM
    n_state = A_WIDTH + 2 * B_KV_WIDTH
    q0 = n_state + A_WIDTH
    k0 = A_WIDTH
    v0 = A_WIDTH + B_KV_WIDTH
    parts = [w_in[:, q0:q0 + 2 * B_WIDTH]]
    for j in range(B_KV_HEADS):
        parts += [w_in[:, k0 + j * dh:k0 + (j + 1) * dh], w_in[:, v0 + j * dh:v0 + (j + 1) * dh]]
    w_b = jnp.concatenate(parts, axis=1).astype(BF16)
    w_a = jnp.concatenate([w_in[:, :A_WIDTH], w_in[:, n_state:n_state + A_WIDTH]], axis=1).astype(BF16)
    return w_b, w_a


def _groups(vec_b, vec_c):
    allv = jnp.concatenate([vec_b, vec_c[None, :]], axis=0)
    return jnp.broadcast_to(allv[:, None, :], (allv.shape[0], SUBLANES, allv.shape[1]))


def kernel(x, c, ctx, c_ctx, ada_w, ada_b, norm_w, ev_w_in, ev_conv_w, ev_conv_b, ev_rg_wx, ev_rg_bx,
           ev_rg_wa, ev_rg_ba, ev_rg_lambda, ev_sink, ev_w_out, od_w_in, od_lb_raw, od_gnorm_w, od_w_out,
           final_norm_w):
    b, s_lat, d = x.shape
    t_ctx = ctx.shape[1]
    t_all = t_ctx + s_lat
    assert b == SUBLANES and d == D_MODEL
    assert t_ctx % ROW_TILE == 0 and s_lat % ROW_TILE == 0 and t_all % (PROJ_SUB * ROW_TILE) == 0
    n_ctx_tiles = t_ctx // ROW_TILE

    sc_rows = jnp.zeros((2 * SUBLANES, d), F32)
    sc_rows = sc_rows.at[:b].set(jax.nn.silu(c)).at[b].set(jax.nn.silu(c_ctx))
    mod = _ada_mod(sc_rows, ada_w, ada_b)

    lb_p = jax.nn.softmax(od_lb_raw.astype(F32), axis=0)
    lower_bounds = jnp.cumsum(lb_p, axis=0) - lb_p[0]
    q_tabs, kv_tabs = _rope_tables(s_lat, t_ctx)

    h = jnp.concatenate([ctx, x], axis=1)
    out = None
    for layer in range(DEPTH):
        last = layer == DEPTH - 1
        j = layer // 2
        ml = mod[layer]
        shift = _groups(ml[:b, :d], ml[b, :d])
        scale = _groups(ml[:b, d:2 * d], ml[b, d:2 * d])
        gate = _groups(ml[:b, 2 * d:], ml[b, 2 * d:])
        if layer % 2 == 0:
            w_b, w_a = _even_weights(ev_w_in[j])
            gate_b, u, kvr, vt, q = _inproj(h, shift, scale, norm_w[layer], w_b, n_ctx_tiles,
                                            kv_tabs=kv_tabs, q_tabs=q_tabs)
            u_tm = jnp.transpose(u, (1, 0, 2)).reshape(t_all * b, d)
            p_a = _matmul(u_tm, w_a)
            lah = -0.5 * A_C * jax.nn.softplus(-ev_rg_lambda[j].astype(F32))
            w_cat = (0.5 * jnp.concatenate([ev_rg_wx[j], ev_rg_wa[j]], axis=-1)).astype(BF16)
            rg = lambda dr, yf: _rglru_pass(
                p_a, yf, ev_conv_w[j], ev_conv_b[j].reshape(1, -1), w_cat[dr],
                0.5 * ev_rg_bx[j, dr].reshape(1, -1), 0.5 * ev_rg_ba[j, dr].reshape(1, -1), lah[dr].reshape(1, -1),
                t_ctx, s_lat, rev=bool(dr))
            ya = jnp.transpose(rg(1, rg(0, None)).reshape(t_all, b, A_WIDTH), (1, 0, 2))
            yb = _attention(q, gate_b, kvr, vt, ev_sink[j].astype(F32), t_ctx)
            ys, w_out = [ya, yb], ev_w_out[j].astype(BF16)
        else:
            w_odd = od_w_in[j].at[:, :2 * C_HEADS * C_EXPAND].multiply(0.5).astype(BF16)
            p = _inproj(h, shift, scale, norm_w[layer], w_odd, n_ctx_tiles)
            ys, w_out = [_gla(p, lower_bounds[j], od_gnorm_w[j], t_ctx)], od_w_out[j].astype(BF16)
        if last:
            out = _outproj(ys, w_out, h, gate, n_ctx_tiles, final_w=final_norm_w)
        else:
            h = _outproj(ys, w_out, h, gate, n_ctx_tiles)
    return out
```

```python
import functools

import jax
import jax.numpy as jnp
import numpy as np
from jax import lax
from jax.experimental import pallas as pl
from jax.experimental.pallas import tpu as pltpu

F32 = jnp.float32
BF16 = jnp.bfloat16

D_MODEL = 1024
DEPTH = 4
GRID_W = 64
RMS_EPS = 1e-6
NEG_BIG = -1e30
F_MIN = 1e-30

A_WIDTH = D_MODEL
A_HEADS = 8
A_BLOCK = A_WIDTH // A_HEADS
A_CONV = 4
A_C = 8.0
B_HEADS = 16
B_KV_HEADS = 4
B_HEAD_DIM = 64
B_GROUP = B_HEADS // B_KV_HEADS
B_WIDTH = B_HEADS * B_HEAD_DIM
B_KV_WIDTH = B_KV_HEADS * B_HEAD_DIM
WINDOW = 128
ROPE_BASE = 10000.0
ATTN_SCALE = B_HEAD_DIM ** -0.5
LOG2E = 1.4426950408889634
C_EXPAND = 128
C_HEADS = D_MODEL // C_EXPAND
C_CHUNK = 64

SUBLANES = 8
LANES = 128
ROW_TILE = 256
PROJ_SUB = 3
GLA_HEADS_PER_STEP = 2
VMEM_LIMIT = 56 * 1024 * 1024

_NT = (((1,), (1,)), ((), ()))
_TN = (((0,), (0,)), ((), ()))


def _params(*sem):
    return pltpu.CompilerParams(dimension_semantics=sem, vmem_limit_bytes=VMEM_LIMIT)


def _sigmoid(x):
    return 0.5 * jnp.tanh(0.5 * x) + 0.5


def _silu(x):
    h = 0.5 * x
    return h + h * jnp.tanh(h)


def _ada_kernel(sc_ref, w_ref, b_ref, o_ref):
    o_ref[...] = jnp.dot(sc_ref[...], w_ref[...], preferred_element_type=F32,
                         precision=lax.Precision.HIGHEST) + b_ref[...]


def _ada_mod(sc, ada_w, ada_b):
    depth, d, n = ada_w.shape
    rows = sc.shape[0]
    tn = 1024
    return pl.pallas_call(
        _ada_kernel,
        grid=(depth, n // tn),
        in_specs=[pl.BlockSpec((rows, d), lambda l, j: (0, 0)),
                  pl.BlockSpec((None, d, tn), lambda l, j: (l, 0, j)),
                  pl.BlockSpec((None, 1, tn), lambda l, j: (l, 0, j))],
        out_specs=pl.BlockSpec((None, rows, tn), lambda l, j: (l, 0, j)),
        out_shape=jax.ShapeDtypeStruct((depth, rows, n), F32),
        compiler_params=_params("parallel", "parallel"),
        name="ada_mod",
    )(sc, ada_w, ada_b.reshape(depth, 1, n))


def _mod_specs(b, n_ctx_tiles, sub, off=0):
    def spec(k):
        return pl.BlockSpec((None, SUBLANES, D_MODEL),
                            lambda i, j: (jnp.where(sub * j + k + off < n_ctx_tiles, b, i), 0, 0))
    return [spec(k) for k in range(sub)]


def _rope(x, cos, sin_a, sin_b):
    return x * cos + pltpu.roll(x, LANES - 1, 1) * sin_a + pltpu.roll(x, 1, 1) * sin_b


def _inproj_kernel(*refs, sub, attn):
    h_ref = refs[0]
    sh_refs, sc_refs = refs[1:1 + sub], refs[1 + sub:1 + 2 * sub]
    nw_ref, w_ref = refs[1 + 2 * sub:3 + 2 * sub]
    rest = refs[3 + 2 * sub:]
    if attn:
        cos_ref, sa_ref, sb_ref, qcos_ref, qsa_ref, qsb_ref, p_ref, u_ref, kv_ref, vt_ref, q_ref, u_scr = rest
    else:
        p_ref, u_scr = rest
    d = h_ref.shape[-1]
    r = ROW_TILE
    for k in range(sub):
        x = h_ref[k * r:(k + 1) * r, :]
        y = x * lax.rsqrt(jnp.mean(x * x, axis=-1, keepdims=True) + RMS_EPS) * nw_ref[...]
        y3 = y.reshape(r // SUBLANES, SUBLANES, d)
        u = (y3 * (1.0 + sc_refs[k][...])[None] + sh_refs[k][...][None]).reshape(r, d)
        u_scr[k * r:(k + 1) * r, :] = u.astype(BF16)
        if not attn:
            p_ref[k * r:(k + 1) * r, :] = jnp.dot(u_scr[k * r:(k + 1) * r, :], w_ref[...],
                                                  preferred_element_type=F32)
    n = p_ref.shape[-1]
    if attn:
        u_ref[...] = u_scr[...]
        nq = q_ref.shape[-1]
        kv = jnp.dot(u_scr[...], w_ref[:, nq + n:], preferred_element_type=F32)
        cos, sa, sb = cos_ref[...], sa_ref[...], sb_ref[...]
        dh = B_HEAD_DIM
        for g in range(B_KV_HEADS):
            sl = slice(g * LANES, (g + 1) * LANES)
            x = _rope(kv[:, sl], cos, sa, sb)
            kv_ref[:, sl] = x.astype(BF16)
            vt_ref[g * dh:(g + 1) * dh, :] = x.T[dh:, :].astype(BF16)
        qf = jnp.dot(u_scr[...], w_ref[:, :nq], preferred_element_type=F32)
        cos, sa, sb = qcos_ref[...], qsa_ref[...], qsb_ref[...]
        for c0 in range(0, nq, LANES):
            q_ref[:, c0:c0 + LANES] = (_rope(qf[:, c0:c0 + LANES], cos, sa, sb) * (ATTN_SCALE * LOG2E)).astype(BF16)
        p_ref[...] = jnp.dot(u_scr[...], w_ref[:, nq:nq + n], preferred_element_type=F32)


def _inproj(h, shift, scale, norm_w, w, n_ctx_tiles, kv_tabs=None, q_tabs=None):
    b, t, d = h.shape
    attn = kv_tabs is not None
    wkv = 2 * B_KV_WIDTH
    n = w.shape[1] - ((wkv + B_WIDTH) if attn else 0)
    sub = PROJ_SUB
    tm = sub * ROW_TILE
    row = lambda width: pl.BlockSpec((None, tm, width), lambda i, j: (i, j, 0))
    in_specs = ([row(d)] + _mod_specs(b, n_ctx_tiles, sub) + _mod_specs(b, n_ctx_tiles, sub)
                + [pl.BlockSpec((1, d), lambda i, j: (0, 0)),
                   pl.BlockSpec(w.shape, lambda i, j: (0, 0), pipeline_mode=pl.Buffered(1))])
    args = [h] + [shift] * sub + [scale] * sub + [norm_w.reshape(1, d), w]
    out_shape = [jax.ShapeDtypeStruct((b, t, n), F32)]
    out_specs = [row(n)]
    if attn:
        in_specs += [pl.BlockSpec((tm, LANES), lambda i, j: (j, 0))] * 6
        args += list(kv_tabs) + list(q_tabs)
        out_shape += [jax.ShapeDtypeStruct((b, t, d), BF16), jax.ShapeDtypeStruct((b, t, wkv), BF16),
                      jax.ShapeDtypeStruct((b, B_KV_WIDTH, t), BF16), jax.ShapeDtypeStruct((b, t, B_WIDTH), BF16)]
        out_specs += [row(d), row(wkv), pl.BlockSpec((None, B_KV_WIDTH, tm), lambda i, j: (i, 0, j)), row(B_WIDTH)]
    res = pl.pallas_call(
        functools.partial(_inproj_kernel, sub=sub, attn=attn),
        grid=(b, t // tm),
        in_specs=in_specs,
        out_specs=out_specs,
        out_shape=out_shape,
        scratch_shapes=[pltpu.VMEM((tm, d), BF16)],
        compiler_params=_params("parallel", "parallel"),
        name="inproj",
    )(*args)
    return res if attn else res[0]


def _matmul_kernel(u_ref, w_ref, p_ref):
    p_ref[...] = jnp.dot(u_ref[...], w_ref[...], preferred_element_type=F32)


def _matmul(u, w):
    r, d = u.shape
    n = w.shape[1]
    tm = 4 * ROW_TILE
    return pl.pallas_call(
        _matmul_kernel,
        grid=(r // tm,),
        in_specs=[pl.BlockSpec((tm, d), lambda i: (i, 0)),
                  pl.BlockSpec((d, n), lambda i: (0, 0), pipeline_mode=pl.Buffered(1))],
        out_specs=pl.BlockSpec((tm, n), lambda i: (i, 0)),
        out_shape=jax.ShapeDtypeStruct((r, n), F32),
        compiler_params=_params("parallel"),
        name="matmul_tm",
    )(u, w)


def _scan_tile(i, n_ctx, n_lat, rev):
    if not rev:
        return i
    return jnp.where(i < n_ctx, n_ctx - 1 - i, 2 * n_ctx + n_lat - 1 - i)


def _rglru_kernel(*refs, rev, tt, n_ctx, n_lat):
    if rev:
        u_ref, ga_ref, yf_ref, w_ref, bx_ref, ba_ref, la_ref, out_ref, h_scr, a_scr, b_scr = refs
    else:
        (xa_ref, xp_ref, xn1_ref, xn2_ref, cw_ref, cb_ref, w_ref, bx_ref, ba_ref,
         la_ref, out_ref, u_ref, h_scr, xpad_scr, a_scr, b_scr) = refs
    i = pl.program_id(0)
    rows = tt * SUBLANES

    @pl.when(i == 0)
    def _():
        h_scr[...] = jnp.zeros_like(h_scr)

    if not rev:
        tile = _scan_tile(i, n_ctx, n_lat, rev)
        seg_first = jnp.logical_or(tile == 0, tile == n_ctx)
        seg_last = jnp.logical_or(tile == n_ctx - 1, tile == n_ctx + n_lat - 1)
        xpad_scr[0:8, :] = jnp.where(seg_first, 0.0, xp_ref[...])
        xpad_scr[8:8 + rows, :] = xa_ref[...]
        xpad_scr[8 + rows:16 + rows, :] = jnp.where(seg_last, 0.0, xn1_ref[...])
        xpad_scr[16 + rows:24 + rows, :] = jnp.where(seg_last, 0.0, xn2_ref[...])
        u = cb_ref[...] + xpad_scr[0:rows, :] * cw_ref[0:1, :]
        for k in range(1, A_CONV):
            u = u + xpad_scr[8 * k:8 * k + rows, :] * cw_ref[k:k + 1, :]
        u_ref[...] = u

    for hd in range(A_HEADS):
        sl = slice(hd * A_BLOCK, (hd + 1) * A_BLOCK)
        uh = u_ref[:, sl]
        g = jnp.dot(uh.astype(BF16), w_ref[hd], preferred_element_type=F32)
        gate_x = 0.5 * jnp.tanh(g[:, :A_BLOCK] + bx_ref[:, sl]) + 0.5
        lah = la_ref[:, sl]
        log_a = lah + lah * jnp.tanh(g[:, A_BLOCK:] + ba_ref[:, sl])
        a = jnp.exp(log_a)
        a_scr[:, sl] = a
        b_scr[:, sl] = jnp.sqrt(-jnp.tanh(log_a) * (1.0 + a * a)) * gate_x * uh

    def step(s, h):
        t = (tt - 1 - s) if rev else s
        r0 = pl.multiple_of(t * SUBLANES, SUBLANES)
        h = a_scr[pl.ds(r0, SUBLANES), :] * h + b_scr[pl.ds(r0, SUBLANES), :]
        b_scr[pl.ds(r0, SUBLANES), :] = h
        return h

    h_scr[...] = lax.fori_loop(0, tt, step, h_scr[...], unroll=8)
    if rev:
        out_ref[...] = ((yf_ref[...] + b_scr[...]) * _silu(ga_ref[...])).astype(BF16)
    else:
        out_ref[...] = b_scr[...]


def _rglru_pass(p_a, fwd, conv_w, conv_b, w_cat, bx, ba, la, t_ctx, t_lat, rev):
    w = A_WIDTH
    tt = 128
    rows = tt * SUBLANES
    n_ctx, n_lat = t_ctx // tt, t_lat // tt
    t_all = t_ctx + t_lat
    tile = functools.partial(_scan_tile, n_ctx=n_ctx, n_lat=n_lat, rev=rev)
    const = lambda i: (0, 0)
    blk = lambda col: pl.BlockSpec((rows, w), lambda i: (tile(i), col))
    if rev:
        yf, u = fwd
        in_specs = [blk(0), blk(1), blk(0)]
        args = [u, p_a, yf]
    else:
        in_specs = [blk(0),
                    pl.BlockSpec((SUBLANES, w), lambda i: (jnp.maximum(tile(i) * tt - 1, 0), 0)),
                    pl.BlockSpec((SUBLANES, w), lambda i: (jnp.minimum(tile(i) * tt + tt, t_all - 1), 0)),
                    pl.BlockSpec((SUBLANES, w), lambda i: (jnp.minimum(tile(i) * tt + tt + 1, t_all - 1), 0)),
                    pl.BlockSpec((A_CONV, w), const), pl.BlockSpec((1, w), const)]
        args = [p_a, p_a, p_a, p_a, conv_w, conv_b]
    in_specs += [pl.BlockSpec((A_HEADS, A_BLOCK, 2 * A_BLOCK), lambda i: (0, 0, 0)),
                 pl.BlockSpec((1, w), const), pl.BlockSpec((1, w), const), pl.BlockSpec((1, w), const)]
    args += [w_cat, bx, ba, la]
    full = jax.ShapeDtypeStruct((t_all * SUBLANES, w), F32)
    scratch = [pltpu.VMEM((SUBLANES, w), F32)]
    if not rev:
        scratch.append(pltpu.VMEM((rows + 3 * SUBLANES, w), F32))
    scratch += [pltpu.VMEM((rows, w), F32), pltpu.VMEM((rows, w), F32)]
    return pl.pallas_call(
        functools.partial(_rglru_kernel, rev=rev, tt=tt, n_ctx=n_ctx, n_lat=n_lat),
        grid=(n_ctx + n_lat,),
        in_specs=in_specs,
        out_specs=blk(0) if rev else [blk(0), blk(0)],
        out_shape=jax.ShapeDtypeStruct((t_all * SUBLANES, w), BF16) if rev else [full, full],
        scratch_shapes=scratch,
        compiler_params=_params("arbitrary"),
        name="rglru_bwd" if rev else "rglru_fwd",
    )(*args)


def _attn_kernel(sink_ref, q_ref, kc_ref, kp_ref, ko_ref, kn_ref, vc_ref, vp_ref, vo_ref, vn_ref, gb_ref,
                 o_ref, s_scr, p_scr, bias_scr, *, n_ctx_blk, n_lat_blk):
    n = pl.program_id(1)
    is_lat = n >= n_ctx_blk

    @pl.when(is_lat)
    def _():
        _attend(True, n - n_ctx_blk, n_lat_blk, sink_ref, q_ref, (kc_ref, kp_ref, ko_ref, kn_ref),
                (vc_ref, vp_ref, vo_ref, vn_ref), gb_ref, o_ref, s_scr, p_scr, bias_scr)

    @pl.when(jnp.logical_not(is_lat))
    def _():
        _attend(False, None, None, sink_ref, q_ref, (kc_ref,), (vc_ref,), gb_ref, o_ref, s_scr, p_scr, bias_scr)


def _attend(window, m, n_lat_blk, sink_ref, q_ref, k_refs, v_refs, gb_ref, o_ref, s_scr, p_scr, bias_scr):
    w = WINDOW
    dh = B_HEAD_DIM
    lc = k_refs[0].shape[0]

    low = lax.broadcasted_iota(jnp.int32, (w, LANES), 1) < dh
    segs = [(0, lc, None)]
    if window:
        kj = lax.broadcasted_iota(jnp.int32, (w, w), 0)
        qi = lax.broadcasted_iota(jnp.int32, (w, w), 1)
        neg = jnp.full((w, w), NEG_BIG, F32)
        zer = jnp.zeros((w, w), F32)
        bias_scr[0] = jnp.where(jnp.logical_and(m >= 1, kj >= qi), zer, neg)
        bias_scr[1] = jnp.where(jnp.logical_and(m <= n_lat_blk - 2, kj <= qi), zer, neg)
        segs += [(lc, w, 0), (lc + w, w, None), (lc + 2 * w, w, 1)]
    nk = segs[-1][0] + segs[-1][1]
    piece = 32
    pieces = [(r0, bi, p0) for r0, nr, bi in segs for p0 in range(0, nr, piece)]

    def scores(g, cb, r0, bi, p0):
        x = s_scr[g, r0 + p0:r0 + p0 + piece, cb]
        return x if bi is None else x + bias_scr[bi, p0:p0 + piece, :]

    def tree(xs, op):
        while len(xs) > 1:
            xs = [op(xs[i], xs[i + 1]) for i in range(0, len(xs) - 1, 2)] + ([xs[-1]] if len(xs) % 2 else [])
        return xs[0]

    def fold8(x, op):
        return op(x.reshape(x.shape[0] // SUBLANES, SUBLANES, x.shape[1]), axis=0)

    def score_stage(g):
        parts = []
        for c in range(2):
            col0 = g * B_GROUP * dh + c * LANES
            qc = q_ref[:, col0:col0 + LANES].astype(F32)
            parts.append(jnp.where(low, qc, 0.0).astype(BF16))
            parts.append(jnp.where(low, pltpu.roll(qc, dh, 1), 0.0).astype(BF16))
        qs = jnp.concatenate(parts, axis=0)
        ks = slice(g * LANES, (g + 1) * LANES)
        kv_all = jnp.concatenate([r[:, ks] for r in k_refs], axis=0)
        s_scr[g, 0:nk, :] = lax.dot_general(kv_all, qs, _NT, preferred_element_type=F32)

    def softmax_stage(g):
        rdens = []
        for h in range(B_GROUP):
            cb = slice(h * w, (h + 1) * w)
            snk = sink_ref[g * B_GROUP + h] * LOG2E
            mx8 = tree([fold8(scores(g, cb, r0, bi, p0), jnp.max) for r0, bi, p0 in pieces], jnp.maximum)
            mx = jnp.maximum(jnp.max(mx8, axis=0, keepdims=True), snk)
            sums = []
            for r0, bi, p0 in pieces:
                p = jnp.exp2(scores(g, cb, r0, bi, p0) - mx)
                sums.append(fold8(p, jnp.sum))
                p_scr[g, r0 + p0:r0 + p0 + piece, cb] = p.astype(BF16)
            den8 = tree(sums, jnp.add)
            rdens.append(1.0 / (jnp.sum(den8, axis=0, keepdims=True) + jnp.exp2(snk - mx)))
        return rdens

    def value_stage(g, rdens):
        vt_all = jnp.concatenate([r[g * dh:(g + 1) * dh, :] for r in v_refs], axis=1)
        ov = jnp.dot(vt_all, p_scr[g, 0:nk, :], preferred_element_type=F32) * jnp.concatenate(rdens, axis=1)
        for c in range(2):
            pair = jnp.concatenate([ov[:, (2 * c) * w:(2 * c + 1) * w], ov[:, (2 * c + 1) * w:(2 * c + 2) * w]],
                                   axis=0)
            col0 = g * B_GROUP * dh + c * LANES
            y = pair.T * _silu(gb_ref[:, col0:col0 + LANES])
            o_ref[:, col0:col0 + LANES] = y.astype(BF16)

    rd = {}
    for step in range(B_KV_HEADS + 2):
        if step < B_KV_HEADS:
            score_stage(step)
        if 0 <= step - 1 < B_KV_HEADS:
            rd[step - 1] = softmax_stage(step - 1)
        if 0 <= step - 2 < B_KV_HEADS:
            value_stage(step - 2, rd[step - 2])


def _attention(q, gate_b, kvr, vt, sink, t_ctx):
    b, t, _ = q.shape
    w = WINDOW
    n_ctx_blk, n_lat_blk = t_ctx // w, (t - t_ctx) // w
    nb = t // w
    wkv = 2 * B_KV_WIDTH
    prev = lambda n: jnp.maximum(n - 1, 0)
    nxt = lambda n: jnp.minimum(n + 1, nb - 1)
    kvb = lambda f: pl.BlockSpec((None, w, wkv), lambda i, n, s: (i, f(n), 0))
    vtb = lambda f: pl.BlockSpec((None, B_KV_WIDTH, w), lambda i, n, s: (i, 0, f(n)))
    grid_spec = pltpu.PrefetchScalarGridSpec(
        num_scalar_prefetch=1,
        grid=(b, nb),
        in_specs=[pl.BlockSpec((None, w, B_WIDTH), lambda i, n, s: (i, n, 0)),
                  pl.BlockSpec((None, t_ctx, wkv), lambda i, n, s: (i, 0, 0)),
                  kvb(prev), kvb(lambda n: n), kvb(nxt),
                  pl.BlockSpec((None, B_KV_WIDTH, t_ctx), lambda i, n, s: (i, 0, 0)),
                  vtb(prev), vtb(lambda n: n), vtb(nxt),
                  pl.BlockSpec((None, w, B_WIDTH), lambda i, n, s: (i, n, 0))],
        out_specs=pl.BlockSpec((None, w, B_WIDTH), lambda i, n, s: (i, n, 0)),
        scratch_shapes=[pltpu.VMEM((B_KV_HEADS, t_ctx + 3 * w, B_GROUP * w), F32),
                        pltpu.VMEM((B_KV_HEADS, t_ctx + 3 * w, B_GROUP * w), BF16),
                        pltpu.VMEM((2, w, w), F32)],
    )
    return pl.pallas_call(
        functools.partial(_attn_kernel, n_ctx_blk=n_ctx_blk, n_lat_blk=n_lat_blk),
        grid_spec=grid_spec,
        out_shape=jax.ShapeDtypeStruct((b, t, B_WIDTH), BF16),
        compiler_params=_params("parallel", "parallel"),
        name="window_attn",
    )(sink, q, kvr, kvr, kvr, kvr, vt, vt, vt, vt, gate_b)


GLA_DIAG = 4
GLA_LEVELS = (4, 8, 16, 32)
GLA_EXP_KINDS = 2 + len(GLA_LEVELS)
GLA_UNROLL = 2


def _gla_region_ids(rev):
    c = C_CHUNK
    ri = lax.broadcasted_iota(jnp.int32, (c, c), 0)
    ci = lax.broadcasted_iota(jnp.int32, (c, c), 1)
    reg = jnp.full((c, c), len(GLA_LEVELS), jnp.int32)
    for li in range(len(GLA_LEVELS) - 1, -1, -1):
        m = GLA_LEVELS[li]
        reg = jnp.where(ri // m == ci // m, li, reg)
    seen = (ci >= ri) if rev else (ci <= ri)
    return jnp.where(seen, reg, -1)


def _gla_exponent_matrix(rev):
    c = C_CHUNK
    t = np.arange(c)[:, None]
    s = np.arange(c)[None, :]
    upto = (s >= t) if rev else (s <= t)
    after = ~upto
    mats = [upto, after]
    for m in GLA_LEVELS:
        later = ((t // m) % 2 == 0) if rev else ((t // m) % 2 == 1)
        mats.append((t // m == s // m) & np.where(later, upto, after))
    mat = np.concatenate(mats, axis=0).astype(np.float32)
    return np.concatenate([mat, mat], axis=1)


def _grp(x, j):
    return x[j * SUBLANES:(j + 1) * SUBLANES, :]


def _gla_stage_a1(z, qr, lb, emat, rev, lane):
    c = C_CHUNK
    ng = c // SUBLANES

    half = 0.5 * (1.0 - lb)
    ht = half * jnp.tanh(z)
    f = (lb + half) + ht
    kk = half - ht
    ft = jnp.maximum(f, F_MIN)
    q = _silu(qr)

    g2 = jnp.log2(ft)
    g_hi = g2.astype(BF16)
    g_lo = (g2 - g_hi.astype(F32)).astype(BF16)
    expo = jnp.dot(emat, jnp.concatenate([g_hi, g_lo], axis=0), preferred_element_type=F32)

    dparts = []
    for j in range(ng):
        fg, qg, kg = _grp(ft, j), _grp(q, j), _grp(kk, j)
        w_ = kg
        acc = jnp.where(lane == 0, jnp.sum(qg * kg, axis=-1, keepdims=True), 0.0)
        for dlt in range(1, GLA_DIAG):
            w_ = fg * pltpu.roll(w_, (SUBLANES - 1) if rev else 1, 0)
            red = jnp.sum(qg * w_, axis=-1, keepdims=True)
            acc = jnp.where(lane == (dlt if rev else LANES - dlt), red, acc)
        dparts.append(acc)
    diag = pltpu.roll(jnp.concatenate(dparts, axis=0), 0, 1, stride=1, stride_axis=0)[:, :c]
    return q, kk, expo, diag


def _gla_stage_a2(q, kk, expo, diag, v, rev, reg, row8):
    c = C_CHUNK
    ng = c // SUBLANES
    dec = jnp.exp2(expo)
    last = 0 if rev else c - 1
    total = dec[last:last + 1, :]
    qd = (q * dec[0:c]).astype(BF16)
    kd = (kk * dec[c:2 * c]).astype(BF16)
    grp = _grp

    zero = jnp.zeros((SUBLANES, LANES), F32)
    levels = []
    for li, m in enumerate(GLA_LEVELS):
        fac = dec[(2 + li) * c:(3 + li) * c]
        qparts, kparts = [], []
        for j in range(ng):
            if m >= SUBLANES:
                later = ((j * SUBLANES // m) % 2 == 1) != rev
                prod = (grp(q, j) if later else grp(kk, j)) * grp(fac, j)
                qparts.append(prod if later else zero)
                kparts.append(zero if later else prod)
            else:
                later = ((row8 // m) % 2 == 0) if rev else ((row8 // m) % 2 == 1)
                prod = jnp.where(later, grp(q, j), grp(kk, j)) * grp(fac, j)
                qparts.append(jnp.where(later, prod, 0.0))
                kparts.append(jnp.where(later, 0.0, prod))
        levels.append(lax.dot_general(jnp.concatenate(qparts, axis=0).astype(BF16),
                                      jnp.concatenate(kparts, axis=0).astype(BF16), _NT,
                                      preferred_element_type=F32))

    attn = jnp.zeros((c, c), F32)
    for li in range(len(GLA_LEVELS) - 1, -1, -1):
        attn = jnp.where(reg == li + 1, levels[li], attn)
    attn = jnp.where(reg == 0, diag, attn)
    return attn.astype(BF16), qd, kd, v.astype(BF16), total


def _gla_stage_b(attn, qd, kd, vb, total, st):
    o = jnp.dot(attn, vb, preferred_element_type=F32)
    o = o + lax.dot_general(qd, st.astype(BF16), _NT, preferred_element_type=F32)
    st_new = total * st + lax.dot_general(vb, kd, _TN, preferred_element_type=F32)
    return o, st_new


def _gla_kernel(zf_ref, zb_ref, v_ref, q_ref, og_ref, lb_ref, gw_ref, emat_ref, y_ref, of_scr, ob_scr, st_scr,
                attn_c, qd_c, kd_c, vb_c, tot_c, q_c, k_c, expo_c, diag_c,
                *, n_ctx, n_lat, heads):
    c = C_CHUNK
    row8 = lax.broadcasted_iota(jnp.int32, (SUBLANES, LANES), 0)
    lane = lax.broadcasted_iota(jnp.int32, (SUBLANES, LANES), 1)
    regs = (_gla_region_ids(False), _gla_region_ids(True))
    n_all = n_ctx + n_lat
    st_scr[...] = jnp.zeros_like(st_scr)
    streams = [(hh, rev) for hh in range(heads) for rev in (False, True)]

    def rows_of(i, rev):
        if isinstance(i, int):
            cidx = ((n_ctx - 1 - i) if i < n_ctx else (2 * n_ctx + n_lat - 1 - i)) if rev else i
            return pl.ds(cidx * c, c)
        cidx = jnp.where(i < n_ctx, n_ctx - 1 - i, 2 * n_ctx + n_lat - 1 - i) if rev else i
        return pl.ds(pl.multiple_of(cidx * c, c), c)

    def stage_a1(i):
        for k, (hh, rev) in enumerate(streams):
            ls = slice(hh * LANES, (hh + 1) * LANES)
            rows = rows_of(i, rev)
            z_ref = zb_ref if rev else zf_ref
            q, kk, expo, diag = _gla_stage_a1(z_ref[rows, ls], q_ref[rows, ls], lb_ref[:, ls],
                                              emat_ref[int(rev)], rev, lane)
            q_c[k] = q
            k_c[k] = kk
            expo_c[k] = expo
            diag_c[k] = diag

    def stage_a2(i):
        for k, (hh, rev) in enumerate(streams):
            ls = slice(hh * LANES, (hh + 1) * LANES)
            attn, qd, kd, vb, total = _gla_stage_a2(q_c[k], k_c[k], expo_c[k], diag_c[k],
                                                    v_ref[rows_of(i, rev), ls], rev, regs[int(rev)], row8)
            attn_c[k] = attn
            qd_c[k] = qd
            kd_c[k] = kd
            vb_c[k] = vb
            tot_c[k] = jnp.broadcast_to(total, (SUBLANES, LANES))

    def stage_b(i):
        for k, (hh, rev) in enumerate(streams):
            ls = slice(hh * LANES, (hh + 1) * LANES)
            o, st = _gla_stage_b(attn_c[k], qd_c[k], kd_c[k], vb_c[k], tot_c[k, 0:1, :], st_scr[k])
            (ob_scr if rev else of_scr)[rows_of(i, rev), ls] = o
            st_scr[k] = st

    stage_a1(0)
    stage_a2(0)
    stage_a1(1)

    def step(i):
        stage_b(i - 2)
        stage_a2(i - 1)
        stage_a1(i)

    def body(j, carry):
        for r in range(GLA_UNROLL):
            step(2 + GLA_UNROLL * j + r)
        return carry

    trips = (n_all - 2) // GLA_UNROLL
    lax.fori_loop(0, trips, body, 0)
    for i in range(2 + trips * GLA_UNROLL, n_all):
        step(i)
    stage_b(n_all - 2)
    stage_a2(n_all - 1)
    stage_b(n_all - 1)

    gw = gw_ref[...]
    piece = PROJ_SUB * ROW_TILE

    def fin(i, carry):
        rows = pl.ds(pl.multiple_of(i * piece, piece), piece)
        for hh in range(heads):
            ls = slice(hh * LANES, (hh + 1) * LANES)
            o = of_scr[rows, ls] + ob_scr[rows, ls]
            y = o * lax.rsqrt(jnp.mean(o * o, axis=-1, keepdims=True) + RMS_EPS) * gw
            y_ref[rows, ls] = (y * _silu(og_ref[rows, ls])).astype(BF16)
        return carry

    lax.fori_loop(0, (n_all * c) // piece, fin, 0)


def _gla(p, lb, gw, t_ctx):
    b, t, _ = p.shape
    hps = GLA_HEADS_PER_STEP
    wb = hps * LANES
    nblk = C_HEADS // hps
    col = lambda k: pl.BlockSpec((None, t, wb), lambda i, h: (i, 0, k * nblk + h))
    emat = jnp.asarray(np.stack([_gla_exponent_matrix(False), _gla_exponent_matrix(True)]), BF16)
    return pl.pallas_call(
        functools.partial(_gla_kernel, n_ctx=t_ctx // C_CHUNK, n_lat=(t - t_ctx) // C_CHUNK, heads=hps),
        grid=(b, nblk),
        in_specs=[col(0), col(1), col(2), col(3), col(4),
                  pl.BlockSpec((None, 1, wb), lambda i, h: (h, 0, 0)),
                  pl.BlockSpec((1, LANES), lambda i, h: (0, 0)),
                  pl.BlockSpec(emat.shape, lambda i, h: (0, 0, 0))],
        out_specs=pl.BlockSpec((None, t, wb), lambda i, h: (i, 0, h)),
        out_shape=jax.ShapeDtypeStruct((b, t, C_HEADS * LANES), BF16),
        scratch_shapes=[pltpu.VMEM((t, wb), F32), pltpu.VMEM((t, wb), F32),
                        pltpu.VMEM((2 * hps, C_EXPAND, LANES), F32),
                        pltpu.VMEM((2 * hps, C_CHUNK, C_CHUNK), BF16),
                        pltpu.VMEM((2 * hps, C_CHUNK, LANES), BF16),
                        pltpu.VMEM((2 * hps, C_CHUNK, LANES), BF16),
                        pltpu.VMEM((2 * hps, C_CHUNK, LANES), BF16),
                        pltpu.VMEM((2 * hps, SUBLANES, LANES), F32),
                        pltpu.VMEM((2 * hps, C_CHUNK, LANES), F32),
                        pltpu.VMEM((2 * hps, C_CHUNK, LANES), F32),
                        pltpu.VMEM((2 * hps, GLA_EXP_KINDS * C_CHUNK, LANES), F32),
                        pltpu.VMEM((2 * hps, C_CHUNK, C_CHUNK), F32)],
        compiler_params=_params("parallel", "parallel"),
        name="hgrn2_gla",
    )(p, p, p, p, p, lb.reshape(nblk, 1, wb), gw.reshape(1, LANES), emat)


def _outproj_kernel(*refs, n_in, sub, final):
    y_refs = refs[:n_in]
    w_ref, h_ref = refs[n_in:n_in + 2]
    gt_refs = refs[n_in + 2:n_in + 2 + sub]
    if final:
        fw_ref, o_ref = refs[n_in + 2 + sub:]
    else:
        (o_ref,) = refs[n_in + 2 + sub:]
    acc = None
    k0 = 0
    for y_ref in y_refs:
        kw = y_ref.shape[-1]
        part = jnp.dot(y_ref[...], w_ref[k0:k0 + kw, :], preferred_element_type=F32)
        acc = part if acc is None else acc + part
        k0 += kw
    d = acc.shape[-1]
    r = ROW_TILE
    for k in range(sub):
        a3 = acc[k * r:(k + 1) * r, :].reshape(r // SUBLANES, SUBLANES, d)
        hn = h_ref[k * r:(k + 1) * r, :] + (a3 * gt_refs[k][...][None]).reshape(r, d)
        if final:
            hn = hn * lax.rsqrt(jnp.mean(hn * hn, axis=-1, keepdims=True) + RMS_EPS) * fw_ref[...]
        o_ref[k * r:(k + 1) * r, :] = hn


def _outproj(ys, w, h, gate, n_ctx_tiles, final_w=None):
    b, t, d = h.shape
    final = final_w is not None
    sub = 1 if final else PROJ_SUB
    tm = sub * ROW_TILE
    off = n_ctx_tiles if final else 0
    nt = t // tm - off
    in_specs = [pl.BlockSpec((None, tm, y.shape[-1]), lambda i, j: (i, j + off, 0)) for y in ys]
    in_specs += [pl.BlockSpec(w.shape, lambda i, j: (0, 0), pipeline_mode=pl.Buffered(1)),
                 pl.BlockSpec((None, tm, d), lambda i, j: (i, j + off, 0))]
    in_specs += _mod_specs(b, n_ctx_tiles, sub, off)
    args = list(ys) + [w, h] + [gate] * sub
    if final:
        in_specs.append(pl.BlockSpec((1, d), lambda i, j: (0, 0)))
        args.append(final_w.reshape(1, d))
    return pl.pallas_call(
        functools.partial(_outproj_kernel, n_in=len(ys), sub=sub, final=final),
        grid=(b, nt),
        in_specs=in_specs,
        out_specs=pl.BlockSpec((None, tm, d), lambda i, j: (i, j, 0)),
        out_shape=jax.ShapeDtypeStruct((b, nt * tm, d), F32),
        compiler_params=_params("parallel", "parallel"),
        name="outproj",
    )(*args)


def _rope_tables(s_lat, t_ctx):
    pos = np.arange(s_lat)
    row = (pos // GRID_W).astype(np.float32)
    colp = (pos % GRID_W).astype(np.float32)
    axis_dim = B_HEAD_DIM // 2
    inv = jnp.asarray(ROPE_BASE, F32) ** (-jnp.arange(0, axis_dim, 2, dtype=F32) / axis_dim)
    ang = jnp.concatenate([jnp.asarray(row)[:, None] * inv, jnp.asarray(colp)[:, None] * inv], axis=-1)
    cos, sin = jnp.cos(ang), jnp.sin(ang)
    zer = jnp.zeros_like(sin)
    pairs = lambda even, odd: jnp.stack([even, odd], axis=-1).reshape(s_lat, B_HEAD_DIM)
    cos64 = pairs(cos, cos)
    sa64 = pairs(-sin, zer)
    sb64 = pairs(zer, sin)
    one64 = jnp.ones_like(cos64)
    zer64 = jnp.zeros_like(cos64)

    def full(lat128, ident):
        return jnp.concatenate([jnp.broadcast_to(ident, (t_ctx, LANES)), lat128], axis=0)

    ident_c = jnp.ones((1, LANES), F32)
    ident_s = jnp.zeros((1, LANES), F32)
    q_tabs = (full(jnp.concatenate([cos64, cos64], -1), ident_c),
              full(jnp.concatenate([sa64, sa64], -1), ident_s),
              full(jnp.concatenate([sb64, sb64], -1), ident_s))
    kv_tabs = (full(jnp.concatenate([cos64, one64], -1), ident_c),
               full(jnp.concatenate([sa64, zer64], -1), ident_s),
               full(jnp.concatenate([sb64, zer64], -1), ident_s))
    return q_tabs, kv_tabs


def _even_weights(w_in):
    dh = B_HEAD_DIM
    n_state = A_WIDTH + 2 * B_KV_WIDTH
    q0 = n_state + A_WIDTH
    k0 = A_WIDTH
    v0 = A_WIDTH + B_KV_WIDTH
    parts = [w_in[:, q0:q0 + 2 * B_WIDTH]]
    for j in range(B_KV_HEADS):
        parts += [w_in[:, k0 + j * dh:k0 + (j + 1) * dh], w_in[:, v0 + j * dh:v0 + (j + 1) * dh]]
    w_b = jnp.concatenate(parts, axis=1).astype(BF16)
    w_a = jnp.concatenate([w_in[:, :A_WIDTH], w_in[:, n_state:n_state + A_WIDTH]], axis=1).astype(BF16)
    return w_b, w_a


def _groups(vec_b, vec_c):
    allv = jnp.concatenate([vec_b, vec_c[None, :]], axis=0)
    return jnp.broadcast_to(allv[:, None, :], (allv.shape[0], SUBLANES, allv.shape[1]))


def kernel(x, c, ctx, c_ctx, ada_w, ada_b, norm_w, ev_w_in, ev_conv_w, ev_conv_b, ev_rg_wx, ev_rg_bx,
           ev_rg_wa, ev_rg_ba, ev_rg_lambda, ev_sink, ev_w_out, od_w_in, od_lb_raw, od_gnorm_w, od_w_out,
           final_norm_w):
    b, s_lat, d = x.shape
    t_ctx = ctx.shape[1]
    t_all = t_ctx + s_lat
    assert b == SUBLANES and d == D_MODEL
    assert t_ctx % ROW_TILE == 0 and s_lat % ROW_TILE == 0 and t_all % (PROJ_SUB * ROW_TILE) == 0
    n_ctx_tiles = t_ctx // ROW_TILE

    sc_rows = jnp.zeros((2 * SUBLANES, d), F32)
    sc_rows = sc_rows.at[:b].set(jax.nn.silu(c)).at[b].set(jax.nn.silu(c_ctx))
    mod = _ada_mod(sc_rows, ada_w, ada_b)

    lb_p = jax.nn.softmax(od_lb_raw.astype(F32), axis=0)
    lower_bounds = jnp.cumsum(lb_p, axis=0) - lb_p[0]
    q_tabs, kv_tabs = _rope_tables(s_lat, t_ctx)

    h = jnp.concatenate([ctx, x], axis=1)
    out = None
    for layer in range(DEPTH):
        last = layer == DEPTH - 1
        j = layer // 2
        ml = mod[layer]
        shift = _groups(ml[:b, :d], ml[b, :d])
        scale = _groups(ml[:b, d:2 * d], ml[b, d:2 * d])
        gate = _groups(ml[:b, 2 * d:], ml[b, 2 * d:])
        if layer % 2 == 0:
            w_b, w_a = _even_weights(ev_w_in[j])
            gate_b, u, kvr, vt, q = _inproj(h, shift, scale, norm_w[layer], w_b, n_ctx_tiles,
                                            kv_tabs=kv_tabs, q_tabs=q_tabs)
            u_tm = jnp.transpose(u, (1, 0, 2)).reshape(t_all * b, d)
            p_a = _matmul(u_tm, w_a)
            lah = -0.5 * A_C * jax.nn.softplus(-ev_rg_lambda[j].astype(F32))
            w_cat = (0.5 * jnp.concatenate([ev_rg_wx[j], ev_rg_wa[j]], axis=-1)).astype(BF16)
            rg = lambda dr, yf: _rglru_pass(
                p_a, yf, ev_conv_w[j], ev_conv_b[j].reshape(1, -1), w_cat[dr],
                0.5 * ev_rg_bx[j, dr].reshape(1, -1), 0.5 * ev_rg_ba[j, dr].reshape(1, -1), lah[dr].reshape(1, -1),
                t_ctx, s_lat, rev=bool(dr))
            ya = jnp.transpose(rg(1, rg(0, None)).reshape(t_all, b, A_WIDTH), (1, 0, 2))
            yb = _attention(q, gate_b, kvr, vt, ev_sink[j].astype(F32), t_ctx)
            ys, w_out = [ya, yb], ev_w_out[j].astype(BF16)
        else:
            w_odd = od_w_in[j].at[:, :2 * C_HEADS * C_EXPAND].multiply(0.5).astype(BF16)
            p = _inproj(h, shift, scale, norm_w[layer], w_odd, n_ctx_tiles)
            ys, w_out = [_gla(p, lower_bounds[j], od_gnorm_w[j], t_ctx)], od_w_out[j].astype(BF16)
        if last:
            out = _outproj(ys, w_out, h, gate, n_ctx_tiles, final_w=final_norm_w)
        else:
            h = _outproj(ys, w_out, h, gate, n_ctx_tiles)
    return out
```

```python
import functools

import jax
import jax.numpy as jnp
import numpy as np
from jax import lax
from jax.experimental import pallas as pl
from jax.experimental.pallas import tpu as pltpu

F32 = jnp.float32
BF16 = jnp.bfloat16

D_MODEL = 1024
DEPTH = 4
GRID_W = 64
RMS_EPS = 1e-6
NEG_BIG = -1e30
F_MIN = 1e-30

A_WIDTH = D_MODEL
A_HEADS = 8
A_BLOCK = A_WIDTH // A_HEADS
A_CONV = 4
A_C = 8.0
B_HEADS = 16
B_KV_HEADS = 4
B_HEAD_DIM = 64
B_GROUP = B_HEADS // B_KV_HEADS
B_WIDTH = B_HEADS * B_HEAD_DIM
B_KV_WIDTH = B_KV_HEADS * B_HEAD_DIM
WINDOW = 128
ROPE_BASE = 10000.0
ATTN_SCALE = B_HEAD_DIM ** -0.5
LOG2E = 1.4426950408889634
C_EXPAND = 128
C_HEADS = D_MODEL // C_EXPAND
C_CHUNK = 64

SUBLANES = 8
LANES = 128
ROW_TILE = 256
PROJ_SUB = 3
GLA_HEADS_PER_STEP = 2
VMEM_LIMIT = 56 * 1024 * 1024

_NT = (((1,), (1,)), ((), ()))
_TN = (((0,), (0,)), ((), ()))


def _params(*sem, fuse=None):
    return pltpu.CompilerParams(dimension_semantics=sem, vmem_limit_bytes=VMEM_LIMIT, allow_input_fusion=fuse)


def _sigmoid(x):
    return 0.5 * jnp.tanh(0.5 * x) + 0.5


def _silu(x):
    h = 0.5 * x
    return h + h * jnp.tanh(h)


def _ada_kernel(sc_ref, w_ref, b_ref, o_ref):
    o_ref[...] = jnp.dot(sc_ref[...], w_ref[...], preferred_element_type=F32,
                         precision=lax.Precision.HIGHEST) + b_ref[...]


def _ada_mod(sc, ada_w, ada_b):
    depth, d, n = ada_w.shape
    rows = sc.shape[0]
    tn = 1024
    return pl.pallas_call(
        _ada_kernel,
        grid=(depth, n // tn),
        in_specs=[pl.BlockSpec((rows, d), lambda l, j: (0, 0)),
                  pl.BlockSpec((None, d, tn), lambda l, j: (l, 0, j)),
                  pl.BlockSpec((None, 1, tn), lambda l, j: (l, 0, j))],
        out_specs=pl.BlockSpec((None, rows, tn), lambda l, j: (l, 0, j)),
        out_shape=jax.ShapeDtypeStruct((depth, rows, n), F32),
        compiler_params=_params("parallel", "parallel"),
        name="ada_mod",
    )(sc, ada_w, ada_b.reshape(depth, 1, n))


def _mod_specs(b, n_ctx_tiles, sub, off=0):
    def spec(k):
        return pl.BlockSpec((None, SUBLANES, D_MODEL),
                            lambda i, j: (jnp.where(sub * j + k + off < n_ctx_tiles, b, i), 0, 0))
    return [spec(k) for k in range(sub)]


def _rope(x, cos, sin_a, sin_b):
    return x * cos + pltpu.roll(x, LANES - 1, 1) * sin_a + pltpu.roll(x, 1, 1) * sin_b


def _inproj_kernel(*refs, sub, attn):
    h_ref = refs[0]
    sh_refs, sc_refs = refs[1:1 + sub], refs[1 + sub:1 + 2 * sub]
    nw_ref, w_ref = refs[1 + 2 * sub:3 + 2 * sub]
    rest = refs[3 + 2 * sub:]
    if attn:
        cos_ref, sa_ref, sb_ref, qcos_ref, qsa_ref, qsb_ref, p_ref, u_ref, kv_ref, vt_ref, q_ref, u_scr = rest
    else:
        p_ref, u_scr = rest
    d = h_ref.shape[-1]
    r = ROW_TILE
    for k in range(sub):
        x = h_ref[k * r:(k + 1) * r, :]
        y = x * lax.rsqrt(jnp.mean(x * x, axis=-1, keepdims=True) + RMS_EPS) * nw_ref[...]
        y3 = y.reshape(r // SUBLANES, SUBLANES, d)
        u = (y3 * (1.0 + sc_refs[k][...])[None] + sh_refs[k][...][None]).reshape(r, d)
        u_scr[k * r:(k + 1) * r, :] = u.astype(BF16)
        if not attn:
            p_ref[k * r:(k + 1) * r, :] = jnp.dot(u_scr[k * r:(k + 1) * r, :], w_ref[...],
                                                  preferred_element_type=F32)
    n = p_ref.shape[-1]
    if attn:
        u_ref[...] = u_scr[...]
        nq = q_ref.shape[-1]
        kv = jnp.dot(u_scr[...], w_ref[:, nq + n:], preferred_element_type=F32)
        cos, sa, sb = cos_ref[...], sa_ref[...], sb_ref[...]
        dh = B_HEAD_DIM
        for g in range(B_KV_HEADS):
            sl = slice(g * LANES, (g + 1) * LANES)
            x = _rope(kv[:, sl], cos, sa, sb)
            kv_ref[:, sl] = x.astype(BF16)
            vt_ref[g * dh:(g + 1) * dh, :] = x.T[dh:, :].astype(BF16)
        qf = jnp.dot(u_scr[...], w_ref[:, :nq], preferred_element_type=F32)
        cos, sa, sb = qcos_ref[...], qsa_ref[...], qsb_ref[...]
        for c0 in range(0, nq, LANES):
            q_ref[:, c0:c0 + LANES] = (_rope(qf[:, c0:c0 + LANES], cos, sa, sb) * (ATTN_SCALE * LOG2E)).astype(BF16)
        p_ref[...] = jnp.dot(u_scr[...], w_ref[:, nq:nq + n], preferred_element_type=F32)


def _inproj(h, shift, scale, norm_w, w, n_ctx_tiles, kv_tabs=None, q_tabs=None):
    b, t, d = h.shape
    attn = kv_tabs is not None
    wkv = 2 * B_KV_WIDTH
    n = w.shape[1] - ((wkv + B_WIDTH) if attn else 0)
    sub = PROJ_SUB
    tm = sub * ROW_TILE
    row = lambda width: pl.BlockSpec((None, tm, width), lambda i, j: (i, j, 0))
    in_specs = ([row(d)] + _mod_specs(b, n_ctx_tiles, sub) + _mod_specs(b, n_ctx_tiles, sub)
                + [pl.BlockSpec((1, d), lambda i, j: (0, 0)),
                   pl.BlockSpec(w.shape, lambda i, j: (0, 0), pipeline_mode=pl.Buffered(1))])
    args = [h] + [shift] * sub + [scale] * sub + [norm_w.reshape(1, d), w]
    out_shape = [jax.ShapeDtypeStruct((b, t, n), F32)]
    out_specs = [row(n)]
    if attn:
        in_specs += [pl.BlockSpec((tm, LANES), lambda i, j: (j, 0))] * 6
        args += list(kv_tabs) + list(q_tabs)
        out_shape += [jax.ShapeDtypeStruct((b, t, d), BF16), jax.ShapeDtypeStruct((b, t, wkv), BF16),
                      jax.ShapeDtypeStruct((b, B_KV_WIDTH, t), BF16), jax.ShapeDtypeStruct((b, t, B_WIDTH), BF16)]
        out_specs += [row(d), row(wkv), pl.BlockSpec((None, B_KV_WIDTH, tm), lambda i, j: (i, 0, j)), row(B_WIDTH)]
    res = pl.pallas_call(
        functools.partial(_inproj_kernel, sub=sub, attn=attn),
        grid=(b, t // tm),
        in_specs=in_specs,
        out_specs=out_specs,
        out_shape=out_shape,
        scratch_shapes=[pltpu.VMEM((tm, d), BF16)],
        compiler_params=_params("parallel", "parallel", fuse=[a is w for a in args]),
        name="inproj",
    )(*args)
    return res if attn else res[0]


def _matmul_kernel(u_ref, w_ref, p_ref):
    p_ref[...] = jnp.dot(u_ref[...], w_ref[...], preferred_element_type=F32)


def _matmul(u, w):
    r, d = u.shape
    n = w.shape[1]
    tm = 4 * ROW_TILE
    return pl.pallas_call(
        _matmul_kernel,
        grid=(r // tm,),
        in_specs=[pl.BlockSpec((tm, d), lambda i: (i, 0)),
                  pl.BlockSpec((d, n), lambda i: (0, 0), pipeline_mode=pl.Buffered(1))],
        out_specs=pl.BlockSpec((tm, n), lambda i: (i, 0)),
        out_shape=jax.ShapeDtypeStruct((r, n), F32),
        compiler_params=_params("parallel", fuse=[False, True]),
        name="matmul_tm",
    )(u, w)


def _scan_tile(i, n_ctx, n_lat, rev):
    if not rev:
        return i
    return jnp.where(i < n_ctx, n_ctx - 1 - i, 2 * n_ctx + n_lat - 1 - i)


def _rglru_kernel(*refs, rev, tt, n_ctx, n_lat):
    if rev:
        u_ref, ga_ref, yf_ref, w_ref, bx_ref, ba_ref, la_ref, out_ref, h_scr, a_scr, b_scr = refs
    else:
        (xa_ref, xp_ref, xn1_ref, xn2_ref, cw_ref, cb_ref, w_ref, bx_ref, ba_ref,
         la_ref, out_ref, u_ref, h_scr, xpad_scr, a_scr, b_scr) = refs
    i = pl.program_id(0)
    rows = tt * SUBLANES

    @pl.when(i == 0)
    def _():
        h_scr[...] = jnp.zeros_like(h_scr)

    if not rev:
        tile = _scan_tile(i, n_ctx, n_lat, rev)
        seg_first = jnp.logical_or(tile == 0, tile == n_ctx)
        seg_last = jnp.logical_or(tile == n_ctx - 1, tile == n_ctx + n_lat - 1)
        xpad_scr[0:8, :] = jnp.where(seg_first, 0.0, xp_ref[...])
        xpad_scr[8:8 + rows, :] = xa_ref[...]
        xpad_scr[8 + rows:16 + rows, :] = jnp.where(seg_last, 0.0, xn1_ref[...])
        xpad_scr[16 + rows:24 + rows, :] = jnp.where(seg_last, 0.0, xn2_ref[...])
        u = cb_ref[...] + xpad_scr[0:rows, :] * cw_ref[0:1, :]
        for k in range(1, A_CONV):
            u = u + xpad_scr[8 * k:8 * k + rows, :] * cw_ref[k:k + 1, :]
        u_ref[...] = u

    for hd in range(A_HEADS):
        sl = slice(hd * A_BLOCK, (hd + 1) * A_BLOCK)
        uh = u_ref[:, sl]
        g = jnp.dot(uh.astype(BF16), w_ref[hd], preferred_element_type=F32)
        gate_x = 0.5 * jnp.tanh(g[:, :A_BLOCK] + bx_ref[:, sl]) + 0.5
        lah = la_ref[:, sl]
        log_a = lah + lah * jnp.tanh(g[:, A_BLOCK:] + ba_ref[:, sl])
        a = jnp.exp(log_a)
        a_scr[:, sl] = a
        b_scr[:, sl] = jnp.sqrt(-jnp.tanh(log_a) * (1.0 + a * a)) * gate_x * uh

    def step(s, h):
        t = (tt - 1 - s) if rev else s
        r0 = pl.multiple_of(t * SUBLANES, SUBLANES)
        h = a_scr[pl.ds(r0, SUBLANES), :] * h + b_scr[pl.ds(r0, SUBLANES), :]
        b_scr[pl.ds(r0, SUBLANES), :] = h
        return h

    h_scr[...] = lax.fori_loop(0, tt, step, h_scr[...], unroll=8)
    if rev:
        out_ref[...] = ((yf_ref[...] + b_scr[...]) * _silu(ga_ref[...])).astype(BF16)
    else:
        out_ref[...] = b_scr[...]


def _rglru_pass(p_a, fwd, conv_w, conv_b, w_cat, bx, ba, la, t_ctx, t_lat, rev):
    w = A_WIDTH
    tt = 128
    rows = tt * SUBLANES
    n_ctx, n_lat = t_ctx // tt, t_lat // tt
    t_all = t_ctx + t_lat
    tile = functools.partial(_scan_tile, n_ctx=n_ctx, n_lat=n_lat, rev=rev)
    const = lambda i: (0, 0)
    blk = lambda col: pl.BlockSpec((rows, w), lambda i: (tile(i), col))
    if rev:
        yf, u = fwd
        in_specs = [blk(0), blk(1), blk(0)]
        args = [u, p_a, yf]
    else:
        in_specs = [blk(0),
                    pl.BlockSpec((SUBLANES, w), lambda i: (jnp.maximum(tile(i) * tt - 1, 0), 0)),
                    pl.BlockSpec((SUBLANES, w), lambda i: (jnp.minimum(tile(i) * tt + tt, t_all - 1), 0)),
                    pl.BlockSpec((SUBLANES, w), lambda i: (jnp.minimum(tile(i) * tt + tt + 1, t_all - 1), 0)),
                    pl.BlockSpec((A_CONV, w), const), pl.BlockSpec((1, w), const)]
        args = [p_a, p_a, p_a, p_a, conv_w, conv_b]
    in_specs += [pl.BlockSpec((A_HEADS, A_BLOCK, 2 * A_BLOCK), lambda i: (0, 0, 0)),
                 pl.BlockSpec((1, w), const), pl.BlockSpec((1, w), const), pl.BlockSpec((1, w), const)]
    args += [w_cat, bx, ba, la]
    full = jax.ShapeDtypeStruct((t_all * SUBLANES, w), F32)
    scratch = [pltpu.VMEM((SUBLANES, w), F32)]
    if not rev:
        scratch.append(pltpu.VMEM((rows + 3 * SUBLANES, w), F32))
    scratch += [pltpu.VMEM((rows, w), F32), pltpu.VMEM((rows, w), F32)]
    return pl.pallas_call(
        functools.partial(_rglru_kernel, rev=rev, tt=tt, n_ctx=n_ctx, n_lat=n_lat),
        grid=(n_ctx + n_lat,),
        in_specs=in_specs,
        out_specs=blk(0) if rev else [blk(0), blk(0)],
        out_shape=jax.ShapeDtypeStruct((t_all * SUBLANES, w), BF16) if rev else [full, full],
        scratch_shapes=scratch,
        compiler_params=_params("arbitrary"),
        name="rglru_bwd" if rev else "rglru_fwd",
    )(*args)


def _attn_kernel(sink_ref, q_ref, kc_ref, kp_ref, ko_ref, kn_ref, vc_ref, vp_ref, vo_ref, vn_ref, gb_ref,
                 o_ref, s_scr, p_scr, bias_scr, *, n_ctx_blk, n_lat_blk):
    n = pl.program_id(1)
    is_lat = n >= n_ctx_blk

    @pl.when(is_lat)
    def _():
        _attend(True, n - n_ctx_blk, n_lat_blk, sink_ref, q_ref, (kc_ref, kp_ref, ko_ref, kn_ref),
                (vc_ref, vp_ref, vo_ref, vn_ref), gb_ref, o_ref, s_scr, p_scr, bias_scr)

    @pl.when(jnp.logical_not(is_lat))
    def _():
        _attend(False, None, None, sink_ref, q_ref, (kc_ref,), (vc_ref,), gb_ref, o_ref, s_scr, p_scr, bias_scr)


def _attend(window, m, n_lat_blk, sink_ref, q_ref, k_refs, v_refs, gb_ref, o_ref, s_scr, p_scr, bias_scr):
    w = WINDOW
    dh = B_HEAD_DIM
    lc = k_refs[0].shape[0]

    low = lax.broadcasted_iota(jnp.int32, (w, LANES), 1) < dh
    segs = [(0, lc, None)]
    if window:
        kj = lax.broadcasted_iota(jnp.int32, (w, w), 0)
        qi = lax.broadcasted_iota(jnp.int32, (w, w), 1)
        neg = jnp.full((w, w), NEG_BIG, F32)
        zer = jnp.zeros((w, w), F32)
        bias_scr[0] = jnp.where(jnp.logical_and(m >= 1, kj >= qi), zer, neg)
        bias_scr[1] = jnp.where(jnp.logical_and(m <= n_lat_blk - 2, kj <= qi), zer, neg)
        segs += [(lc, w, 0), (lc + w, w, None), (lc + 2 * w, w, 1)]
    nk = segs[-1][0] + segs[-1][1]
    piece = 32
    pieces = [(r0, bi, p0) for r0, nr, bi in segs for p0 in range(0, nr, piece)]

    def scores(g, cb, r0, bi, p0):
        x = s_scr[g, r0 + p0:r0 + p0 + piece, cb]
        return x if bi is None else x + bias_scr[bi, p0:p0 + piece, :]

    def tree(xs, op):
        while len(xs) > 1:
            xs = [op(xs[i], xs[i + 1]) for i in range(0, len(xs) - 1, 2)] + ([xs[-1]] if len(xs) % 2 else [])
        return xs[0]

    def fold8(x, op):
        return op(x.reshape(x.shape[0] // SUBLANES, SUBLANES, x.shape[1]), axis=0)

    def score_stage(g):
        parts = []
        for c in range(2):
            col0 = g * B_GROUP * dh + c * LANES
            qc = q_ref[:, col0:col0 + LANES].astype(F32)
            parts.append(jnp.where(low, qc, 0.0).astype(BF16))
            parts.append(jnp.where(low, pltpu.roll(qc, dh, 1), 0.0).astype(BF16))
        qs = jnp.concatenate(parts, axis=0)
        ks = slice(g * LANES, (g + 1) * LANES)
        kv_all = jnp.concatenate([r[:, ks] for r in k_refs], axis=0)
        s_scr[g, 0:nk, :] = lax.dot_general(kv_all, qs, _NT, preferred_element_type=F32)

    def softmax_stage(g):
        rdens = []
        for h in range(B_GROUP):
            cb = slice(h * w, (h + 1) * w)
            snk = sink_ref[g * B_GROUP + h] * LOG2E
            mx8 = tree([fold8(scores(g, cb, r0, bi, p0), jnp.max) for r0, bi, p0 in pieces], jnp.maximum)
            mx = jnp.maximum(jnp.max(mx8, axis=0, keepdims=True), snk)
            sums = []
            for r0, bi, p0 in pieces:
                p = jnp.exp2(scores(g, cb, r0, bi, p0) - mx)
                sums.append(fold8(p, jnp.sum))
                p_scr[g, r0 + p0:r0 + p0 + piece, cb] = p.astype(BF16)
            den8 = tree(sums, jnp.add)
            rdens.append(1.0 / (jnp.sum(den8, axis=0, keepdims=True) + jnp.exp2(snk - mx)))
        return rdens

    def value_stage(g, rdens):
        vt_all = jnp.concatenate([r[g * dh:(g + 1) * dh, :] for r in v_refs], axis=1)
        ov = jnp.dot(vt_all, p_scr[g, 0:nk, :], preferred_element_type=F32) * jnp.concatenate(rdens, axis=1)
        for c in range(2):
            pair = jnp.concatenate([ov[:, (2 * c) * w:(2 * c + 1) * w], ov[:, (2 * c + 1) * w:(2 * c + 2) * w]],
                                   axis=0)
            col0 = g * B_GROUP * dh + c * LANES
            y = pair.T * _silu(gb_ref[:, col0:col0 + LANES])
            o_ref[:, col0:col0 + LANES] = y.astype(BF16)

    rd = {}
    for step in range(B_KV_HEADS + 2):
        if step < B_KV_HEADS:
            score_stage(step)
        if 0 <= step - 1 < B_KV_HEADS:
            rd[step - 1] = softmax_stage(step - 1)
        if 0 <= step - 2 < B_KV_HEADS:
            value_stage(step - 2, rd[step - 2])


def _attention(q, gate_b, kvr, vt, sink, t_ctx):
    b, t, _ = q.shape
    w = WINDOW
    n_ctx_blk, n_lat_blk = t_ctx // w, (t - t_ctx) // w
    nb = t // w
    wkv = 2 * B_KV_WIDTH
    prev = lambda n: jnp.maximum(n - 1, 0)
    nxt = lambda n: jnp.minimum(n + 1, nb - 1)
    kvb = lambda f: pl.BlockSpec((None, w, wkv), lambda i, n, s: (i, f(n), 0))
    vtb = lambda f: pl.BlockSpec((None, B_KV_WIDTH, w), lambda i, n, s: (i, 0, f(n)))
    grid_spec = pltpu.PrefetchScalarGridSpec(
        num_scalar_prefetch=1,
        grid=(b, nb),
        in_specs=[pl.BlockSpec((None, w, B_WIDTH), lambda i, n, s: (i, n, 0)),
                  pl.BlockSpec((None, t_ctx, wkv), lambda i, n, s: (i, 0, 0)),
                  kvb(prev), kvb(lambda n: n), kvb(nxt),
                  pl.BlockSpec((None, B_KV_WIDTH, t_ctx), lambda i, n, s: (i, 0, 0)),
                  vtb(prev), vtb(lambda n: n), vtb(nxt),
                  pl.BlockSpec((None, w, B_WIDTH), lambda i, n, s: (i, n, 0))],
        out_specs=pl.BlockSpec((None, w, B_WIDTH), lambda i, n, s: (i, n, 0)),
        scratch_shapes=[pltpu.VMEM((B_KV_HEADS, t_ctx + 3 * w, B_GROUP * w), F32),
                        pltpu.VMEM((B_KV_HEADS, t_ctx + 3 * w, B_GROUP * w), BF16),
                        pltpu.VMEM((2, w, w), F32)],
    )
    return pl.pallas_call(
        functools.partial(_attn_kernel, n_ctx_blk=n_ctx_blk, n_lat_blk=n_lat_blk),
        grid_spec=grid_spec,
        out_shape=jax.ShapeDtypeStruct((b, t, B_WIDTH), BF16),
        compiler_params=_params("parallel", "parallel"),
        name="window_attn",
    )(sink, q, kvr, kvr, kvr, kvr, vt, vt, vt, vt, gate_b)


GLA_DIAG = 4
GLA_LEVELS = (4, 8, 16, 32)
GLA_EXP_KINDS = 2 + len(GLA_LEVELS)
GLA_UNROLL = 2


def _gla_region_ids(rev):
    c = C_CHUNK
    ri = lax.broadcasted_iota(jnp.int32, (c, c), 0)
    ci = lax.broadcasted_iota(jnp.int32, (c, c), 1)
    reg = jnp.full((c, c), len(GLA_LEVELS), jnp.int32)
    for li in range(len(GLA_LEVELS) - 1, -1, -1):
        m = GLA_LEVELS[li]
        reg = jnp.where(ri // m == ci // m, li, reg)
    seen = (ci >= ri) if rev else (ci <= ri)
    return jnp.where(seen, reg, -1)


def _gla_exponent_matrix(rev):
    c = C_CHUNK
    t = np.arange(c)[:, None]
    s = np.arange(c)[None, :]
    upto = (s >= t) if rev else (s <= t)
    after = ~upto
    mats = [upto, after]
    for m in GLA_LEVELS:
        later = ((t // m) % 2 == 0) if rev else ((t // m) % 2 == 1)
        mats.append((t // m == s // m) & np.where(later, upto, after))
    mat = np.concatenate(mats, axis=0).astype(np.float32)
    return np.concatenate([mat, mat], axis=1)


def _grp(x, j):
    return x[j * SUBLANES:(j + 1) * SUBLANES, :]


def _gla_stage_a1(z, qr, lb, emat, rev, lane):
    c = C_CHUNK
    ng = c // SUBLANES

    half = 0.5 * (1.0 - lb)
    ht = half * jnp.tanh(z)
    f = (lb + half) + ht
    kk = half - ht
    ft = jnp.maximum(f, F_MIN)
    q = _silu(qr)

    g2 = jnp.log2(ft)
    g_hi = g2.astype(BF16)
    g_lo = (g2 - g_hi.astype(F32)).astype(BF16)
    expo = jnp.dot(emat, jnp.concatenate([g_hi, g_lo], axis=0), preferred_element_type=F32)

    dparts = []
    for j in range(ng):
        fg, qg, kg = _grp(ft, j), _grp(q, j), _grp(kk, j)
        w_ = kg
        acc = jnp.where(lane == 0, jnp.sum(qg * kg, axis=-1, keepdims=True), 0.0)
        for dlt in range(1, GLA_DIAG):
            w_ = fg * pltpu.roll(w_, (SUBLANES - 1) if rev else 1, 0)
            red = jnp.sum(qg * w_, axis=-1, keepdims=True)
            acc = jnp.where(lane == (dlt if rev else LANES - dlt), red, acc)
        dparts.append(acc)
    diag = pltpu.roll(jnp.concatenate(dparts, axis=0), 0, 1, stride=1, stride_axis=0)[:, :c]
    return q, kk, expo, diag


def _gla_stage_a2(q, kk, expo, diag, v, rev, reg, row8):
    c = C_CHUNK
    ng = c // SUBLANES
    dec = jnp.exp2(expo)
    last = 0 if rev else c - 1
    total = dec[last:last + 1, :]
    qd = (q * dec[0:c]).astype(BF16)
    kd = (kk * dec[c:2 * c]).astype(BF16)
    grp = _grp

    zero = jnp.zeros((SUBLANES, LANES), F32)
    levels = []
    for li, m in enumerate(GLA_LEVELS):
        fac = dec[(2 + li) * c:(3 + li) * c]
        qparts, kparts = [], []
        for j in range(ng):
            if m >= SUBLANES:
                later = ((j * SUBLANES // m) % 2 == 1) != rev
                prod = (grp(q, j) if later else grp(kk, j)) * grp(fac, j)
                qparts.append(prod if later else zero)
                kparts.append(zero if later else prod)
            else:
                later = ((row8 // m) % 2 == 0) if rev else ((row8 // m) % 2 == 1)
                prod = jnp.where(later, grp(q, j), grp(kk, j)) * grp(fac, j)
                qparts.append(jnp.where(later, prod, 0.0))
                kparts.append(jnp.where(later, 0.0, prod))
        levels.append(lax.dot_general(jnp.concatenate(qparts, axis=0).astype(BF16),
                                      jnp.concatenate(kparts, axis=0).astype(BF16), _NT,
                                      preferred_element_type=F32))

    attn = jnp.zeros((c, c), F32)
    for li in range(len(GLA_LEVELS) - 1, -1, -1):
        attn = jnp.where(reg == li + 1, levels[li], attn)
    attn = jnp.where(reg == 0, diag, attn)
    return attn.astype(BF16), qd, kd, v.astype(BF16), total


def _gla_stage_b(attn, qd, kd, vb, total, st):
    o = jnp.dot(attn, vb, preferred_element_type=F32)
    o = o + lax.dot_general(qd, st.astype(BF16), _NT, preferred_element_type=F32)
    st_new = total * st + lax.dot_general(vb, kd, _TN, preferred_element_type=F32)
    return o, st_new


def _gla_kernel(zf_ref, zb_ref, v_ref, q_ref, og_ref, lb_ref, gw_ref, emat_ref, y_ref, of_scr, ob_scr, st_scr,
                attn_c, qd_c, kd_c, vb_c, tot_c, q_c, k_c, expo_c, diag_c,
                *, n_ctx, n_lat, heads):
    c = C_CHUNK
    row8 = lax.broadcasted_iota(jnp.int32, (SUBLANES, LANES), 0)
    lane = lax.broadcasted_iota(jnp.int32, (SUBLANES, LANES), 1)
    regs = (_gla_region_ids(False), _gla_region_ids(True))
    n_all = n_ctx + n_lat
    st_scr[...] = jnp.zeros_like(st_scr)
    streams = [(hh, rev) for hh in range(heads) for rev in (False, True)]

    def rows_of(i, rev):
        if isinstance(i, int):
            cidx = ((n_ctx - 1 - i) if i < n_ctx else (2 * n_ctx + n_lat - 1 - i)) if rev else i
            return pl.ds(cidx * c, c)
        cidx = jnp.where(i < n_ctx, n_ctx - 1 - i, 2 * n_ctx + n_lat - 1 - i) if rev else i
        return pl.ds(pl.multiple_of(cidx * c, c), c)

    def stage_a1(i):
        for k, (hh, rev) in enumerate(streams):
            ls = slice(hh * LANES, (hh + 1) * LANES)
            rows = rows_of(i, rev)
            z_ref = zb_ref if rev else zf_ref
            q, kk, expo, diag = _gla_stage_a1(z_ref[rows, ls], q_ref[rows, ls], lb_ref[:, ls],
                                              emat_ref[int(rev)], rev, lane)
            q_c[k] = q
            k_c[k] = kk
            expo_c[k] = expo
            diag_c[k] = diag

    def stage_a2(i):
        for k, (hh, rev) in enumerate(streams):
            ls = slice(hh * LANES, (hh + 1) * LANES)
            attn, qd, kd, vb, total = _gla_stage_a2(q_c[k], k_c[k], expo_c[k], diag_c[k],
                                                    v_ref[rows_of(i, rev), ls], rev, regs[int(rev)], row8)
            attn_c[k] = attn
            qd_c[k] = qd
            kd_c[k] = kd
            vb_c[k] = vb
            tot_c[k] = jnp.broadcast_to(total, (SUBLANES, LANES))

    def stage_b(i):
        for k, (hh, rev) in enumerate(streams):
            ls = slice(hh * LANES, (hh + 1) * LANES)
            o, st = _gla_stage_b(attn_c[k], qd_c[k], kd_c[k], vb_c[k], tot_c[k, 0:1, :], st_scr[k])
            (ob_scr if rev else of_scr)[rows_of(i, rev), ls] = o
            st_scr[k] = st

    stage_a1(0)
    stage_a2(0)
    stage_a1(1)

    def step(i):
        stage_b(i - 2)
        stage_a2(i - 1)
        stage_a1(i)

    def body(j, carry):
        for r in range(GLA_UNROLL):
            step(2 + GLA_UNROLL * j + r)
        return carry

    trips = (n_all - 2) // GLA_UNROLL
    lax.fori_loop(0, trips, body, 0)
    for i in range(2 + trips * GLA_UNROLL, n_all):
        step(i)
    stage_b(n_all - 2)
    stage_a2(n_all - 1)
    stage_b(n_all - 1)

    gw = gw_ref[...]
    piece = PROJ_SUB * ROW_TILE

    def fin(i, carry):
        rows = pl.ds(pl.multiple_of(i * piece, piece), piece)
        for hh in range(heads):
            ls = slice(hh * LANES, (hh + 1) * LANES)
            o = of_scr[rows, ls] + ob_scr[rows, ls]
            y = o * lax.rsqrt(jnp.mean(o * o, axis=-1, keepdims=True) + RMS_EPS) * gw
            y_ref[rows, ls] = (y * _silu(og_ref[rows, ls])).astype(BF16)
        return carry

    lax.fori_loop(0, (n_all * c) // piece, fin, 0)


def _gla(p, lb, gw, t_ctx):
    b, t, _ = p.shape
    hps = GLA_HEADS_PER_STEP
    wb = hps * LANES
    nblk = C_HEADS // hps
    col = lambda k: pl.BlockSpec((None, t, wb), lambda i, h: (i, 0, k * nblk + h))
    emat = jnp.asarray(np.stack([_gla_exponent_matrix(False), _gla_exponent_matrix(True)]), BF16)
    return pl.pallas_call(
        functools.partial(_gla_kernel, n_ctx=t_ctx // C_CHUNK, n_lat=(t - t_ctx) // C_CHUNK, heads=hps),
        grid=(b, nblk),
        in_specs=[col(0), col(1), col(2), col(3), col(4),
                  pl.BlockSpec((None, 1, wb), lambda i, h: (h, 0, 0)),
                  pl.BlockSpec((1, LANES), lambda i, h: (0, 0)),
                  pl.BlockSpec(emat.shape, lambda i, h: (0, 0, 0))],
        out_specs=pl.BlockSpec((None, t, wb), lambda i, h: (i, 0, h)),
        out_shape=jax.ShapeDtypeStruct((b, t, C_HEADS * LANES), BF16),
        scratch_shapes=[pltpu.VMEM((t, wb), F32), pltpu.VMEM((t, wb), F32),
                        pltpu.VMEM((2 * hps, C_EXPAND, LANES), F32),
                        pltpu.VMEM((2 * hps, C_CHUNK, C_CHUNK), BF16),
                        pltpu.VMEM((2 * hps, C_CHUNK, LANES), BF16),
                        pltpu.VMEM((2 * hps, C_CHUNK, LANES), BF16),
                        pltpu.VMEM((2 * hps, C_CHUNK, LANES), BF16),
                        pltpu.VMEM((2 * hps, SUBLANES, LANES), F32),
                        pltpu.VMEM((2 * hps, C_CHUNK, LANES), F32),
                        pltpu.VMEM((2 * hps, C_CHUNK, LANES), F32),
                        pltpu.VMEM((2 * hps, GLA_EXP_KINDS * C_CHUNK, LANES), F32),
                        pltpu.VMEM((2 * hps, C_CHUNK, C_CHUNK), F32)],
        compiler_params=_params("parallel", "parallel"),
        name="hgrn2_gla",
    )(p, p, p, p, p, lb.reshape(nblk, 1, wb), gw.reshape(1, LANES), emat)


def _outproj_kernel(*refs, n_in, sub, final):
    y_refs = refs[:n_in]
    w_ref, h_ref = refs[n_in:n_in + 2]
    gt_refs = refs[n_in + 2:n_in + 2 + sub]
    if final:
        fw_ref, o_ref = refs[n_in + 2 + sub:]
    else:
        (o_ref,) = refs[n_in + 2 + sub:]
    acc = None
    k0 = 0
    for y_ref in y_refs:
        kw = y_ref.shape[-1]
        part = jnp.dot(y_ref[...], w_ref[k0:k0 + kw, :], preferred_element_type=F32)
        acc = part if acc is None else acc + part
        k0 += kw
    d = acc.shape[-1]
    r = ROW_TILE
    for k in range(sub):
        a3 = acc[k * r:(k + 1) * r, :].reshape(r // SUBLANES, SUBLANES, d)
        hn = h_ref[k * r:(k + 1) * r, :] + (a3 * gt_refs[k][...][None]).reshape(r, d)
        if final:
            hn = hn * lax.rsqrt(jnp.mean(hn * hn, axis=-1, keepdims=True) + RMS_EPS) * fw_ref[...]
        o_ref[k * r:(k + 1) * r, :] = hn


def _outproj(ys, w, h, gate, n_ctx_tiles, final_w=None):
    b, t, d = h.shape
    final = final_w is not None
    sub = 1 if final else PROJ_SUB
    tm = sub * ROW_TILE
    off = n_ctx_tiles if final else 0
    nt = t // tm - off
    in_specs = [pl.BlockSpec((None, tm, y.shape[-1]), lambda i, j: (i, j + off, 0)) for y in ys]
    in_specs += [pl.BlockSpec(w.shape, lambda i, j: (0, 0), pipeline_mode=pl.Buffered(1)),
                 pl.BlockSpec((None, tm, d), lambda i, j: (i, j + off, 0))]
    in_specs += _mod_specs(b, n_ctx_tiles, sub, off)
    args = list(ys) + [w, h] + [gate] * sub
    if final:
        in_specs.append(pl.BlockSpec((1, d), lambda i, j: (0, 0)))
        args.append(final_w.reshape(1, d))
    return pl.pallas_call(
        functools.partial(_outproj_kernel, n_in=len(ys), sub=sub, final=final),
        grid=(b, nt),
        in_specs=in_specs,
        out_specs=pl.BlockSpec((None, tm, d), lambda i, j: (i, j, 0)),
        out_shape=jax.ShapeDtypeStruct((b, nt * tm, d), F32),
        compiler_params=_params("parallel", "parallel", fuse=[a is w for a in args]),
        name="outproj",
    )(*args)


def _rope_tables(s_lat, t_ctx):
    pos = np.arange(s_lat)
    row = (pos // GRID_W).astype(np.float32)
    colp = (pos % GRID_W).astype(np.float32)
    axis_dim = B_HEAD_DIM // 2
    inv = jnp.asarray(ROPE_BASE, F32) ** (-jnp.arange(0, axis_dim, 2, dtype=F32) / axis_dim)
    ang = jnp.concatenate([jnp.asarray(row)[:, None] * inv, jnp.asarray(colp)[:, None] * inv], axis=-1)
    cos, sin = jnp.cos(ang), jnp.sin(ang)
    zer = jnp.zeros_like(sin)
    pairs = lambda even, odd: jnp.stack([even, odd], axis=-1).reshape(s_lat, B_HEAD_DIM)
    cos64 = pairs(cos, cos)
    sa64 = pairs(-sin, zer)
    sb64 = pairs(zer, sin)
    one64 = jnp.ones_like(cos64)
    zer64 = jnp.zeros_like(cos64)

    def full(lat128, ident):
        return jnp.concatenate([jnp.broadcast_to(ident, (t_ctx, LANES)), lat128], axis=0)

    ident_c = jnp.ones((1, LANES), F32)
    ident_s = jnp.zeros((1, LANES), F32)
    q_tabs = (full(jnp.concatenate([cos64, cos64], -1), ident_c),
              full(jnp.concatenate([sa64, sa64], -1), ident_s),
              full(jnp.concatenate([sb64, sb64], -1), ident_s))
    kv_tabs = (full(jnp.concatenate([cos64, one64], -1), ident_c),
               full(jnp.concatenate([sa64, zer64], -1), ident_s),
               full(jnp.concatenate([sb64, zer64], -1), ident_s))
    return q_tabs, kv_tabs


def _even_weights(w_in):
    dh = B_HEAD_DIM
    n_state = A_WIDTH + 2 * B_KV_WIDTH
    q0 = n_state + A_WIDTH
    k0 = A_WIDTH
    v0 = A_WIDTH + B_KV_WIDTH
    parts = [w_in[:, q0:q0 + 2 * B_WIDTH]]
    for j in range(B_KV_HEADS):
        parts += [w_in[:, k0 + j * dh:k0 + (j + 1) * dh], w_in[:, v0 + j * dh:v0 + (j + 1) * dh]]
    w_b = jnp.concatenate(parts, axis=1).astype(BF16)
    w_a = jnp.concatenate([w_in[:, :A_WIDTH], w_in[:, n_state:n_state + A_WIDTH]], axis=1).astype(BF16)
    return w_b, w_a


def _groups(vec_b, vec_c):
    allv = jnp.concatenate([vec_b, vec_c[None, :]], axis=0)
    return jnp.broadcast_to(allv[:, None, :], (allv.shape[0], SUBLANES, allv.shape[1]))


def kernel(x, c, ctx, c_ctx, ada_w, ada_b, norm_w, ev_w_in, ev_conv_w, ev_conv_b, ev_rg_wx, ev_rg_bx,
           ev_rg_wa, ev_rg_ba, ev_rg_lambda, ev_sink, ev_w_out, od_w_in, od_lb_raw, od_gnorm_w, od_w_out,
           final_norm_w):
    b, s_lat, d = x.shape
    t_ctx = ctx.shape[1]
    t_all = t_ctx + s_lat
    assert b == SUBLANES and d == D_MODEL
    assert t_ctx % ROW_TILE == 0 and s_lat % ROW_TILE == 0 and t_all % (PROJ_SUB * ROW_TILE) == 0
    n_ctx_tiles = t_ctx // ROW_TILE

    sc_rows = jnp.zeros((2 * SUBLANES, d), F32)
    sc_rows = sc_rows.at[:b].set(jax.nn.silu(c)).at[b].set(jax.nn.silu(c_ctx))
    mod = _ada_mod(sc_rows, ada_w, ada_b)

    lb_p = jax.nn.softmax(od_lb_raw.astype(F32), axis=0)
    lower_bounds = jnp.cumsum(lb_p, axis=0) - lb_p[0]
    q_tabs, kv_tabs = _rope_tables(s_lat, t_ctx)

    h = jnp.concatenate([ctx, x], axis=1)
    out = None
    for layer in range(DEPTH):
        last = layer == DEPTH - 1
        j = layer // 2
        ml = mod[layer]
        shift = _groups(ml[:b, :d], ml[b, :d])
        scale = _groups(ml[:b, d:2 * d], ml[b, d:2 * d])
        gate = _groups(ml[:b, 2 * d:], ml[b, 2 * d:])
        if layer % 2 == 0:
            w_b, w_a = _even_weights(ev_w_in[j])
            gate_b, u, kvr, vt, q = _inproj(h, shift, scale, norm_w[layer], w_b, n_ctx_tiles,
                                            kv_tabs=kv_tabs, q_tabs=q_tabs)
            u_tm = jnp.transpose(u, (1, 0, 2)).reshape(t_all * b, d)
            p_a = _matmul(u_tm, w_a)
            lah = -0.5 * A_C * jax.nn.softplus(-ev_rg_lambda[j].astype(F32))
            w_cat = (0.5 * jnp.concatenate([ev_rg_wx[j], ev_rg_wa[j]], axis=-1)).astype(BF16)
            rg = lambda dr, yf: _rglru_pass(
                p_a, yf, ev_conv_w[j], ev_conv_b[j].reshape(1, -1), w_cat[dr],
                0.5 * ev_rg_bx[j, dr].reshape(1, -1), 0.5 * ev_rg_ba[j, dr].reshape(1, -1), lah[dr].reshape(1, -1),
                t_ctx, s_lat, rev=bool(dr))
            ya = jnp.transpose(rg(1, rg(0, None)).reshape(t_all, b, A_WIDTH), (1, 0, 2))
            yb = _attention(q, gate_b, kvr, vt, ev_sink[j].astype(F32), t_ctx)
            ys, w_out = [ya, yb], ev_w_out[j].astype(BF16)
        else:
            w_odd = od_w_in[j].at[:, :2 * C_HEADS * C_EXPAND].multiply(0.5).astype(BF16)
            p = _inproj(h, shift, scale, norm_w[layer], w_odd, n_ctx_tiles)
            ys, w_out = [_gla(p, lower_bounds[j], od_gnorm_w[j], t_ctx)], od_w_out[j].astype(BF16)
        if last:
            out = _outproj(ys, w_out, h, gate, n_ctx_tiles, final_w=final_norm_w)
        else:
            h = _outproj(ys, w_out, h, gate, n_ctx_tiles)
    return out
```
